```python
import jax
import jax.numpy as jnp
from jax import lax
import numpy as np

D_MODEL = 1024
BATCH = 32
SEQ = 2048
DEPTH = 4
DEC_BATCH = 32
DEC_SEQ = 32
PAST_LEN = 4096

CHUNK = 64
N_A_LAYERS = DEPTH // 2
N_B_LAYERS = DEPTH - N_A_LAYERS
POOL_WINDOWS = (2, 4, 8, 16)
N_POOL_GROUPS = len(POOL_WINDOWS)
POOL_GROUP_DIM = D_MODEL // N_POOL_GROUPS
POOL_BUF = max(POOL_WINDOWS) - 1
N_HEADS = 16
HEAD_DIM = D_MODEL // N_HEADS
KV_WIDTH = N_HEADS * HEAD_DIM
ATTN_SCALE = HEAD_DIM ** -0.5
QBLOCK = 128
FORGET_BIAS = 2.0
N_EXPERTS = 16
N_EXPERT_GROUPS = 4
EXPERTS_PER_GROUP = N_EXPERTS // N_EXPERT_GROUPS
TOP_K = 2
D_EXPERT = D_MODEL // 2
EPS = 1e-6
NEG_INF = -1e30

kernel_name = 'yoco_pool_fox_moe_stream_step'


def _rms_norm(x, gain):
    xf = x.astype(jnp.float32)
    y = xf * lax.rsqrt(jnp.mean(xf * xf, axis=-1, keepdims=True) + EPS)
    return (y * gain.astype(jnp.float32)).astype(x.dtype)


def _modulate(x, gain, shift, scale):
    return _rms_norm(x, gain) * (1 + scale[:, None, :]) + shift[:, None, :]


def _adaln(c, w, b, n):
    return jnp.split(jax.nn.silu(c) @ w + b, n, axis=-1)


def _pool_mixer(u, hist, start_pos, w_pool, scale):
    B, T, _ = u.shape
    ext = jnp.concatenate([hist.astype(u.dtype), u], axis=1)
    cs = jnp.cumsum(ext.astype(jnp.float32), axis=1)
    cs = jnp.pad(cs, ((0, 0), (1, 0), (0, 0)))
    end = cs[:, POOL_BUF + 1:]
    pos = start_pos + jnp.arange(T)
    means = []
    for g, w in enumerate(POOL_WINDOWS):
        sl = slice(g * POOL_GROUP_DIM, (g + 1) * POOL_GROUP_DIM)
        lo = POOL_BUF + 1 - w
        s = end[..., sl] - cs[:, lo:lo + T, sl]
        cnt = jnp.minimum(pos + 1, w).astype(jnp.float32)[None, :, None]
        means.append(s / cnt)
    pooled = jnp.stack(means, axis=2)
    diff = pooled - u.astype(jnp.float32).reshape(B, T, N_POOL_GROUPS, POOL_GROUP_DIM)
    y = jnp.einsum('btgc,gcd->btgd', diff.astype(u.dtype), w_pool).reshape(B, T, D_MODEL)
    return y * scale, ext[:, -POOL_BUF:]


def _moe(h, w_router, b_router, w_gate, w_up, w_down):
    logits = jnp.einsum('btd,de->bte', h.astype(jnp.float32), w_router.astype(jnp.float32))
    scores = jax.nn.softmax(logits, axis=-1)
    sel = scores + b_router.astype(jnp.float32)
    grouped = sel.reshape(sel.shape[:-1] + (N_EXPERT_GROUPS, EXPERTS_PER_GROUP))
    group_score = jnp.sum(lax.top_k(grouped, TOP_K)[0], axis=-1)
    best = jnp.argmax(group_score, axis=-1)
    in_group = (jnp.arange(N_EXPERTS) // EXPERTS_PER_GROUP) == best[..., None]
    masked = jnp.where(in_group, sel, -jnp.inf)
    _, idx = lax.top_k(masked, TOP_K)
    chosen = jnp.take_along_axis(scores, idx, axis=-1)
    wts = chosen / jnp.sum(chosen, axis=-1, keepdims=True)
    combine = jnp.sum(jax.nn.one_hot(idx, N_EXPERTS, dtype=jnp.float32) * wts[..., None], axis=-2)
    combine = combine.astype(h.dtype)
    y = None
    for e in range(N_EXPERTS):
        act = jax.nn.silu(h @ w_gate[e]) * (h @ w_up[e])
        term = combine[..., e:e + 1] * (act @ w_down[e])
        y = term if y is None else y + term
    return y


def _shared_kv(x, c, past, kv_ada_w, kv_ada_b, kv_norm, w_kvf, b_f, k_norm):
    B, T, _ = x.shape
    shift, scale = _adaln(c, kv_ada_w, kv_ada_b, 2)
    h = _modulate(x, kv_norm, shift, scale)
    proj = h @ w_kvf
    k = _rms_norm(proj[..., :KV_WIDTH].reshape(B, T, N_HEADS, HEAD_DIM), k_norm)
    v = proj[..., KV_WIDTH:2 * KV_WIDTH].reshape(B, T, N_HEADS, HEAD_DIM)
    logf = jax.nn.log_sigmoid((proj[..., 2 * KV_WIDTH:] + b_f).astype(jnp.float32))
    if past is None:
        f_new = jnp.cumsum(logf, axis=1)
        past_segs = []
    else:
        ck, cv, clogf = past
        P = ck.shape[1]
        f_all = jnp.cumsum(jnp.concatenate([clogf.astype(jnp.float32), logf], axis=1), axis=1)
        past_segs = [(ck, cv, f_all[:, :P], jnp.arange(P))]
        f_new = f_all[:, P:]
    return k, v, logf, f_new, past_segs


def _fox_attend(q, fq, pos_q, segments):
    fq_t = jnp.swapaxes(fq, 1, 2)[..., :, None]
    logits, vals = [], []
    for k, v, fk, pos_k in segments:
        s = jnp.einsum('bqhd,bkhd->bhqk', q, k).astype(jnp.float32) * ATTN_SCALE
        s = s + fq_t - jnp.swapaxes(fk, 1, 2)[..., None, :]
        s = jnp.where(pos_k[None, None, None, :] <= pos_q[None, None, :, None], s, NEG_INF)
        logits.append(s)
        vals.append(v)
    p = jax.nn.softmax(jnp.concatenate(logits, axis=-1), axis=-1)
    out = None
    off = 0
    for v in vals:
        n = v.shape[1]
        term = jnp.einsum('bhqk,bkhd->bqhd', p[..., off:off + n].astype(v.dtype), v)
        out = term if out is None else out + term
        off += n
    return out


def _fox_mixer(h, shared, start_pos, w_q, q_norm, w_o):
    B, T, _ = h.shape
    k, v, _, f_new, past_segs = shared
    q = _rms_norm((h @ w_q).reshape(B, T, N_HEADS, HEAD_DIM), q_norm)
    pos = start_pos + jnp.arange(T)
    outs = []
    for i in range(0, T, QBLOCK):
        e = min(i + QBLOCK, T)
        segs = past_segs + [(k[:, :e], v[:, :e], f_new[:, :e], pos[:e])]
        outs.append(_fox_attend(q[:, i:e], f_new[:, i:e], pos[i:e], segs))
    o = jnp.concatenate(outs, axis=1).reshape(B, T, KV_WIDTH)
    return o @ w_o


def _trunk(x, c, pool_hist, start_pos, past, ada_w, ada_b, norm_g, w_pool, pool_scale,
           kv_ada_w, kv_ada_b, kv_norm, w_kvf, b_f, k_norm, w_q, q_norm, w_o,
           w_router, b_router, w_gate, w_up, w_down):
    new_hist = []
    shared = None
    for layer in range(DEPTH):
        sh1, sc1, g1, sh2, sc2, g2 = _adaln(c, ada_w[layer], ada_b[layer], 6)
        if layer < N_A_LAYERS:
            h = _modulate(x, norm_g[layer, 0], sh1, sc1)
            y, hist = _pool_mixer(h, pool_hist[layer], start_pos, w_pool[layer], pool_scale[layer])
            new_hist.append(hist)
        else:
            if shared is None:
                shared = _shared_kv(x, c, past, kv_ada_w, kv_ada_b, kv_norm, w_kvf, b_f, k_norm)
            j = layer - N_A_LAYERS
            h = _modulate(x, norm_g[layer, 0], sh1, sc1)
            y = _fox_mixer(h, shared, start_pos, w_q[j], q_norm[j], w_o[j])
        x = x + g1[:, None, :] * y
        h = _modulate(x, norm_g[layer, 1], sh2, sc2)
        x = x + g2[:, None, :] * _moe(h, w_router, b_router, w_gate[layer], w_up[layer], w_down[layer])
    return x, jnp.stack(new_hist), shared[0], shared[1], shared[2]


def setup_inputs(seed: int = 0) -> dict:
    key = jax.random.key(seed)
    ks = jax.random.split(key, 32)

    def nrm(k, shape, s):
        return s * jax.random.normal(k, shape, jnp.float32)

    return {
        'x_prompt': nrm(ks[0], (BATCH, SEQ, D_MODEL), 1.0),
        'x_sample': nrm(ks[1], (DEC_BATCH, DEC_SEQ, D_MODEL), 1.0),
        'cache_pool': nrm(ks[2], (N_A_LAYERS, DEC_BATCH, POOL_BUF, D_MODEL), 1.0),
        'cache_k': nrm(ks[3], (DEC_BATCH, PAST_LEN, N_HEADS, HEAD_DIM), 1.0),
        'cache_v': nrm(ks[4], (DEC_BATCH, PAST_LEN, N_HEADS, HEAD_DIM), 1.0),
        'cache_logf': jax.nn.log_sigmoid(FORGET_BIAS + nrm(ks[5], (DEC_BATCH, PAST_LEN, N_HEADS), 0.5)),
        'c_prompt': nrm(ks[6], (BATCH, D_MODEL), 1.0),
        'c_sample': nrm(ks[7], (DEC_BATCH, D_MODEL), 1.0),
        'ada_w': nrm(ks[8], (DEPTH, D_MODEL, 6 * D_MODEL), 0.5 * D_MODEL ** -0.5),
        'ada_b': nrm(ks[9], (DEPTH, 6 * D_MODEL), 0.02),
        'norm_g': 1.0 + nrm(ks[10], (DEPTH, 2, D_MODEL), 0.02),
        'w_pool': nrm(ks[11], (N_A_LAYERS, N_POOL_GROUPS, POOL_GROUP_DIM, POOL_GROUP_DIM), POOL_GROUP_DIM ** -0.5),
        'pool_scale': 1.0 + nrm(ks[12], (N_A_LAYERS, D_MODEL), 0.1),
        'kv_ada_w': nrm(ks[13], (D_MODEL, 2 * D_MODEL), 0.5 * D_MODEL ** -0.5),
        'kv_ada_b': nrm(ks[14], (2 * D_MODEL,), 0.02),
        'kv_norm': 1.0 + nrm(ks[15], (D_MODEL,), 0.02),
        'w_kvf': nrm(ks[16], (D_MODEL, 2 * KV_WIDTH + N_HEADS), D_MODEL ** -0.5),
        'b_f': FORGET_BIAS + nrm(ks[17], (N_HEADS,), 0.5),
        'k_norm': 1.0 + nrm(ks[18], (HEAD_DIM,), 0.02),
        'w_q': nrm(ks[19], (N_B_LAYERS, D_MODEL, KV_WIDTH), D_MODEL ** -0.5),
        'q_norm': 1.0 + nrm(ks[20], (N_B_LAYERS, HEAD_DIM), 0.02),
        'w_o': nrm(ks[21], (N_B_LAYERS, KV_WIDTH, D_MODEL), KV_WIDTH ** -0.5),
        'w_router': nrm(ks[22], (D_MODEL, N_EXPERTS), D_MODEL ** -0.5),
        'b_router': nrm(ks[23], (N_EXPERTS,), 0.01),
        'w_gate': nrm(ks[24], (DEPTH, N_EXPERTS, D_MODEL, D_EXPERT), D_MODEL ** -0.5),
        'w_up': nrm(ks[25], (DEPTH, N_EXPERTS, D_MODEL, D_EXPERT), D_MODEL ** -0.5),
        'w_down': nrm(ks[26], (DEPTH, N_EXPERTS, D_EXPERT, D_MODEL), D_EXPERT ** -0.5),
    }


def reference(x_prompt, x_sample, cache_pool, cache_k, cache_v, cache_logf, c_prompt, c_sample,
              ada_w, ada_b, norm_g, w_pool, pool_scale, kv_ada_w, kv_ada_b, kv_norm, w_kvf, b_f,
              k_norm, w_q, q_norm, w_o, w_router, b_router, w_gate, w_up, w_down):
    assert x_sample.shape[1] <= CHUNK
    hist0 = jnp.zeros((N_A_LAYERS, x_prompt.shape[0], POOL_BUF, D_MODEL), x_prompt.dtype)
    y_prompt, pool_prompt, k_prompt, v_prompt, logf_prompt = _trunk(
        x_prompt, c_prompt, hist0, 0, None,
        ada_w, ada_b, norm_g, w_pool, pool_scale, kv_ada_w, kv_ada_b, kv_norm, w_kvf, b_f,
        k_norm, w_q, q_norm, w_o, w_router, b_router, w_gate, w_up, w_down)
    y_sample, pool_sample, k_sample, v_sample, logf_sample = _trunk(
        x_sample, c_sample, cache_pool, cache_k.shape[1], (cache_k, cache_v, cache_logf),
        ada_w, ada_b, norm_g, w_pool, pool_scale, kv_ada_w, kv_ada_b, kv_norm, w_kvf, b_f,
        k_norm, w_q, q_norm, w_o, w_router, b_router, w_gate, w_up, w_down)
    return (y_prompt, y_sample, pool_prompt, k_prompt, v_prompt, logf_prompt,
            pool_sample, k_sample, v_sample, logf_sample)
```

```python
import functools

import jax
import jax.numpy as jnp
from jax import lax
from jax.experimental import pallas as pl
from jax.experimental.pallas import tpu as pltpu
from jax.experimental.pallas import tpu_sc as plsc

F32 = jnp.float32
BF16 = jnp.bfloat16

D_MODEL = 1024
DEPTH = 4
N_A_LAYERS = DEPTH // 2
N_B_LAYERS = DEPTH - N_A_LAYERS
POOL_WINDOWS = (2, 4, 8, 16)
N_POOL_GROUPS = len(POOL_WINDOWS)
POOL_GROUP_DIM = D_MODEL // N_POOL_GROUPS
POOL_BUF = max(POOL_WINDOWS) - 1
HIST_ROWS = POOL_BUF + 1
N_HEADS = 16
HEAD_DIM = D_MODEL // N_HEADS
KV_WIDTH = N_HEADS * HEAD_DIM
ATTN_SCALE = HEAD_DIM ** -0.5
N_EXPERTS = 16
N_EXPERT_GROUPS = 4
EXPERTS_PER_GROUP = N_EXPERTS // N_EXPERT_GROUPS
N_PAIRS = 6
N_CLASSES = N_EXPERT_GROUPS * N_PAIRS
D_EXPERT = D_MODEL // 2
EPS = 1e-6
NEG_INF = -1e30

LANES = 128
SC_CORES = 2
SC_SUBCORES = 16
SC_WINDOW = 64
PAY_W = D_MODEL + LANES
CLS_ROWS = 32
VMEM_LIMIT = 48 * 1024 * 1024


def _cparams(sem):
    return pltpu.CompilerParams(dimension_semantics=sem, vmem_limit_bytes=VMEM_LIMIT)


def _bdot(a, b):
    return jnp.dot(a.astype(BF16), b.astype(BF16), preferred_element_type=F32)


def _split(a):
    hi = a.astype(BF16)
    lo = (a - hi.astype(F32)).astype(BF16)
    return hi, lo


_NN = (((1,), (0,)), ((), ()))
_NT = (((1,), (1,)), ((), ()))


def _dot3(a, b, dims=_NN):
    ah, al = _split(a)
    bh, bl = _split(b)
    d = lambda x, y: lax.dot_general(x, y, dims, preferred_element_type=F32)
    return d(ah, bh) + (d(ah, bl) + d(al, bh))


def _dot2_exact_rhs(a, b_bf16):
    ah, al = _split(a)
    return (jnp.dot(ah, b_bf16, preferred_element_type=F32)
            + jnp.dot(al, b_bf16, preferred_element_type=F32))


def _rms_mod(x, gain, shift, scale):
    ms = jnp.mean(x * x, axis=-1, keepdims=True)
    y = x * lax.rsqrt(ms + EPS) * gain
    return y * (1.0 + scale) + shift


def _head_rms(z, s_ref, st_ref, gain):
    ss = _dot2_exact_rhs(z * z, s_ref[...])
    inv = lax.rsqrt(ss * (1.0 / HEAD_DIM) + EPS)
    invf = _dot2_exact_rhs(inv, st_ref[...])
    return z * invf * gain


def _pad_rows(a, rows):
    if a.shape[0] == rows:
        return a
    return jnp.concatenate([a, jnp.zeros((rows - a.shape[0], a.shape[1]), a.dtype)], axis=0)


def _route(lt, br):
    m = jnp.max(lt, axis=0, keepdims=True)
    p = jnp.exp(lt - m)
    scores = p / jnp.sum(p, axis=0, keepdims=True)
    sel = scores + br
    row = lambda a, e: a[e:e + 1, :]
    gs = []
    for g in range(N_EXPERT_GROUPS):
        v = [row(sel, g * EXPERTS_PER_GROUP + j) for j in range(EXPERTS_PER_GROUP)]
        best = None
        for i in range(EXPERTS_PER_GROUP):
            for j in range(i + 1, EXPERTS_PER_GROUP):
                s = v[i] + v[j]
                best = s if best is None else jnp.maximum(best, s)
        gs.append(best)
    bg = jnp.zeros_like(gs[0])
    bv = gs[0]
    for g in range(1, N_EXPERT_GROUPS):
        better = gs[g] > bv
        bg = jnp.where(better, float(g), bg)
        bv = jnp.where(better, gs[g], bv)

    def in_group(a, j):
        out = row(a, j)
        for g in range(1, N_EXPERT_GROUPS):
            out = jnp.where(bg == float(g), row(a, g * EXPERTS_PER_GROUP + j), out)
        return out

    sg = [in_group(sel, j) for j in range(EXPERTS_PER_GROUP)]
    cg = [in_group(scores, j) for j in range(EXPERTS_PER_GROUP)]

    def first_argmax(vals):
        mx = vals[0]
        for v in vals[1:]:
            mx = jnp.maximum(mx, v)
        idx = jnp.full_like(mx, float(len(vals) - 1))
        for j in range(len(vals) - 2, -1, -1):
            idx = jnp.where(vals[j] == mx, float(j), idx)
        return idx

    i1 = first_argmax(sg)
    i2 = first_argmax([jnp.where(i1 == float(j), -jnp.inf, sg[j]) for j in range(EXPERTS_PER_GROUP)])
    lo = jnp.minimum(i1, i2)
    hi = jnp.maximum(i1, i2)

    def pick(vals, idx):
        out = vals[0]
        for j in range(1, len(vals)):
            out = jnp.where(idx == float(j), vals[j], out)
        return out

    c_lo = pick(cg, lo)
    c_hi = pick(cg, hi)
    tot = c_lo + c_hi
    pair = jnp.where(lo == 0.0, hi - 1.0, jnp.where(lo == 1.0, hi + 1.0, 5.0))
    return bg * float(N_PAIRS) + pair, c_lo / tot, c_hi / tot


def _moe_prep(x1, ng2, sh2, sc2, wrt_ref, br_ref, run_ref, pay_ref, info_ref, cnt_ref):
    tm = x1.shape[0]
    tr = max(tm, LANES)
    h2 = _rms_mod(x1, ng2, sh2, sc2)
    lt = _dot3(wrt_ref[...], _pad_rows(h2, tr), _NT)
    cls, w_lo, w_hi = _route(lt, br_ref[...])

    r = lax.broadcasted_iota(jnp.int32, (LANES, tr), 0)
    wrows = jnp.where(r == 0, w_lo, jnp.where(r == 1, w_hi, 0.0))
    pay_ref[:, :D_MODEL] = h2
    pay_ref[:, D_MODEL:] = wrows.T[:tm, :]

    crow = lax.broadcasted_iota(jnp.int32, (CLS_ROWS, tr), 0).astype(F32)
    lane = lax.broadcasted_iota(jnp.int32, (CLS_ROWS, tr), 1)
    onehot = jnp.where((crow == cls) & (lane < tm), 1.0, 0.0)
    us = lax.broadcasted_iota(jnp.int32, (tr, tr), 0)
    ut = lax.broadcasted_iota(jnp.int32, (tr, tr), 1)
    upper = jnp.where(us < ut, 1.0, 0.0).astype(BF16)
    before = jnp.dot(onehot.astype(BF16), upper, preferred_element_type=F32) + run_ref[:, 0:1]
    rank = jnp.sum(onehot * before, axis=0, keepdims=True)
    run_new = run_ref[...] + jnp.sum(onehot, axis=1, keepdims=True)
    run_ref[...] = run_new
    cnt_ref[...] = run_new
    ir = lax.broadcasted_iota(jnp.int32, (8, tr), 0)
    info = jnp.where(ir == 0, cls, jnp.where(ir == 1, rank, 0.0)).astype(jnp.int32)
    info_ref[...] = info[:, :tm]


def _ada_body(c_ref, w_ref, b_ref, o_ref):
    c = c_ref[...]
    o_ref[...] = _dot3(c * jax.nn.sigmoid(c), w_ref[...]) + b_ref[...]


def _ada_call(c_all, w, b):
    n_l, _, n_out = w.shape
    bc = c_all.shape[0]
    tn = 1536 if n_out % 1536 == 0 else 1024
    return pl.pallas_call(
        _ada_body,
        grid=(n_l, n_out // tn),
        in_specs=[pl.BlockSpec((bc, D_MODEL), lambda l, j: (0, 0)),
                  pl.BlockSpec((None, D_MODEL, tn), lambda l, j: (l, 0, j)),
                  pl.BlockSpec((None, 1, tn), lambda l, j: (l, 0, j))],
        out_specs=pl.BlockSpec((None, bc, tn), lambda l, j: (l, 0, j)),
        out_shape=jax.ShapeDtypeStruct((n_l, bc, n_out), F32),
        compiler_params=_cparams(("arbitrary", "arbitrary")),
        name="adaln",
    )(c_all, w, b.reshape(n_l, 1, n_out))


def _tile_rows(t):
    return 256 if t % 256 == 0 else t


def _tok_spec(tm, width=D_MODEL):
    return pl.BlockSpec((None, tm, width), lambda b, t: (b, t, 0))


def _per_batch_spec(rows, width=D_MODEL):
    return pl.BlockSpec((None, rows, width), lambda b, t: (b, 0, 0))


def _const_spec(shape):
    nd = len(shape)
    return pl.BlockSpec(shape, lambda b, t: (0,) * nd)


def _prep_out(bsz, t, tm):
    shapes = [jax.ShapeDtypeStruct((bsz, t, PAY_W), F32),
              jax.ShapeDtypeStruct((bsz, 8, t), jnp.int32),
              jax.ShapeDtypeStruct((CLS_ROWS, LANES), F32)]
    specs = [_tok_spec(tm, PAY_W),
             pl.BlockSpec((None, 8, tm), lambda b, t: (b, 0, t)),
             _const_spec((CLS_ROWS, LANES))]
    return shapes, specs


def _mixer_body(has_res, tm, start_pos, *refs):
    it = iter(refs)
    x_ref, xp_ref = next(it), next(it)
    if has_res:
        y_ref, yp_ref, g2p_ref = next(it), next(it), next(it)
    hist_ref, ada_ref, ng_ref, wp_ref, ps_ref, wrt_ref, br_ref = (next(it) for _ in range(7))
    x1_ref, pay_ref, info_ref, cnt_ref, hout_ref = (next(it) for _ in range(5))
    run_ref, ue_ref = next(it), next(it)

    b = pl.program_id(0)
    t = pl.program_id(1)

    @pl.when((b == 0) & (t == 0))
    def _():
        run_ref[...] = jnp.zeros_like(run_ref)

    xin = x_ref[...]
    xp = xp_ref[...]
    if has_res:
        g2p = g2p_ref[...]
        xin = xin + g2p * y_ref[...]
        xp = xp + g2p * yp_ref[...]
    ada = ada_ref[...]
    sh1, sc1, g1, sh2, sc2 = (ada[i:i + 1] for i in range(5))
    ng = ng_ref[...]
    u = _rms_mod(xin, ng[0:1], sh1, sc1)
    up = _rms_mod(xp, ng[0:1], sh1, sc1)
    up = jnp.where(t == 0, hist_ref[...], up)
    ue_ref[0:HIST_ROWS, :] = up
    ue_ref[HIST_ROWS:, :] = u

    pos = start_pos + t * tm + lax.broadcasted_iota(jnp.int32, (tm, 1), 0)
    cols = []
    for g, w in enumerate(POOL_WINDOWS):
        sl = slice(g * POOL_GROUP_DIM, (g + 1) * POOL_GROUP_DIM)
        s = u[:, sl]
        for j in range(1, w):
            s = s + ue_ref[HIST_ROWS - j:HIST_ROWS - j + tm, sl]
        cnt = jnp.minimum(pos + 1, w).astype(F32)
        cols.append(_bdot(s / cnt - u[:, sl], wp_ref[g]))
    y = jnp.concatenate(cols, axis=1) * ps_ref[...]
    x1 = xin + g1 * y
    x1_ref[...] = x1
    hout_ref[...] = ue_ref[tm + 1:tm + HIST_ROWS, :]
    _moe_prep(x1, ng[1:2], sh2, sc2, wrt_ref, br_ref, run_ref, pay_ref, info_ref, cnt_ref)


def _mixer_call(x, res, hist16, ada_l, ng_l, wp_l, ps_l, wrt, br, start_pos):
    bsz, t, _ = x.shape
    tm = _tile_rows(t)
    prev_spec = pl.BlockSpec((None, HIST_ROWS, D_MODEL),
                             lambda b, i: (b, jnp.maximum(i * (tm // HIST_ROWS) - 1, 0), 0))
    ins = [x, x]
    specs = [_tok_spec(tm), prev_spec]
    if res is not None:
        y, g2p = res
        ins += [y, y, g2p]
        specs += [_tok_spec(tm), prev_spec, _per_batch_spec(1)]
    ins += [hist16, ada_l, ng_l, wp_l, ps_l, wrt, br]
    specs += [_per_batch_spec(HIST_ROWS), _per_batch_spec(6), _const_spec((2, D_MODEL)),
              _const_spec((N_POOL_GROUPS, POOL_GROUP_DIM, POOL_GROUP_DIM)), _const_spec((1, D_MODEL)),
              _const_spec((N_EXPERTS, D_MODEL)), _const_spec((N_EXPERTS, 1))]
    p_shapes, p_specs = _prep_out(bsz, t, tm)
    return pl.pallas_call(
        functools.partial(_mixer_body, res is not None, tm, start_pos),
        grid=(bsz, t // tm),
        in_specs=specs,
        out_specs=[_tok_spec(tm)] + p_specs + [_per_batch_spec(POOL_BUF)],
        out_shape=[jax.ShapeDtypeStruct((bsz, t, D_MODEL), F32)] + p_shapes
                  + [jax.ShapeDtypeStruct((bsz, POOL_BUF, D_MODEL), F32)],
        scratch_shapes=[pltpu.VMEM((CLS_ROWS, LANES), F32), pltpu.VMEM((tm + HIST_ROWS, D_MODEL), F32)],
        compiler_params=_cparams(("arbitrary", "arbitrary")),
        name="pool_mixer",
    )(*ins)


def _proj_body(with_kv, tm, *refs):
    it = iter(refs)
    x1p_ref, y_ref, g2p_ref, ada_ref, ng_ref, wq_ref, qn_ref, s_ref, st_ref = (next(it) for _ in range(9))
    if with_kv:
        kva_ref, kvn_ref, wkv_ref, wf_ref, bf_ref, kn_ref = (next(it) for _ in range(6))
    x_ref, q_ref = next(it), next(it)
    if with_kv:
        k_ref, v_ref, kb_ref, vb_ref, lf_ref, lft_ref = (next(it) for _ in range(6))

    x = x1p_ref[...] + g2p_ref[...] * y_ref[...]
    x_ref[...] = x
    ada = ada_ref[...]
    ng = ng_ref[...]
    h = _rms_mod(x, ng[0:1], ada[0:1], ada[1:2])
    q = _head_rms(_bdot(h, wq_ref[...]), s_ref, st_ref, qn_ref[...])
    q_ref[...] = (q * ATTN_SCALE).astype(BF16)
    if with_kv:
        kva = kva_ref[...]
        hk = _rms_mod(x, kvn_ref[...], kva[0:1], kva[1:2])
        proj = _bdot(hk, wkv_ref[...])
        k = _head_rms(proj[:, :KV_WIDTH], s_ref, st_ref, kn_ref[...])
        v = proj[:, KV_WIDTH:]
        k_ref[...] = k
        v_ref[...] = v
        kb_ref[...] = k.astype(BF16)
        vb_ref[...] = v.astype(BF16)
        z = _dot3(hk, wf_ref[...]) + bf_ref[...]
        lf = jnp.minimum(z, 0.0) - jnp.log(1.0 + jnp.exp(-jnp.abs(z)))
        lf_ref[...] = lf[:, :N_HEADS]
        lft_ref[...] = _pad_rows(lf, max(tm, LANES)).T[:N_HEADS, :tm]


def _proj_call(x1p, y, g2p, ada_l, ng_l, wq, qn, s_mat, st_mat, kv=None):
    bsz, t, _ = x1p.shape
    tm = _tile_rows(t)
    ins = [x1p, y, g2p, ada_l, ng_l, wq, qn, s_mat, st_mat]
    specs = [_tok_spec(tm), _tok_spec(tm), _per_batch_spec(1), _per_batch_spec(6), _const_spec((2, D_MODEL)),
             _const_spec((D_MODEL, KV_WIDTH)), _const_spec((1, D_MODEL)),
             _const_spec((D_MODEL, LANES)), _const_spec((LANES, D_MODEL))]
    out_shapes = [jax.ShapeDtypeStruct((bsz, t, D_MODEL), F32), jax.ShapeDtypeStruct((bsz, t, D_MODEL), BF16)]
    out_specs = [_tok_spec(tm), _tok_spec(tm)]
    if kv is not None:
        kva, kvn, wkv, wf, bf, kn = kv
        ins += [kva, kvn, wkv, wf, bf, kn]
        specs += [_per_batch_spec(2), _const_spec((1, D_MODEL)), _const_spec((D_MODEL, 2 * KV_WIDTH)),
                  _const_spec((D_MODEL, LANES)), _const_spec((1, LANES)), _const_spec((1, D_MODEL))]
        out_shapes += [jax.ShapeDtypeStruct((bsz, t, D_MODEL), F32)] * 2
        out_shapes += [jax.ShapeDtypeStruct((bsz, t, D_MODEL), BF16)] * 2
        out_shapes += [jax.ShapeDtypeStruct((bsz, t, N_HEADS), F32), jax.ShapeDtypeStruct((bsz, N_HEADS, t), F32)]
        out_specs += [_tok_spec(tm)] * 4
        out_specs += [_tok_spec(tm, N_HEADS), pl.BlockSpec((None, N_HEADS, tm), lambda b, i: (b, 0, i))]
    return pl.pallas_call(
        functools.partial(_proj_body, kv is not None, tm),
        grid=(bsz, t // tm),
        in_specs=specs, out_specs=out_specs, out_shape=out_shapes,
        compiler_params=_cparams(("arbitrary", "arbitrary")),
        name="qkv_proj" if kv is not None else "q_proj",
    )(*ins)


def _cumsum_body(n_chunks, lf_ref, o_ref):
    us = lax.broadcasted_iota(jnp.int32, (LANES, LANES), 0)
    ut = lax.broadcasted_iota(jnp.int32, (LANES, LANES), 1)
    upper = jnp.where(us <= ut, 1.0, 0.0).astype(BF16)

    def step(c, carry):
        off = pl.multiple_of(c * LANES, LANES)
        chunk = lf_ref[:, pl.ds(off, LANES)]
        o_ref[:, pl.ds(off, LANES)] = _dot2_exact_rhs(chunk, upper) + carry
        return carry + jnp.sum(chunk, axis=1, keepdims=True)

    lax.fori_loop(0, n_chunks, step, jnp.zeros((N_HEADS, 1), F32))


def _cumsum_call(lft):
    bsz, _, tk = lft.shape
    spec = pl.BlockSpec((None, N_HEADS, tk), lambda b: (b, 0, 0))
    return pl.pallas_call(
        functools.partial(_cumsum_body, tk // LANES),
        grid=(bsz,), in_specs=[spec], out_specs=spec,
        out_shape=jax.ShapeDtypeStruct(lft.shape, F32),
        compiler_params=_cparams(("arbitrary",)),
        name="logf_cumsum",
    )(lft)


def _flash_body(tq, tk, q_off, n_k, q_ref, k_ref, v_ref, f_ref, o_ref, m_ref, l_ref, acc_ref):
    qi = pl.program_id(1)
    kj = pl.program_id(2)

    @pl.when(kj == 0)
    def _():
        m_ref[...] = jnp.full_like(m_ref, NEG_INF)
        l_ref[...] = jnp.zeros_like(l_ref)
        acc_ref[...] = jnp.zeros_like(acc_ref)

    last = jnp.minimum((q_off + (qi + 1) * tq - 1) // tk, n_k - 1)

    @pl.when(kj <= last)
    def _():
        qpos = q_off + qi * tq + lax.broadcasted_iota(jnp.int32, (tq, 1), 0)
        kpos = kj * tk + lax.broadcasted_iota(jnp.int32, (1, tk), 1)
        visible = kpos <= qpos
        lane = lax.broadcasted_iota(jnp.int32, (1, LANES), 1)

        def pair(hp, carry):
            off = pl.multiple_of(hp * LANES, LANES)
            q = q_ref[:, pl.ds(off, LANES)]
            k = k_ref[:, pl.ds(off, LANES)]
            v = v_ref[:, pl.ds(off, LANES)]
            for hh in range(2):
                h = 2 * hp + hh
                in_head = (lane >= hh * HEAD_DIM) & (lane < (hh + 1) * HEAD_DIM)
                qm = jnp.where(in_head, q, jnp.zeros_like(q))
                s = lax.dot_general(qm, k, _NT, preferred_element_type=F32)
                s = jnp.where(visible, s - f_ref[pl.ds(h, 1), :], NEG_INF)
                m_old = m_ref[h]
                m_new = jnp.maximum(m_old, jnp.max(s, axis=-1, keepdims=True))
                alpha = jnp.exp(m_old - m_new)
                p = jnp.exp(s - m_new)
                l_ref[h] = alpha * l_ref[h] + jnp.sum(p, axis=-1, keepdims=True)
                m_ref[h] = m_new
                pv = jnp.dot(p.astype(BF16), v, preferred_element_type=F32)
                cur = acc_ref[:, pl.ds(off, LANES)]
                acc_ref[:, pl.ds(off, LANES)] = jnp.where(in_head, alpha * cur + pv, cur)
            return carry

        lax.fori_loop(0, N_HEADS // 2, pair, 0)

    @pl.when(kj == n_k - 1)
    def _():
        lane = lax.broadcasted_iota(jnp.int32, (1, LANES), 1)
        for hp in range(N_HEADS // 2):
            sl = slice(hp * LANES, (hp + 1) * LANES)
            denom = jnp.where(lane < HEAD_DIM, l_ref[2 * hp], l_ref[2 * hp + 1])
            o_ref[:, sl] = (acc_ref[:, sl] / denom).astype(BF16)


def _flash_call(q, kb, vb, fk, q_off):
    bsz, t_q, _ = q.shape
    t_k = kb.shape[1]
    tq = _tile_rows(t_q)
    tk = 512
    n_k = t_k // tk

    def kmap(b, i, j):
        return (b, jnp.minimum(j, jnp.minimum((q_off + (i + 1) * tq - 1) // tk, n_k - 1)), 0)

    def fmap(b, i, j):
        return (b, 0, jnp.minimum(j, jnp.minimum((q_off + (i + 1) * tq - 1) // tk, n_k - 1)))

    return pl.pallas_call(
        functools.partial(_flash_body, tq, tk, q_off, n_k),
        grid=(bsz, t_q // tq, n_k),
        in_specs=[pl.BlockSpec((None, tq, D_MODEL), lambda b, i, j: (b, i, 0)),
                  pl.BlockSpec((None, tk, D_MODEL), kmap),
                  pl.BlockSpec((None, tk, D_MODEL), kmap),
                  pl.BlockSpec((None, N_HEADS, tk), fmap)],
        out_specs=pl.BlockSpec((None, tq, D_MODEL), lambda b, i, j: (b, i, 0)),
        out_shape=jax.ShapeDtypeStruct((bsz, t_q, D_MODEL), BF16),
        scratch_shapes=[pltpu.VMEM((N_HEADS, tq, 1), F32), pltpu.VMEM((N_HEADS, tq, 1), F32),
                        pltpu.VMEM((tq, D_MODEL), F32)],
        compiler_params=_cparams(("arbitrary", "arbitrary", "arbitrary")),
        name="fox_attention",
    )(q, kb, vb, fk)


def _oproj_body(x_ref, o_ref, wo_ref, ada_ref, ng_ref, wrt_ref, br_ref,
                x1_ref, pay_ref, info_ref, cnt_ref, run_ref):
    @pl.when((pl.program_id(0) == 0) & (pl.program_id(1) == 0))
    def _():
        run_ref[...] = jnp.zeros_like(run_ref)

    ada = ada_ref[...]
    x1 = x_ref[...] + ada[2:3] * jnp.dot(o_ref[...], wo_ref[...], preferred_element_type=F32)
    x1_ref[...] = x1
    _moe_prep(x1, ng_ref[...][1:2], ada[3:4], ada[4:5], wrt_ref, br_ref, run_ref, pay_ref, info_ref, cnt_ref)


def _oproj_call(x, o, wo, ada_l, ng_l, wrt, br):
    bsz, t, _ = x.shape
    tm = _tile_rows(t)
    p_shapes, p_specs = _prep_out(bsz, t, tm)
    return pl.pallas_call(
        _oproj_body,
        grid=(bsz, t // tm),
        in_specs=[_tok_spec(tm), _tok_spec(tm), _const_spec((KV_WIDTH, D_MODEL)), _per_batch_spec(6),
                  _const_spec((2, D_MODEL)), _const_spec((N_EXPERTS, D_MODEL)), _const_spec((N_EXPERTS, 1))],
        out_specs=[_tok_spec(tm)] + p_specs,
        out_shape=[jax.ShapeDtypeStruct((bsz, t, D_MODEL), F32)] + p_shapes,
        scratch_shapes=[pltpu.VMEM((CLS_ROWS, LANES), F32)],
        compiler_params=_cparams(("arbitrary", "arbitrary")),
        name="attn_out_proj",
    )(x, o, wo, ada_l, ng_l, wrt, br)


def _final_body(x_ref, y_ref, g_ref, o_ref):
    o_ref[...] = x_ref[...] + g_ref[...] * y_ref[...]


def _final_call(x1, y, g2):
    bsz, t, _ = x1.shape
    tm = _tile_rows(t)
    return pl.pallas_call(
        _final_body, grid=(bsz, t // tm),
        in_specs=[_tok_spec(tm), _tok_spec(tm), _per_batch_spec(1)],
        out_specs=_tok_spec(tm),
        out_shape=jax.ShapeDtypeStruct(x1.shape, F32),
        compiler_params=_cparams(("arbitrary", "arbitrary")),
        name="final_residual",
    )(x1, y, g2)


def _sc_worker_loop(n_win, fn):
    wid = lax.axis_index("s") * SC_CORES + lax.axis_index("c")
    n_workers = SC_CORES * SC_SUBCORES

    @pl.loop(0, pl.cdiv(n_win, n_workers))
    def _(j):
        win = j * n_workers + wid

        @pl.when(win < n_win)
        def _():
            fn(pl.multiple_of(win * SC_WINDOW, SC_WINDOW))


def _sc_scatter_rows(rows, idx, n_out):
    n, width = rows.shape
    mesh = plsc.VectorSubcoreMesh(core_axis_name="c", subcore_axis_name="s")

    @functools.partial(
        pl.kernel, mesh=mesh, out_type=jax.ShapeDtypeStruct((n_out, width), rows.dtype),
        scratch_types=[pltpu.VMEM((SC_WINDOW,), jnp.int32), pltpu.VMEM((SC_WINDOW, width), rows.dtype)])
    def k(rows_hbm, idx_hbm, out_hbm, idx_v, rows_v):
        def one(base):
            pltpu.sync_copy(idx_hbm.at[pl.ds(base, SC_WINDOW)], idx_v)
            pltpu.sync_copy(rows_hbm.at[pl.ds(base, SC_WINDOW)], rows_v)
            pltpu.sync_copy(rows_v, out_hbm.at[idx_v])
        _sc_worker_loop(n // SC_WINDOW, one)

    return k(rows, idx)


def _sc_gather_rows(table, idx):
    n = idx.shape[0]
    width = table.shape[1]
    mesh = plsc.VectorSubcoreMesh(core_axis_name="c", subcore_axis_name="s")

    @functools.partial(
        pl.kernel, mesh=mesh, out_type=jax.ShapeDtypeStruct((n, width), table.dtype),
        scratch_types=[pltpu.VMEM((SC_WINDOW,), jnp.int32), pltpu.VMEM((SC_WINDOW, width), table.dtype)])
    def k(table_hbm, idx_hbm, out_hbm, idx_v, rows_v):
        def one(base):
            pltpu.sync_copy(idx_hbm.at[pl.ds(base, SC_WINDOW)], idx_v)
            pltpu.sync_copy(table_hbm.at[idx_v], rows_v)
            pltpu.sync_copy(rows_v, out_hbm.at[pl.ds(base, SC_WINDOW)])
        _sc_worker_loop(n // SC_WINDOW, one)

    return k(table, idx)


def _moe_body(e1_ref, e2_ref, na_ref, hs_ref, g1_ref, g2_ref, u1_ref, u2_ref, d1_ref, d2_ref, o_ref):
    @pl.when(pl.program_id(0) < na_ref[0])
    def _():
        blk = hs_ref[...]
        h = blk[:, :D_MODEL].astype(BF16)
        y = None
        for lane, (g_ref, u_ref, d_ref) in enumerate(((g1_ref, u1_ref, d1_ref), (g2_ref, u2_ref, d2_ref))):
            gate = jnp.dot(h, g_ref[...], preferred_element_type=F32)
            up = jnp.dot(h, u_ref[...], preferred_element_type=F32)
            w = blk[:, D_MODEL + lane:D_MODEL + lane + 1]
            act = (gate * jax.nn.sigmoid(gate) * up * w).astype(BF16)
            term = jnp.dot(act, d_ref[...], preferred_element_type=F32)
            y = term if y is None else y + term
        o_ref[...] = y


def _moe_call(hs, tile_e1, tile_e2, n_active, wg, wu, wd, layer, tm):
    n_s = hs.shape[0]
    n_tiles = n_s // tm
    base = layer * N_EXPERTS

    def row_map(i, e1, e2, na):
        return (jnp.minimum(i, na[0] - 1), 0)

    def w_map(which):
        def m(i, e1, e2, na):
            e = (e1, e2)[which]
            return (base + e[jnp.minimum(i, na[0] - 1)], 0, 0)
        return m

    gu = lambda which: pl.BlockSpec((None, D_MODEL, D_EXPERT), w_map(which))
    dn = lambda which: pl.BlockSpec((None, D_EXPERT, D_MODEL), w_map(which))
    return pl.pallas_call(
        _moe_body,
        grid_spec=pltpu.PrefetchScalarGridSpec(
            num_scalar_prefetch=3, grid=(n_tiles,),
            in_specs=[pl.BlockSpec((tm, PAY_W), row_map), gu(0), gu(1), gu(0), gu(1), dn(0), dn(1)],
            out_specs=pl.BlockSpec((tm, D_MODEL), row_map)),
        out_shape=jax.ShapeDtypeStruct((n_s, D_MODEL), F32),
        compiler_params=_cparams(("arbitrary",)),
        name="grouped_experts",
    )(tile_e1, tile_e2, n_active, hs, wg, wg, wu, wu, wd, wd)


_PAIR_LO = (0, 0, 0, 1, 1, 2)
_PAIR_HI = (1, 2, 3, 2, 3, 3)


def _moe_layer(pay, info, counts, wg, wu, wd, layer):
    bsz, t, _ = pay.shape
    n = bsz * t
    tm = 256
    n_s = ((n + N_CLASSES * (tm - 1)) // tm + 1) * tm
    n_tiles = n_s // tm
    cls = info[:, 0, :].reshape(n)
    rank = info[:, 1, :].reshape(n)
    cnt = counts[:N_CLASSES, 0].astype(jnp.int32)
    padded = ((cnt + tm - 1) // tm) * tm
    ends = jnp.cumsum(padded)
    starts = ends - padded
    dest = starts[cls] + rank
    tile_cls = jnp.minimum(jnp.searchsorted(ends, jnp.arange(n_tiles, dtype=jnp.int32) * tm, side="right"),
                           N_CLASSES - 1).astype(jnp.int32)
    grp = tile_cls // N_PAIRS
    pr = tile_cls % N_PAIRS
    tile_e1 = grp * EXPERTS_PER_GROUP + jnp.asarray(_PAIR_LO, jnp.int32)[pr]
    tile_e2 = grp * EXPERTS_PER_GROUP + jnp.asarray(_PAIR_HI, jnp.int32)[pr]
    n_active = (ends[-1:] // tm).astype(jnp.int32)
    hs = _sc_scatter_rows(pay.reshape(n, PAY_W), dest, n_s)
    ys = _moe_call(hs, tile_e1, tile_e2, n_active, wg, wu, wd, layer, tm)
    return _sc_gather_rows(ys, dest).reshape(bsz, t, D_MODEL)


def _trunk(x, ada, kva, hist, past, prm):
    bsz, t, _ = x.shape
    start_pos = 0 if past is None else past[0].shape[1]
    wrt, br = prm["wrt"], prm["br"]
    res = None
    new_hist = []
    for layer in range(N_A_LAYERS):
        if hist is None:
            h16 = jnp.zeros((bsz, HIST_ROWS, D_MODEL), F32)
        else:
            h16 = jnp.pad(hist[layer], ((0, 0), (1, 0), (0, 0)))
        x, pay, info, counts, hout = _mixer_call(
            x, res, h16, ada[layer], prm["norm_g"][layer], prm["w_pool"][layer], prm["pool_scale"][layer],
            wrt, br, start_pos)
        new_hist.append(hout)
        y = _moe_layer(pay, info, counts, prm["w_gate"], prm["w_up"], prm["w_down"], layer)
        res = (y, ada[layer][:, 5:6, :])

    shared = None
    for j in range(N_B_LAYERS):
        layer = N_A_LAYERS + j
        y, g2p = res
        if j == 0:
            kv = (kva, prm["kv_norm"], prm["w_kv"], prm["w_f"], prm["b_f"], prm["k_norm"])
            xr, q, k, v, kb, vb, logf, lft = _proj_call(
                x, y, g2p, ada[layer], prm["norm_g"][layer], prm["w_q"][j], prm["q_norm"][j],
                prm["s_mat"], prm["st_mat"], kv)
            if past is None:
                fk = _cumsum_call(lft)
            else:
                ck, cv, clogf = past
                n_past = ck.shape[1]
                t_k = -(-(n_past + t) // 512) * 512
                pad = t_k - n_past - t
                lft = jnp.concatenate([jnp.swapaxes(clogf, 1, 2), lft, jnp.zeros((bsz, N_HEADS, pad), F32)], axis=2)
                fk = _cumsum_call(lft)
                zpad = jnp.zeros((bsz, pad, D_MODEL), BF16)
                kb = jnp.concatenate([ck.reshape(bsz, n_past, D_MODEL).astype(BF16), kb, zpad], axis=1)
                vb = jnp.concatenate([cv.reshape(bsz, n_past, D_MODEL).astype(BF16), vb, zpad], axis=1)
            shared = (k, v, logf, kb, vb, fk)
        else:
            xr, q = _proj_call(x, y, g2p, ada[layer], prm["norm_g"][layer], prm["w_q"][j], prm["q_norm"][j],
                               prm["s_mat"], prm["st_mat"])
        o = _flash_call(q, shared[3], shared[4], shared[5], start_pos)
        x, pay, info, counts = _oproj_call(xr, o, prm["w_o"][j], ada[layer], prm["norm_g"][layer], wrt, br)
        y = _moe_layer(pay, info, counts, prm["w_gate"], prm["w_up"], prm["w_down"], layer)
        res = (y, ada[layer][:, 5:6, :])

    out = _final_call(x, res[0], res[1])
    k, v, logf = shared[:3]
    return (out, jnp.stack(new_hist), k.reshape(bsz, t, N_HEADS, HEAD_DIM),
            v.reshape(bsz, t, N_HEADS, HEAD_DIM), logf)


def kernel(x_prompt, x_sample, cache_pool, cache_k, cache_v, cache_logf, c_prompt, c_sample, ada_w, ada_b, norm_g, w_pool, pool_scale, kv_ada_w, kv_ada_b, kv_norm, w_kvf, b_f, k_norm, w_q, q_norm, w_o, w_router, b_router, w_gate, w_up, w_down):
    bp = x_prompt.shape[0]
    c_all = jnp.concatenate([c_prompt, c_sample], axis=0)
    bc = c_all.shape[0]
    ada = _ada_call(c_all, ada_w, ada_b).reshape(DEPTH, bc, 6, D_MODEL)
    kva = _ada_call(c_all, kv_ada_w[None], kv_ada_b[None]).reshape(bc, 2, D_MODEL)

    head_of_lane = jnp.arange(D_MODEL, dtype=jnp.int32) // HEAD_DIM
    s_mat = (head_of_lane[:, None] == jnp.arange(LANES, dtype=jnp.int32)[None, :]).astype(BF16)
    prm = {
        "norm_g": norm_g,
        "w_pool": w_pool.astype(BF16),
        "pool_scale": pool_scale.reshape(N_A_LAYERS, 1, D_MODEL),
        "kv_norm": kv_norm.reshape(1, D_MODEL),
        "w_kv": w_kvf[:, :2 * KV_WIDTH].astype(BF16),
        "w_f": jnp.pad(w_kvf[:, 2 * KV_WIDTH:], ((0, 0), (0, LANES - N_HEADS))),
        "b_f": jnp.pad(b_f, (0, LANES - N_HEADS)).reshape(1, LANES),
        "k_norm": jnp.tile(k_norm, N_HEADS).reshape(1, D_MODEL),
        "w_q": w_q.astype(BF16),
        "q_norm": jnp.tile(q_norm, (1, N_HEADS)).reshape(N_B_LAYERS, 1, D_MODEL),
        "w_o": w_o.astype(BF16),
        "wrt": w_router.T,
        "br": b_router.reshape(N_EXPERTS, 1),
        "w_gate": w_gate.astype(BF16).reshape(DEPTH * N_EXPERTS, D_MODEL, D_EXPERT),
        "w_up": w_up.astype(BF16).reshape(DEPTH * N_EXPERTS, D_MODEL, D_EXPERT),
        "w_down": w_down.astype(BF16).reshape(DEPTH * N_EXPERTS, D_EXPERT, D_MODEL),
        "s_mat": s_mat,
        "st_mat": s_mat.T,
    }
    outs_p = _trunk(x_prompt, ada[:, :bp], kva[:bp], None, None, prm)
    outs_s = _trunk(x_sample, ada[:, bp:], kva[bp:], cache_pool, (cache_k, cache_v, cache_logf), prm)
    return (outs_p[0], outs_s[0]) + outs_p[1:] + outs_s[1:]
```

```python
import functools

import jax
import jax.numpy as jnp
from jax import lax
from jax.experimental import pallas as pl
from jax.experimental.pallas import tpu as pltpu
from jax.experimental.pallas import tpu_sc as plsc

F32 = jnp.float32
BF16 = jnp.bfloat16

D_MODEL = 1024
DEPTH = 4
N_A_LAYERS = DEPTH // 2
N_B_LAYERS = DEPTH - N_A_LAYERS
POOL_WINDOWS = (2, 4, 8, 16)
N_POOL_GROUPS = len(POOL_WINDOWS)
POOL_GROUP_DIM = D_MODEL // N_POOL_GROUPS
POOL_BUF = max(POOL_WINDOWS) - 1
HIST_ROWS = POOL_BUF + 1
N_HEADS = 16
HEAD_DIM = D_MODEL // N_HEADS
KV_WIDTH = N_HEADS * HEAD_DIM
ATTN_SCALE = HEAD_DIM ** -0.5
N_EXPERTS = 16
N_EXPERT_GROUPS = 4
EXPERTS_PER_GROUP = N_EXPERTS // N_EXPERT_GROUPS
N_PAIRS = 6
N_CLASSES = N_EXPERT_GROUPS * N_PAIRS
D_EXPERT = D_MODEL // 2
EPS = 1e-6
NEG_INF = -1e30

LANES = 128
SC_CORES = 2
SC_SUBCORES = 16
SC_WINDOW = 64
PAY_W = D_MODEL + LANES
PAD_W = N_HEADS * LANES
LOG2E = 1.4426950408889634
DECAY_PIECES = 3
CLS_ROWS = 32
VMEM_LIMIT = 48 * 1024 * 1024


def _cparams(sem, flags=None):
    return pltpu.CompilerParams(dimension_semantics=sem, vmem_limit_bytes=VMEM_LIMIT, flags=flags)


def _bdot(a, b):
    return jnp.dot(a.astype(BF16), b.astype(BF16), preferred_element_type=F32)


def _split(a):
    hi = a.astype(BF16)
    lo = (a - hi.astype(F32)).astype(BF16)
    return hi, lo


_NN = (((1,), (0,)), ((), ()))
_NT = (((1,), (1,)), ((), ()))


def _dot3(a, b, dims=_NN):
    ah, al = _split(a)
    bh, bl = _split(b)
    d = lambda x, y: lax.dot_general(x, y, dims, preferred_element_type=F32)
    return d(ah, bh) + (d(ah, bl) + d(al, bh))


def _dot2_exact_rhs(a, b_bf16):
    ah, al = _split(a)
    return (jnp.dot(ah, b_bf16, preferred_element_type=F32)
            + jnp.dot(al, b_bf16, preferred_element_type=F32))


def _rms_mod(x, gain, shift, scale):
    ms = jnp.mean(x * x, axis=-1, keepdims=True)
    y = x * lax.rsqrt(ms + EPS) * gain
    return y * (1.0 + scale) + shift


def _head_rms(z, s_ref, st_ref, gain):
    ss = _dot2_exact_rhs(z * z, s_ref[...])
    inv = lax.rsqrt(ss * (1.0 / HEAD_DIM) + EPS)
    invf = _dot2_exact_rhs(inv, st_ref[...])
    return z * invf * gain


def _head_blocks(x, zero_pad):
    lane = lax.broadcasted_iota(jnp.int32, (1, LANES), 1)
    blocks = []
    for hp in range(N_HEADS // 2):
        pair = x[:, hp * LANES:(hp + 1) * LANES]
        for blk in (pair, pltpu.roll(pair, HEAD_DIM, 1)):
            blocks.append(jnp.where(lane < HEAD_DIM, blk, 0.0) if zero_pad else blk)
    return blocks


def _pad_rows(a, rows):
    if a.shape[0] == rows:
        return a
    return jnp.concatenate([a, jnp.zeros((rows - a.shape[0], a.shape[1]), a.dtype)], axis=0)


def _route(lt, br):
    m = jnp.max(lt, axis=0, keepdims=True)
    p = jnp.exp(lt - m)
    scores = p / jnp.sum(p, axis=0, keepdims=True)
    sel = scores + br
    row = lambda a, e: a[e:e + 1, :]
    gs = []
    for g in range(N_EXPERT_GROUPS):
        v = [row(sel, g * EXPERTS_PER_GROUP + j) for j in range(EXPERTS_PER_GROUP)]
        best = None
        for i in range(EXPERTS_PER_GROUP):
            for j in range(i + 1, EXPERTS_PER_GROUP):
                s = v[i] + v[j]
                best = s if best is None else jnp.maximum(best, s)
        gs.append(best)
    bg = jnp.zeros_like(gs[0])
    bv = gs[0]
    for g in range(1, N_EXPERT_GROUPS):
        better = gs[g] > bv
        bg = jnp.where(better, float(g), bg)
        bv = jnp.where(better, gs[g], bv)

    def in_group(a, j):
        out = row(a, j)
        for g in range(1, N_EXPERT_GROUPS):
            out = jnp.where(bg == float(g), row(a, g * EXPERTS_PER_GROUP + j), out)
        return out

    sg = [in_group(sel, j) for j in range(EXPERTS_PER_GROUP)]
    cg = [in_group(scores, j) for j in range(EXPERTS_PER_GROUP)]

    def first_argmax(vals):
        mx = vals[0]
        for v in vals[1:]:
            mx = jnp.maximum(mx, v)
        idx = jnp.full_like(mx, float(len(vals) - 1))
        for j in range(len(vals) - 2, -1, -1):
            idx = jnp.where(vals[j] == mx, float(j), idx)
        return idx

    i1 = first_argmax(sg)
    i2 = first_argmax([jnp.where(i1 == float(j), -jnp.inf, sg[j]) for j in range(EXPERTS_PER_GROUP)])
    lo = jnp.minimum(i1, i2)
    hi = jnp.maximum(i1, i2)

    def pick(vals, idx):
        out = vals[0]
        for j in range(1, len(vals)):
            out = jnp.where(idx == float(j), vals[j], out)
        return out

    c_lo = pick(cg, lo)
    c_hi = pick(cg, hi)
    tot = c_lo + c_hi
    pair = jnp.where(lo == 0.0, hi - 1.0, jnp.where(lo == 1.0, hi + 1.0, 5.0))
    return bg * float(N_PAIRS) + pair, c_lo / tot, c_hi / tot


def _moe_prep(x1, ng2, sh2, sc2, wrt_ref, br_ref, run_ref, pay_ref, info_ref, cnt_ref):
    tm = x1.shape[0]
    tr = max(tm, LANES)
    h2 = _rms_mod(x1, ng2, sh2, sc2)
    lt = _dot3(wrt_ref[...], _pad_rows(h2, tr), _NT)
    cls, w_lo, w_hi = _route(lt, br_ref[...])

    r = lax.broadcasted_iota(jnp.int32, (LANES, tr), 0)
    wrows = jnp.where(r == 0, w_lo, jnp.where(r == 1, w_hi, 0.0))
    pay_ref[:, :D_MODEL] = h2
    pay_ref[:, D_MODEL:] = wrows.T[:tm, :]

    crow = lax.broadcasted_iota(jnp.int32, (CLS_ROWS, tr), 0).astype(F32)
    lane = lax.broadcasted_iota(jnp.int32, (CLS_ROWS, tr), 1)
    onehot = jnp.where((crow == cls) & (lane < tm), 1.0, 0.0)
    us = lax.broadcasted_iota(jnp.int32, (tr, tr), 0)
    ut = lax.broadcasted_iota(jnp.int32, (tr, tr), 1)
    upper = jnp.where(us < ut, 1.0, 0.0).astype(BF16)
    before = jnp.dot(onehot.astype(BF16), upper, preferred_element_type=F32) + run_ref[:, 0:1]
    rank = jnp.sum(onehot * before, axis=0, keepdims=True)
    run_new = run_ref[...] + jnp.sum(onehot, axis=1, keepdims=True)
    run_ref[...] = run_new
    cnt_ref[...] = run_new
    ir = lax.broadcasted_iota(jnp.int32, (8, tr), 0)
    info = jnp.where(ir == 0, cls, jnp.where(ir == 1, rank, 0.0)).astype(jnp.int32)
    info_ref[...] = info[:, :tm]


def _ada_body(c_ref, w_ref, b_ref, o_ref):
    c = c_ref[...]
    o_ref[...] = _dot3(c * jax.nn.sigmoid(c), w_ref[...]) + b_ref[...]


def _ada_call(c_all, w, b):
    n_l, _, n_out = w.shape
    bc = c_all.shape[0]
    tn = 1536 if n_out % 1536 == 0 else 1024
    return pl.pallas_call(
        _ada_body,
        grid=(n_l, n_out // tn),
        in_specs=[pl.BlockSpec((bc, D_MODEL), lambda l, j: (0, 0)),
                  pl.BlockSpec((None, D_MODEL, tn), lambda l, j: (l, 0, j)),
                  pl.BlockSpec((None, 1, tn), lambda l, j: (l, 0, j))],
        out_specs=pl.BlockSpec((None, bc, tn), lambda l, j: (l, 0, j)),
        out_shape=jax.ShapeDtypeStruct((n_l, bc, n_out), F32),
        compiler_params=_cparams(("arbitrary", "arbitrary")),
        name="adaln",
    )(c_all, w, b.reshape(n_l, 1, n_out))


def _tile_rows(t):
    return 256 if t % 256 == 0 else t


def _tok_spec(tm, width=D_MODEL):
    return pl.BlockSpec((None, tm, width), lambda b, t: (b, t, 0))


def _per_batch_spec(rows, width=D_MODEL):
    return pl.BlockSpec((None, rows, width), lambda b, t: (b, 0, 0))


def _const_spec(shape):
    nd = len(shape)
    return pl.BlockSpec(shape, lambda b, t: (0,) * nd)


def _prep_out(bsz, t, tm):
    shapes = [jax.ShapeDtypeStruct((bsz, t, PAY_W), F32),
              jax.ShapeDtypeStruct((bsz, 8, t), jnp.int32),
              jax.ShapeDtypeStruct((CLS_ROWS, LANES), F32)]
    specs = [_tok_spec(tm, PAY_W),
             pl.BlockSpec((None, 8, tm), lambda b, t: (b, 0, t)),
             _const_spec((CLS_ROWS, LANES))]
    return shapes, specs


def _mixer_body(has_res, tm, start_pos, *refs):
    it = iter(refs)
    x_ref, xp_ref = next(it), next(it)
    if has_res:
        y_ref, yp_ref, g2p_ref = next(it), next(it), next(it)
    hist_ref, ada_ref, ng_ref, wp_ref, ps_ref, wrt_ref, br_ref = (next(it) for _ in range(7))
    x1_ref, pay_ref, info_ref, cnt_ref, hout_ref = (next(it) for _ in range(5))
    run_ref, ue_ref = next(it), next(it)

    b = pl.program_id(0)
    t = pl.program_id(1)

    @pl.when((b == 0) & (t == 0))
    def _():
        run_ref[...] = jnp.zeros_like(run_ref)

    xin = x_ref[...]
    xp = xp_ref[...]
    if has_res:
        g2p = g2p_ref[...]
        xin = xin + g2p * y_ref[...]
        xp = xp + g2p * yp_ref[...]
    ada = ada_ref[...]
    sh1, sc1, g1, sh2, sc2 = (ada[i:i + 1] for i in range(5))
    ng = ng_ref[...]
    u = _rms_mod(xin, ng[0:1], sh1, sc1)
    up = _rms_mod(xp, ng[0:1], sh1, sc1)
    up = jnp.where(t == 0, hist_ref[...], up)
    ue_ref[0:HIST_ROWS, :] = up
    ue_ref[HIST_ROWS:, :] = u

    pos = start_pos + t * tm + lax.broadcasted_iota(jnp.int32, (tm, 1), 0)
    cols = []
    for g, w in enumerate(POOL_WINDOWS):
        sl = slice(g * POOL_GROUP_DIM, (g + 1) * POOL_GROUP_DIM)
        s = u[:, sl]
        for j in range(1, w):
            s = s + ue_ref[HIST_ROWS - j:HIST_ROWS - j + tm, sl]
        cnt = jnp.minimum(pos + 1, w).astype(F32)
        cols.append(_bdot(s / cnt - u[:, sl], wp_ref[g]))
    y = jnp.concatenate(cols, axis=1) * ps_ref[...]
    x1 = xin + g1 * y
    x1_ref[...] = x1
    hout_ref[...] = ue_ref[tm + 1:tm + HIST_ROWS, :]
    _moe_prep(x1, ng[1:2], sh2, sc2, wrt_ref, br_ref, run_ref, pay_ref, info_ref, cnt_ref)


def _mixer_call(x, res, hist16, ada_l, ng_l, wp_l, ps_l, wrt, br, start_pos):
    bsz, t, _ = x.shape
    tm = _tile_rows(t)
    prev_spec = pl.BlockSpec((None, HIST_ROWS, D_MODEL),
                             lambda b, i: (b, jnp.maximum(i * (tm // HIST_ROWS) - 1, 0), 0))
    ins = [x, x]
    specs = [_tok_spec(tm), prev_spec]
    if res is not None:
        y, g2p = res
        ins += [y, y, g2p]
        specs += [_tok_spec(tm), prev_spec, _per_batch_spec(1)]
    ins += [hist16, ada_l, ng_l, wp_l, ps_l, wrt, br]
    specs += [_per_batch_spec(HIST_ROWS), _per_batch_spec(6), _const_spec((2, D_MODEL)),
              _const_spec((N_POOL_GROUPS, POOL_GROUP_DIM, POOL_GROUP_DIM)), _const_spec((1, D_MODEL)),
              _const_spec((N_EXPERTS, D_MODEL)), _const_spec((N_EXPERTS, 1))]
    p_shapes, p_specs = _prep_out(bsz, t, tm)
    return pl.pallas_call(
        functools.partial(_mixer_body, res is not None, tm, start_pos),
        grid=(bsz, t // tm),
        in_specs=specs,
        out_specs=[_tok_spec(tm)] + p_specs + [_per_batch_spec(POOL_BUF)],
        out_shape=[jax.ShapeDtypeStruct((bsz, t, D_MODEL), F32)] + p_shapes
                  + [jax.ShapeDtypeStruct((bsz, POOL_BUF, D_MODEL), F32)],
        scratch_shapes=[pltpu.VMEM((CLS_ROWS, LANES), F32), pltpu.VMEM((tm + HIST_ROWS, D_MODEL), F32)],
        compiler_params=_cparams(("arbitrary", "arbitrary")),
        name="pool_mixer",
    )(*ins)


def _proj_body(with_kv, tm, *refs):
    it = iter(refs)
    x1p_ref, y_ref, g2p_ref, ada_ref, ng_ref, wq_ref, qn_ref, s_ref, st_ref = (next(it) for _ in range(9))
    if with_kv:
        kva_ref, kvn_ref, wkv_ref, wf_ref, bf_ref, kn_ref = (next(it) for _ in range(6))
    x_ref, q_ref = next(it), next(it)
    if with_kv:
        k_ref, v_ref, kp_ref, vt_ref, lf_ref, lfw_ref = (next(it) for _ in range(6))

    x = x1p_ref[...] + g2p_ref[...] * y_ref[...]
    x_ref[...] = x
    ada = ada_ref[...]
    ng = ng_ref[...]
    h = _rms_mod(x, ng[0:1], ada[0:1], ada[1:2])
    q = _head_rms(_bdot(h, wq_ref[...]), s_ref, st_ref, qn_ref[...])
    for i, blk in enumerate(_head_blocks(q * (ATTN_SCALE * LOG2E), True)):
        q_ref[:, i * LANES:(i + 1) * LANES] = blk.astype(BF16)
    if with_kv:
        kva = kva_ref[...]
        hk = _rms_mod(x, kvn_ref[...], kva[0:1], kva[1:2])
        proj = _bdot(hk, wkv_ref[...])
        k = _head_rms(proj[:, :KV_WIDTH], s_ref, st_ref, kn_ref[...])
        v = proj[:, KV_WIDTH:]
        k_ref[...] = k
        v_ref[...] = v
        for i, blk in enumerate(_head_blocks(k, False)):
            kp_ref[:, i * LANES:(i + 1) * LANES] = blk.astype(BF16)
        vt_ref[...] = _pad_rows(v, max(tm, LANES)).T[:, :tm].astype(BF16)
        z = _dot3(hk, wf_ref[...]) + bf_ref[...]
        lf = jnp.minimum(z, 0.0) - jnp.log(1.0 + jnp.exp(-jnp.abs(z)))
        lf_ref[...] = lf[:, :N_HEADS]
        lane = lax.broadcasted_iota(jnp.int32, (1, LANES), 1)
        lfw_ref[...] = jnp.where(lane < N_HEADS, lf, 0.0)


def _proj_call(x1p, y, g2p, ada_l, ng_l, wq, qn, s_mat, st_mat, kv=None):
    bsz, t, _ = x1p.shape
    tm = _tile_rows(t)
    ins = [x1p, y, g2p, ada_l, ng_l, wq, qn, s_mat, st_mat]
    specs = [_tok_spec(tm), _tok_spec(tm), _per_batch_spec(1), _per_batch_spec(6), _const_spec((2, D_MODEL)),
             _const_spec((D_MODEL, KV_WIDTH)), _const_spec((1, D_MODEL)),
             _const_spec((D_MODEL, LANES)), _const_spec((LANES, D_MODEL))]
    out_shapes = [jax.ShapeDtypeStruct((bsz, t, D_MODEL), F32), jax.ShapeDtypeStruct((bsz, t, PAD_W), BF16)]
    out_specs = [_tok_spec(tm), _tok_spec(tm, PAD_W)]
    if kv is not None:
        kva, kvn, wkv, wf, bf, kn = kv
        ins += [kva, kvn, wkv, wf, bf, kn]
        specs += [_per_batch_spec(2), _const_spec((1, D_MODEL)), _const_spec((D_MODEL, 2 * KV_WIDTH)),
                  _const_spec((D_MODEL, LANES)), _const_spec((1, LANES)), _const_spec((1, D_MODEL))]
        out_shapes += [jax.ShapeDtypeStruct((bsz, t, D_MODEL), F32)] * 2
        out_shapes += [jax.ShapeDtypeStruct((bsz, t, PAD_W), BF16), jax.ShapeDtypeStruct((bsz, D_MODEL, t), BF16)]
        out_shapes += [jax.ShapeDtypeStruct((bsz, t, N_HEADS), F32), jax.ShapeDtypeStruct((bsz, t, LANES), F32)]
        out_specs += [_tok_spec(tm)] * 2
        out_specs += [_tok_spec(tm, PAD_W), pl.BlockSpec((None, D_MODEL, tm), lambda b, i: (b, 0, i))]
        out_specs += [_tok_spec(tm, N_HEADS), _tok_spec(tm, LANES)]
    return pl.pallas_call(
        functools.partial(_proj_body, kv is not None, tm),
        grid=(bsz, t // tm),
        in_specs=specs, out_specs=out_specs, out_shape=out_shapes,
        compiler_params=_cparams(("arbitrary", "arbitrary")),
        name="qkv_proj" if kv is not None else "q_proj",
    )(*ins)


def _decay_body(tc, lf_ref, o_ref, carry_ref):
    @pl.when(pl.program_id(1) == 0)
    def _():
        carry_ref[...] = jnp.zeros_like(carry_ref)

    lf = lf_ref[...]
    r = lax.broadcasted_iota(jnp.int32, (tc, tc), 0)
    c = lax.broadcasted_iota(jnp.int32, (tc, tc), 1)
    lower = jnp.where(r >= c, 1.0, 0.0).astype(BF16)
    hi, lo = _split(lf)
    f = (jnp.dot(lower, hi, preferred_element_type=F32) + jnp.dot(lower, lo, preferred_element_type=F32)
         + carry_ref[0:1, :])
    carry_ref[...] = jnp.broadcast_to(f[tc - 1:tc, :], carry_ref.shape)
    nb = f * (-LOG2E)
    p1 = nb.astype(BF16)
    r1 = nb - p1.astype(F32)
    p2 = r1.astype(BF16)
    p3 = (r1 - p2.astype(F32)).astype(BF16)
    hr = lax.broadcasted_iota(jnp.int32, (LANES, LANES), 0)
    lc = lax.broadcasted_iota(jnp.int32, (LANES, LANES), 1)
    out = None
    for i, p in enumerate((p1, p2, p3)):
        place = jnp.where((lc == DECAY_PIECES * hr + i) & (hr < N_HEADS), 1.0, 0.0).astype(BF16)
        term = jnp.dot(p, place, preferred_element_type=F32)
        out = term if out is None else out + term
    o_ref[...] = out.astype(BF16)


def _decay_call(lfw):
    bsz, tk, _ = lfw.shape
    tc = 512
    spec = pl.BlockSpec((None, tc, LANES), lambda b, t: (b, t, 0))
    return pl.pallas_call(
        functools.partial(_decay_body, tc),
        grid=(bsz, tk // tc), in_specs=[spec], out_specs=spec,
        out_shape=jax.ShapeDtypeStruct(lfw.shape, BF16),
        scratch_shapes=[pltpu.VMEM((8, LANES), F32)],
        compiler_params=_cparams(("arbitrary", "arbitrary")),
        name="decay_bias",
    )(lfw)


def _flash_body(tq, tkp, tks, n_past, n_self, *refs):
    it = iter(refs)
    q_ref = next(it)
    if n_past:
        kp_ref, vtp_ref, fpp_ref = next(it), next(it), next(it)
    ks_ref, vts_ref, fps_ref = next(it), next(it), next(it)
    o_ref, m_ref, l_ref, acc_ref = next(it), next(it), next(it), next(it)
    qi = pl.program_id(1)
    step = pl.program_id(2)

    @pl.when(step == 0)
    def _():
        m_ref[...] = jnp.full_like(m_ref, NEG_INF)
        l_ref[...] = jnp.zeros_like(l_ref)
        acc_ref[...] = jnp.zeros_like(acc_ref)

    def process(k_ref, vt_ref, fp_ref, tk, key_base):
        fp = fp_ref[...]
        lane = lax.broadcasted_iota(jnp.int32, (1, LANES), 1)
        if key_base is not None:
            kpos = key_base + lax.broadcasted_iota(jnp.int32, (tk, 1), 0)
            qpos = qi * tq + lax.broadcasted_iota(jnp.int32, (1, tq), 1)
            visible = kpos <= qpos

        def logits(h):
            d = lane - DECAY_PIECES * h
            ones = jnp.where((d >= 0) & (d < DECAY_PIECES), 1.0, 0.0).astype(BF16)
            lhs = jnp.concatenate([k_ref[:, h * LANES:(h + 1) * LANES], fp], axis=1)
            rhs = jnp.concatenate([q_ref[:, h * LANES:(h + 1) * LANES],
                                   jnp.broadcast_to(ones, (tq, LANES))], axis=1)
            s = lax.dot_general(lhs, rhs, _NT, preferred_element_type=F32)
            return s if key_base is None else jnp.where(visible, s, NEG_INF)

        s_next = logits(0)
        for h in range(N_HEADS):
            s = s_next
            if h + 1 < N_HEADS:
                s_next = logits(h + 1)
            rows = slice(h * HEAD_DIM, (h + 1) * HEAD_DIM)
            m_old = m_ref[h]
            m_new = jnp.maximum(m_old, jnp.max(s, axis=0, keepdims=True))
            alpha = jnp.exp2(m_old - m_new)
            p = jnp.exp2(s - m_new)
            l_ref[h] = alpha * l_ref[h] + jnp.sum(p, axis=0, keepdims=True)
            m_ref[h] = m_new
            pv = jnp.dot(vt_ref[rows, :], p.astype(BF16), preferred_element_type=F32)
            acc_ref[rows, :] = alpha * acc_ref[rows, :] + pv

    if n_past:
        @pl.when(step < n_past)
        def _():
            process(kp_ref, vtp_ref, fpp_ref, tkp, None)

    j = step - n_past
    last = (qi * tq + tq - 1) // tks
    has_hidden = (j + 1) * tks - 1 > qi * tq

    @pl.when((j >= 0) & (j <= last) & has_hidden)
    def _():
        process(ks_ref, vts_ref, fps_ref, tks, j * tks)

    @pl.when((j >= 0) & (j <= last) & jnp.logical_not(has_hidden))
    def _():
        process(ks_ref, vts_ref, fps_ref, tks, None)

    @pl.when(step == n_past + n_self - 1)
    def _():
        r = lax.broadcasted_iota(jnp.int32, (tq, tq), 0)
        c = lax.broadcasted_iota(jnp.int32, (tq, tq), 1)
        eye = jnp.where(r == c, 1.0, 0.0).astype(BF16)
        row = lax.broadcasted_iota(jnp.int32, (LANES, 1), 0)
        for hp in range(N_HEADS // 2):
            denom = jnp.where(row < HEAD_DIM, l_ref[2 * hp], l_ref[2 * hp + 1])
            o_t = (acc_ref[hp * LANES:(hp + 1) * LANES, :] / denom).astype(BF16)
            o_ref[:, hp * LANES:(hp + 1) * LANES] = lax.dot_general(
                eye, o_t, _NT, preferred_element_type=F32).astype(BF16)


def _flash_call(q, k_self, vt_self, past, fp, n_past_keys):
    bsz, t_q, _ = q.shape
    tq = 512 if t_q % 512 == 0 else t_q
    tks = 256 if t_q % 256 == 0 else t_q
    n_self = t_q // tks
    tkp = 512
    n_past = n_past_keys // tkp
    fp_self_base = n_past_keys // tks

    def self_idx(i, s):
        return jnp.clip(s - n_past, 0, (i * tq + tq - 1) // tks)

    ins = [q]
    specs = [pl.BlockSpec((None, tq, PAD_W), lambda b, i, s: (b, i, 0))]
    if n_past:
        past_idx = lambda s: jnp.minimum(s, n_past - 1)
        ins += [past[0], past[1], fp]
        specs += [pl.BlockSpec((None, tkp, PAD_W), lambda b, i, s: (b, past_idx(s), 0)),
                  pl.BlockSpec((None, D_MODEL, tkp), lambda b, i, s: (b, 0, past_idx(s))),
                  pl.BlockSpec((None, tkp, LANES), lambda b, i, s: (b, past_idx(s), 0))]
    ins += [k_self, vt_self, fp]
    specs += [pl.BlockSpec((None, tks, PAD_W), lambda b, i, s: (b, self_idx(i, s), 0)),
              pl.BlockSpec((None, D_MODEL, tks), lambda b, i, s: (b, 0, self_idx(i, s))),
              pl.BlockSpec((None, tks, LANES), lambda b, i, s: (b, fp_self_base + self_idx(i, s), 0))]
    return pl.pallas_call(
        functools.partial(_flash_body, tq, tkp, tks, n_past, n_self),
        grid=(bsz, t_q // tq, n_past + n_self),
        in_specs=specs,
        out_specs=pl.BlockSpec((None, tq, D_MODEL), lambda b, i, s: (b, i, 0)),
        out_shape=jax.ShapeDtypeStruct((bsz, t_q, D_MODEL), BF16),
        scratch_shapes=[pltpu.VMEM((N_HEADS, 1, tq), F32), pltpu.VMEM((N_HEADS, 1, tq), F32),
                        pltpu.VMEM((D_MODEL, tq), F32)],
        compiler_params=_cparams(("arbitrary", "arbitrary", "arbitrary")),
        name="fox_attention",
    )(*ins)


def _oproj_body(x_ref, o_ref, wo_ref, ada_ref, ng_ref, wrt_ref, br_ref,
                x1_ref, pay_ref, info_ref, cnt_ref, run_ref):
    @pl.when((pl.program_id(0) == 0) & (pl.program_id(1) == 0))
    def _():
        run_ref[...] = jnp.zeros_like(run_ref)

    ada = ada_ref[...]
    x1 = x_ref[...] + ada[2:3] * jnp.dot(o_ref[...], wo_ref[...], preferred_element_type=F32)
    x1_ref[...] = x1
    _moe_prep(x1, ng_ref[...][1:2], ada[3:4], ada[4:5], wrt_ref, br_ref, run_ref, pay_ref, info_ref, cnt_ref)


def _oproj_call(x, o, wo, ada_l, ng_l, wrt, br):
    bsz, t, _ = x.shape
    tm = _tile_rows(t)
    p_shapes, p_specs = _prep_out(bsz, t, tm)
    return pl.pallas_call(
        _oproj_body,
        grid=(bsz, t // tm),
        in_specs=[_tok_spec(tm), _tok_spec(tm), _const_spec((KV_WIDTH, D_MODEL)), _per_batch_spec(6),
                  _const_spec((2, D_MODEL)), _const_spec((N_EXPERTS, D_MODEL)), _const_spec((N_EXPERTS, 1))],
        out_specs=[_tok_spec(tm)] + p_specs,
        out_shape=[jax.ShapeDtypeStruct((bsz, t, D_MODEL), F32)] + p_shapes,
        scratch_shapes=[pltpu.VMEM((CLS_ROWS, LANES), F32)],
        compiler_params=_cparams(("arbitrary", "arbitrary")),
        name="attn_out_proj",
    )(x, o, wo, ada_l, ng_l, wrt, br)


def _final_body(x_ref, y_ref, g_ref, o_ref):
    o_ref[...] = x_ref[...] + g_ref[...] * y_ref[...]


def _final_call(x1, y, g2):
    bsz, t, _ = x1.shape
    tm = _tile_rows(t)
    return pl.pallas_call(
        _final_body, grid=(bsz, t // tm),
        in_specs=[_tok_spec(tm), _tok_spec(tm), _per_batch_spec(1)],
        out_specs=_tok_spec(tm),
        out_shape=jax.ShapeDtypeStruct(x1.shape, F32),
        compiler_params=_cparams(("arbitrary", "arbitrary")),
        name="final_residual",
    )(x1, y, g2)


def _sc_worker_loop(n_win, fn):
    wid = lax.axis_index("s") * SC_CORES + lax.axis_index("c")
    n_workers = SC_CORES * SC_SUBCORES

    @pl.loop(0, pl.cdiv(n_win, n_workers))
    def _(j):
        win = j * n_workers + wid

        @pl.when(win < n_win)
        def _():
            fn(pl.multiple_of(win * SC_WINDOW, SC_WINDOW))


def _sc_scatter_rows(rows, idx, n_out):
    n, width = rows.shape
    mesh = plsc.VectorSubcoreMesh(core_axis_name="c", subcore_axis_name="s")

    @functools.partial(
        pl.kernel, mesh=mesh, out_type=jax.ShapeDtypeStruct((n_out, width), rows.dtype),
        scratch_types=[pltpu.VMEM((SC_WINDOW,), jnp.int32), pltpu.VMEM((SC_WINDOW, width), rows.dtype)])
    def k(rows_hbm, idx_hbm, out_hbm, idx_v, rows_v):
        def one(base):
            pltpu.sync_copy(idx_hbm.at[pl.ds(base, SC_WINDOW)], idx_v)
            pltpu.sync_copy(rows_hbm.at[pl.ds(base, SC_WINDOW)], rows_v)
            pltpu.sync_copy(rows_v, out_hbm.at[idx_v])
        _sc_worker_loop(n // SC_WINDOW, one)

    return k(rows, idx)


def _sc_gather_rows(table, idx):
    n = idx.shape[0]
    width = table.shape[1]
    mesh = plsc.VectorSubcoreMesh(core_axis_name="c", subcore_axis_name="s")

    @functools.partial(
        pl.kernel, mesh=mesh, out_type=jax.ShapeDtypeStruct((n, width), table.dtype),
        scratch_types=[pltpu.VMEM((SC_WINDOW,), jnp.int32), pltpu.VMEM((SC_WINDOW, width), table.dtype)])
    def k(table_hbm, idx_hbm, out_hbm, idx_v, rows_v):
        def one(base):
            pltpu.sync_copy(idx_hbm.at[pl.ds(base, SC_WINDOW)], idx_v)
            pltpu.sync_copy(table_hbm.at[idx_v], rows_v)
            pltpu.sync_copy(rows_v, out_hbm.at[pl.ds(base, SC_WINDOW)])
        _sc_worker_loop(n // SC_WINDOW, one)

    return k(table, idx)


def _moe_body(e1_ref, e2_ref, na_ref, hs_ref, g1_ref, g2_ref, u1_ref, u2_ref, d1_ref, d2_ref, o_ref):
    @pl.when(pl.program_id(0) < na_ref[0])
    def _():
        blk = hs_ref[...]
        h = blk[:, :D_MODEL].astype(BF16)
        y = None
        for lane, (g_ref, u_ref, d_ref) in enumerate(((g1_ref, u1_ref, d1_ref), (g2_ref, u2_ref, d2_ref))):
            gate = jnp.dot(h, g_ref[...], preferred_element_type=F32)
            up = jnp.dot(h, u_ref[...], preferred_element_type=F32)
            w = blk[:, D_MODEL + lane:D_MODEL + lane + 1]
            act = (gate * jax.nn.sigmoid(gate) * up * w).astype(BF16)
            term = jnp.dot(act, d_ref[...], preferred_element_type=F32)
            y = term if y is None else y + term
        o_ref[...] = y


def _moe_call(hs, tile_e1, tile_e2, n_active, wg, wu, wd, layer, tm):
    n_s = hs.shape[0]
    n_tiles = n_s // tm
    base = layer * N_EXPERTS

    def row_map(i, e1, e2, na):
        return (jnp.minimum(i, na[0] - 1), 0)

    def w_map(which):
        def m(i, e1, e2, na):
            e = (e1, e2)[which]
            return (base + e[jnp.minimum(i, na[0] - 1)], 0, 0)
        return m

    gu = lambda which: pl.BlockSpec((None, D_MODEL, D_EXPERT), w_map(which))
    dn = lambda which: pl.BlockSpec((None, D_EXPERT, D_MODEL), w_map(which))
    return pl.pallas_call(
        _moe_body,
        grid_spec=pltpu.PrefetchScalarGridSpec(
            num_scalar_prefetch=3, grid=(n_tiles,),
            in_specs=[pl.BlockSpec((tm, PAY_W), row_map), gu(0), gu(1), gu(0), gu(1), dn(0), dn(1)],
            out_specs=pl.BlockSpec((tm, D_MODEL), row_map)),
        out_shape=jax.ShapeDtypeStruct((n_s, D_MODEL), F32),
        compiler_params=_cparams(("arbitrary",)),
        name="grouped_experts",
    )(tile_e1, tile_e2, n_active, hs, wg, wg, wu, wu, wd, wd)


_PAIR_LO = (0, 0, 0, 1, 1, 2)
_PAIR_HI = (1, 2, 3, 2, 3, 3)


def _moe_layer(pay, info, counts, wg, wu, wd, layer):
    bsz, t, _ = pay.shape
    n = bsz * t
    tm = 256
    n_s = ((n + N_CLASSES * (tm - 1)) // tm + 1) * tm
    n_tiles = n_s // tm
    cls = info[:, 0, :].reshape(n)
    rank = info[:, 1, :].reshape(n)
    cnt = counts[:N_CLASSES, 0].astype(jnp.int32)
    padded = ((cnt + tm - 1) // tm) * tm
    ends = jnp.cumsum(padded)
    starts = ends - padded
    dest = starts[cls] + rank
    tile_start = jnp.arange(n_tiles, dtype=jnp.int32) * tm
    tile_cls = jnp.minimum(jnp.sum((tile_start[:, None] >= ends[None, :]).astype(jnp.int32), axis=1),
                           N_CLASSES - 1)
    grp = tile_cls // N_PAIRS
    pr = tile_cls % N_PAIRS
    tile_e1 = grp * EXPERTS_PER_GROUP + jnp.asarray(_PAIR_LO, jnp.int32)[pr]
    tile_e2 = grp * EXPERTS_PER_GROUP + jnp.asarray(_PAIR_HI, jnp.int32)[pr]
    n_active = (ends[-1:] // tm).astype(jnp.int32)
    hs = _sc_scatter_rows(pay.reshape(n, PAY_W), dest, n_s)
    ys = _moe_call(hs, tile_e1, tile_e2, n_active, wg, wu, wd, layer, tm)
    return _sc_gather_rows(ys, dest).reshape(bsz, t, D_MODEL)


def _trunk(x, ada, kva, hist, past, prm):
    bsz, t, _ = x.shape
    start_pos = 0 if past is None else past[0].shape[1]
    wrt, br = prm["wrt"], prm["br"]
    res = None
    new_hist = []
    for layer in range(N_A_LAYERS):
        if hist is None:
            h16 = jnp.zeros((bsz, HIST_ROWS, D_MODEL), F32)
        else:
            h16 = jnp.pad(hist[layer], ((0, 0), (1, 0), (0, 0)))
        x, pay, info, counts, hout = _mixer_call(
            x, res, h16, ada[layer], prm["norm_g"][layer], prm["w_pool"][layer], prm["pool_scale"][layer],
            wrt, br, start_pos)
        new_hist.append(hout)
        y = _moe_layer(pay, info, counts, prm["w_gate"], prm["w_up"], prm["w_down"], layer)
        res = (y, ada[layer][:, 5:6, :])

    shared = None
    for j in range(N_B_LAYERS):
        layer = N_A_LAYERS + j
        y, g2p = res
        if j == 0:
            kv = (kva, prm["kv_norm"], prm["w_kv"], prm["w_f"], prm["b_f"], prm["k_norm"])
            xr, q, k, v, k_pad, vt, logf, lfw = _proj_call(
                x, y, g2p, ada[layer], prm["norm_g"][layer], prm["w_q"][j], prm["q_norm"][j],
                prm["s_mat"], prm["st_mat"], kv)
            if past is None:
                past_kv = None
                fp = _decay_call(lfw)
            else:
                ck, cv, clogf = past
                n_past = ck.shape[1]
                pad = -(n_past + t) % 512
                lfw = jnp.concatenate([jnp.pad(clogf, ((0, 0), (0, 0), (0, LANES - N_HEADS))), lfw,
                                       jnp.zeros((bsz, pad, LANES), F32)], axis=1)
                fp = _decay_call(lfw)
                ck_pad = jnp.pad(ck, ((0, 0), (0, 0), (0, 0), (0, LANES - HEAD_DIM)))
                past_kv = (ck_pad.reshape(bsz, n_past, PAD_W).astype(BF16),
                           jnp.swapaxes(cv.reshape(bsz, n_past, D_MODEL), 1, 2).astype(BF16))
            shared = (k, v, logf, k_pad, vt, past_kv, fp)
        else:
            xr, q = _proj_call(x, y, g2p, ada[layer], prm["norm_g"][layer], prm["w_q"][j], prm["q_norm"][j],
                               prm["s_mat"], prm["st_mat"])
        o = _flash_call(q, shared[3], shared[4], shared[5], shared[6], start_pos)
        x, pay, info, counts = _oproj_call(xr, o, prm["w_o"][j], ada[layer], prm["norm_g"][layer], wrt, br)
        y = _moe_layer(pay, info, counts, prm["w_gate"], prm["w_up"], prm["w_down"], layer)
        res = (y, ada[layer][:, 5:6, :])

    out = _final_call(x, res[0], res[1])
    k, v, logf = shared[:3]
    return (out, jnp.stack(new_hist), k.reshape(bsz, t, N_HEADS, HEAD_DIM),
            v.reshape(bsz, t, N_HEADS, HEAD_DIM), logf)


def kernel(x_prompt, x_sample, cache_pool, cache_k, cache_v, cache_logf, c_prompt, c_sample, ada_w, ada_b, norm_g, w_pool, pool_scale, kv_ada_w, kv_ada_b, kv_norm, w_kvf, b_f, k_norm, w_q, q_norm, w_o, w_router, b_router, w_gate, w_up, w_down):
    bp = x_prompt.shape[0]
    c_all = jnp.concatenate([c_prompt, c_sample], axis=0)
    bc = c_all.shape[0]
    ada = _ada_call(c_all, ada_w, ada_b).reshape(DEPTH, bc, 6, D_MODEL)
    kva = _ada_call(c_all, kv_ada_w[None], kv_ada_b[None]).reshape(bc, 2, D_MODEL)

    head_of_lane = jnp.arange(D_MODEL, dtype=jnp.int32) // HEAD_DIM
    s_mat = (head_of_lane[:, None] == jnp.arange(LANES, dtype=jnp.int32)[None, :]).astype(BF16)
    prm = {
        "norm_g": norm_g,
        "w_pool": w_pool.astype(BF16),
        "pool_scale": pool_scale.reshape(N_A_LAYERS, 1, D_MODEL),
        "kv_norm": kv_norm.reshape(1, D_MODEL),
        "w_kv": w_kvf[:, :2 * KV_WIDTH].astype(BF16),
        "w_f": jnp.pad(w_kvf[:, 2 * KV_WIDTH:], ((0, 0), (0, LANES - N_HEADS))),
        "b_f": jnp.pad(b_f, (0, LANES - N_HEADS)).reshape(1, LANES),
        "k_norm": jnp.tile(k_norm, N_HEADS).reshape(1, D_MODEL),
        "w_q": w_q.astype(BF16),
        "q_norm": jnp.tile(q_norm, (1, N_HEADS)).reshape(N_B_LAYERS, 1, D_MODEL),
        "w_o": w_o.astype(BF16),
        "wrt": w_router.T,
        "br": b_router.reshape(N_EXPERTS, 1),
        "w_gate": w_gate.astype(BF16).reshape(DEPTH * N_EXPERTS, D_MODEL, D_EXPERT),
        "w_up": w_up.astype(BF16).reshape(DEPTH * N_EXPERTS, D_MODEL, D_EXPERT),
        "w_down": w_down.astype(BF16).reshape(DEPTH * N_EXPERTS, D_EXPERT, D_MODEL),
        "s_mat": s_mat,
        "st_mat": s_mat.T,
    }
    outs_p = _trunk(x_prompt, ada[:, :bp], kva[:bp], None, None, prm)
    outs_s = _trunk(x_sample, ada[:, bp:], kva[bp:], cache_pool, (cache_k, cache_v, cache_logf), prm)
    return (outs_p[0], outs_s[0]) + outs_p[1:] + outs_s[1:]
```

```python
import functools

import jax
import jax.numpy as jnp
from jax import lax
from jax.experimental import pallas as pl
from jax.experimental.pallas import tpu as pltpu
from jax.experimental.pallas import tpu_sc as plsc

F32 = jnp.float32
BF16 = jnp.bfloat16

D_MODEL = 1024
DEPTH = 4
N_A_LAYERS = DEPTH // 2
N_B_LAYERS = DEPTH - N_A_LAYERS
POOL_WINDOWS = (2, 4, 8, 16)
N_POOL_GROUPS = len(POOL_WINDOWS)
POOL_GROUP_DIM = D_MODEL // N_POOL_GROUPS
POOL_BUF = max(POOL_WINDOWS) - 1
HIST_ROWS = POOL_BUF + 1
N_HEADS = 16
HEAD_DIM = D_MODEL // N_HEADS
KV_WIDTH = N_HEADS * HEAD_DIM
ATTN_SCALE = HEAD_DIM ** -0.5
N_EXPERTS = 16
N_EXPERT_GROUPS = 4
EXPERTS_PER_GROUP = N_EXPERTS // N_EXPERT_GROUPS
N_PAIRS = 6
N_CLASSES = N_EXPERT_GROUPS * N_PAIRS
D_EXPERT = D_MODEL // 2
EPS = 1e-6
NEG_INF = -1e30

LANES = 128
SC_CORES = 2
SC_SUBCORES = 16
SC_WINDOW = 64
PACK_W = D_MODEL // 2
PAY_W = PACK_W + LANES
PAD_W = N_HEADS * LANES
LOG2E = 1.4426950408889634
DECAY_PIECES = 3
SUM_ROWS = 16
CLS_ROWS = 32
VMEM_LIMIT = 48 * 1024 * 1024


def _cparams(sem, flags=None):
    return pltpu.CompilerParams(dimension_semantics=sem, vmem_limit_bytes=VMEM_LIMIT, flags=flags)


def _bdot(a, b):
    return jnp.dot(a.astype(BF16), b.astype(BF16), preferred_element_type=F32)


def _split(a):
    hi = a.astype(BF16)
    lo = (a - hi.astype(F32)).astype(BF16)
    return hi, lo


_NN = (((1,), (0,)), ((), ()))
_NT = (((1,), (1,)), ((), ()))


def _dot3(a, b, dims=_NN):
    ah, al = _split(a)
    bh, bl = _split(b)
    d = lambda x, y: lax.dot_general(x, y, dims, preferred_element_type=F32)
    return d(ah, bh) + (d(ah, bl) + d(al, bh))


def _dot2_exact_rhs(a, b_bf16):
    ah, al = _split(a)
    return (jnp.dot(ah, b_bf16, preferred_element_type=F32)
            + jnp.dot(al, b_bf16, preferred_element_type=F32))


def _rms_mod(x, gain, shift, scale):
    ms = jnp.mean(x * x, axis=-1, keepdims=True)
    y = x * lax.rsqrt(ms + EPS) * gain
    return y * (1.0 + scale) + shift


def _head_rms(z, s_ref, st_ref, gain):
    ss = _dot2_exact_rhs(z * z, s_ref[...])
    inv = lax.rsqrt(ss * (1.0 / HEAD_DIM) + EPS)
    invf = _dot2_exact_rhs(inv, st_ref[...])
    return z * invf * gain


def _query_blocks(q):
    lane = lax.broadcasted_iota(jnp.int32, (1, LANES), 1)
    blocks = []
    for h in range(N_HEADS):
        pair = q[:, (h // 2) * LANES:(h // 2 + 1) * LANES]
        in_head = (lane >= HEAD_DIM) if h % 2 else (lane < HEAD_DIM)
        blocks.append(jnp.where(in_head, pair, 0.0))
    return blocks


_HI_MASK = 0xFFFF0000


def _pack_bf16_pairs(x):
    bits = lambda a: lax.bitcast_convert_type(a.astype(BF16).astype(F32), jnp.uint32)
    half = x.shape[1] // 2
    word = (bits(x[:, :half]) >> 16) | (bits(x[:, half:]) & jnp.uint32(_HI_MASK))
    return lax.bitcast_convert_type(word, F32)


def _unpack_bf16_pairs(w):
    word = lax.bitcast_convert_type(w, jnp.uint32)
    lo = lax.bitcast_convert_type(word << 16, F32).astype(BF16)
    hi = lax.bitcast_convert_type(word & jnp.uint32(_HI_MASK), F32).astype(BF16)
    return jnp.concatenate([lo, hi], axis=1)


def _pad_rows(a, rows):
    if a.shape[0] == rows:
        return a
    return jnp.concatenate([a, jnp.zeros((rows - a.shape[0], a.shape[1]), a.dtype)], axis=0)


def _route(lt, br):
    m = jnp.max(lt, axis=0, keepdims=True)
    p = jnp.exp(lt - m)
    scores = p / jnp.sum(p, axis=0, keepdims=True)
    sel = scores + br
    row = lambda a, e: a[e:e + 1, :]
    gs = []
    for g in range(N_EXPERT_GROUPS):
        v = [row(sel, g * EXPERTS_PER_GROUP + j) for j in range(EXPERTS_PER_GROUP)]
        best = None
        for i in range(EXPERTS_PER_GROUP):
            for j in range(i + 1, EXPERTS_PER_GROUP):
                s = v[i] + v[j]
                best = s if best is None else jnp.maximum(best, s)
        gs.append(best)
    bg = jnp.zeros_like(gs[0])
    bv = gs[0]
    for g in range(1, N_EXPERT_GROUPS):
        better = gs[g] > bv
        bg = jnp.where(better, float(g), bg)
        bv = jnp.where(better, gs[g], bv)

    def in_group(a, j):
        out = row(a, j)
        for g in range(1, N_EXPERT_GROUPS):
            out = jnp.where(bg == float(g), row(a, g * EXPERTS_PER_GROUP + j), out)
        return out

    sg = [in_group(sel, j) for j in range(EXPERTS_PER_GROUP)]
    cg = [in_group(scores, j) for j in range(EXPERTS_PER_GROUP)]

    def first_argmax(vals):
        mx = vals[0]
        for v in vals[1:]:
            mx = jnp.maximum(mx, v)
        idx = jnp.full_like(mx, float(len(vals) - 1))
        for j in range(len(vals) - 2, -1, -1):
            idx = jnp.where(vals[j] == mx, float(j), idx)
        return idx

    i1 = first_argmax(sg)
    i2 = first_argmax([jnp.where(i1 == float(j), -jnp.inf, sg[j]) for j in range(EXPERTS_PER_GROUP)])
    lo = jnp.minimum(i1, i2)
    hi = jnp.maximum(i1, i2)

    def pick(vals, idx):
        out = vals[0]
        for j in range(1, len(vals)):
            out = jnp.where(idx == float(j), vals[j], out)
        return out

    c_lo = pick(cg, lo)
    c_hi = pick(cg, hi)
    tot = c_lo + c_hi
    pair = jnp.where(lo == 0.0, hi - 1.0, jnp.where(lo == 1.0, hi + 1.0, 5.0))
    return bg * float(N_PAIRS) + pair, c_lo / tot, c_hi / tot


def _moe_prep(x1, ng2, sh2, sc2, wrt_ref, br_ref, run_ref, pay_ref, info_ref, cnt_ref):
    tm = x1.shape[0]
    tr = max(tm, LANES)
    h2 = _rms_mod(x1, ng2, sh2, sc2)
    lt = _dot3(wrt_ref[...], _pad_rows(h2, tr), _NT)
    cls, w_lo, w_hi = _route(lt, br_ref[...])

    r = lax.broadcasted_iota(jnp.int32, (LANES, tr), 0)
    wrows = jnp.where(r == 0, w_lo, jnp.where(r == 1, w_hi, 0.0))
    pay_ref[:, :PACK_W] = _pack_bf16_pairs(h2)
    pay_ref[:, PACK_W:] = wrows.T[:tm, :]

    crow = lax.broadcasted_iota(jnp.int32, (CLS_ROWS, tr), 0).astype(F32)
    lane = lax.broadcasted_iota(jnp.int32, (CLS_ROWS, tr), 1)
    onehot = jnp.where((crow == cls) & (lane < tm), 1.0, 0.0)
    us = lax.broadcasted_iota(jnp.int32, (tr, tr), 0)
    ut = lax.broadcasted_iota(jnp.int32, (tr, tr), 1)
    upper = jnp.where(us < ut, 1.0, 0.0).astype(BF16)
    before = jnp.dot(onehot.astype(BF16), upper, preferred_element_type=F32) + run_ref[:, 0:1]
    rank = jnp.sum(onehot * before, axis=0, keepdims=True)
    run_new = run_ref[...] + jnp.sum(onehot, axis=1, keepdims=True)
    run_ref[...] = run_new
    cnt_ref[...] = run_new
    ir = lax.broadcasted_iota(jnp.int32, (8, tr), 0)
    info = jnp.where(ir == 0, cls, jnp.where(ir == 1, rank, 0.0)).astype(jnp.int32)
    info_ref[...] = info[:, :tm]


def _ada_body(c_ref, w_ref, b_ref, o_ref):
    c = c_ref[...]
    o_ref[...] = _dot3(c * jax.nn.sigmoid(c), w_ref[...]) + b_ref[...]


def _ada_call(c_all, w, b):
    n_l, _, n_out = w.shape
    bc = c_all.shape[0]
    tn = 1536 if n_out % 1536 == 0 else 1024
    return pl.pallas_call(
        _ada_body,
        grid=(n_l, n_out // tn),
        in_specs=[pl.BlockSpec((bc, D_MODEL), lambda l, j: (0, 0)),
                  pl.BlockSpec((None, D_MODEL, tn), lambda l, j: (l, 0, j)),
                  pl.BlockSpec((None, 1, tn), lambda l, j: (l, 0, j))],
        out_specs=pl.BlockSpec((None, bc, tn), lambda l, j: (l, 0, j)),
        out_shape=jax.ShapeDtypeStruct((n_l, bc, n_out), F32),
        compiler_params=_cparams(("arbitrary", "arbitrary")),
        name="adaln",
    )(c_all, w, b.reshape(n_l, 1, n_out))


def _tile_rows(t):
    return 256 if t % 256 == 0 else t


def _tok_spec(tm, width=D_MODEL):
    return pl.BlockSpec((None, tm, width), lambda b, t: (b, t, 0))


def _per_batch_spec(rows, width=D_MODEL):
    return pl.BlockSpec((None, rows, width), lambda b, t: (b, 0, 0))


def _const_spec(shape):
    nd = len(shape)
    return pl.BlockSpec(shape, lambda b, t: (0,) * nd)


def _prep_out(bsz, t, tm):
    shapes = [jax.ShapeDtypeStruct((bsz, t, PAY_W), F32),
              jax.ShapeDtypeStruct((bsz, 8, t), jnp.int32),
              jax.ShapeDtypeStruct((CLS_ROWS, LANES), F32)]
    specs = [_tok_spec(tm, PAY_W),
             pl.BlockSpec((None, 8, tm), lambda b, t: (b, 0, t)),
             _const_spec((CLS_ROWS, LANES))]
    return shapes, specs


def _mixer_body(has_res, tm, start_pos, *refs):
    it = iter(refs)
    x_ref, xp_ref = next(it), next(it)
    if has_res:
        y_ref, yp_ref, g2p_ref = next(it), next(it), next(it)
    hist_ref, ada_ref, ng_ref, wp_ref, ps_ref, wrt_ref, br_ref = (next(it) for _ in range(7))
    x1_ref, pay_ref, info_ref, cnt_ref, hout_ref = (next(it) for _ in range(5))
    run_ref, ue_ref = next(it), next(it)

    b = pl.program_id(0)
    t = pl.program_id(1)

    @pl.when((b == 0) & (t == 0))
    def _():
        run_ref[...] = jnp.zeros_like(run_ref)

    xin = x_ref[...]
    xp = xp_ref[...]
    if has_res:
        g2p = g2p_ref[...]
        xin = xin + g2p * y_ref[...]
        xp = xp + g2p * yp_ref[...]
    ada = ada_ref[...]
    sh1, sc1, g1, sh2, sc2 = (ada[i:i + 1] for i in range(5))
    ng = ng_ref[...]
    u = _rms_mod(xin, ng[0:1], sh1, sc1)
    up = _rms_mod(xp, ng[0:1], sh1, sc1)
    up = jnp.where(t == 0, hist_ref[...], up)
    ue_ref[0:HIST_ROWS, :] = up
    ue_ref[HIST_ROWS:, :] = u

    pos = start_pos + t * tm + lax.broadcasted_iota(jnp.int32, (tm, 1), 0)
    cols = []
    for g, w in enumerate(POOL_WINDOWS):
        sl = slice(g * POOL_GROUP_DIM, (g + 1) * POOL_GROUP_DIM)
        s = u[:, sl]
        for j in range(1, w):
            s = s + ue_ref[HIST_ROWS - j:HIST_ROWS - j + tm, sl]
        cnt = jnp.minimum(pos + 1, w).astype(F32)
        cols.append(_bdot(s / cnt - u[:, sl], wp_ref[g]))
    y = jnp.concatenate(cols, axis=1) * ps_ref[...]
    x1 = xin + g1 * y
    x1_ref[...] = x1
    hout_ref[...] = ue_ref[tm + 1:tm + HIST_ROWS, :]
    _moe_prep(x1, ng[1:2], sh2, sc2, wrt_ref, br_ref, run_ref, pay_ref, info_ref, cnt_ref)


def _mixer_call(x, res, hist16, ada_l, ng_l, wp_l, ps_l, wrt, br, start_pos):
    bsz, t, _ = x.shape
    tm = _tile_rows(t)
    prev_spec = pl.BlockSpec((None, HIST_ROWS, D_MODEL),
                             lambda b, i: (b, jnp.maximum(i * (tm // HIST_ROWS) - 1, 0), 0))
    ins = [x, x]
    specs = [_tok_spec(tm), prev_spec]
    if res is not None:
        y, g2p = res
        ins += [y, y, g2p]
        specs += [_tok_spec(tm), prev_spec, _per_batch_spec(1)]
    ins += [hist16, ada_l, ng_l, wp_l, ps_l, wrt, br]
    specs += [_per_batch_spec(HIST_ROWS), _per_batch_spec(6), _const_spec((2, D_MODEL)),
              _const_spec((N_POOL_GROUPS, POOL_GROUP_DIM, POOL_GROUP_DIM)), _const_spec((1, D_MODEL)),
              _const_spec((N_EXPERTS, D_MODEL)), _const_spec((N_EXPERTS, 1))]
    p_shapes, p_specs = _prep_out(bsz, t, tm)
    return pl.pallas_call(
        functools.partial(_mixer_body, res is not None, tm, start_pos),
        grid=(bsz, t // tm),
        in_specs=specs,
        out_specs=[_tok_spec(tm)] + p_specs + [_per_batch_spec(POOL_BUF)],
        out_shape=[jax.ShapeDtypeStruct((bsz, t, D_MODEL), F32)] + p_shapes
                  + [jax.ShapeDtypeStruct((bsz, POOL_BUF, D_MODEL), F32)],
        scratch_shapes=[pltpu.VMEM((CLS_ROWS, LANES), F32), pltpu.VMEM((tm + HIST_ROWS, D_MODEL), F32)],
        compiler_params=_cparams(("arbitrary", "arbitrary")),
        name="pool_mixer",
    )(*ins)


def _proj_body(with_kv, tm, *refs):
    it = iter(refs)
    x1p_ref, y_ref, g2p_ref, ada_ref, ng_ref, wq_ref, qn_ref, s_ref, st_ref = (next(it) for _ in range(9))
    if with_kv:
        kva_ref, kvn_ref, wkv_ref, wf_ref, bf_ref, kn_ref = (next(it) for _ in range(6))
    x_ref, q_ref = next(it), next(it)
    if with_kv:
        k_ref, v_ref, kb_ref, vt_ref, lf_ref, lfw_ref = (next(it) for _ in range(6))

    x = x1p_ref[...] + g2p_ref[...] * y_ref[...]
    x_ref[...] = x
    ada = ada_ref[...]
    ng = ng_ref[...]
    h = _rms_mod(x, ng[0:1], ada[0:1], ada[1:2])
    q = _head_rms(_bdot(h, wq_ref[...]), s_ref, st_ref, qn_ref[...])
    for i, blk in enumerate(_query_blocks(q * (ATTN_SCALE * LOG2E))):
        q_ref[:, i * LANES:(i + 1) * LANES] = blk.astype(BF16)
    if with_kv:
        kva = kva_ref[...]
        hk = _rms_mod(x, kvn_ref[...], kva[0:1], kva[1:2])
        proj = _bdot(hk, wkv_ref[...])
        k = _head_rms(proj[:, :KV_WIDTH], s_ref, st_ref, kn_ref[...])
        v = proj[:, KV_WIDTH:]
        k_ref[...] = k
        v_ref[...] = v
        kb_ref[...] = k.astype(BF16)
        vt_ref[...] = _pad_rows(v, max(tm, LANES)).T[:, :tm].astype(BF16)
        z = _dot3(hk, wf_ref[...]) + bf_ref[...]
        lf = jnp.minimum(z, 0.0) - jnp.log(1.0 + jnp.exp(-jnp.abs(z)))
        lf_ref[...] = lf[:, :N_HEADS]
        lane = lax.broadcasted_iota(jnp.int32, (1, LANES), 1)
        lfw_ref[...] = jnp.where(lane < N_HEADS, lf, 0.0)


def _proj_call(x1p, y, g2p, ada_l, ng_l, wq, qn, s_mat, st_mat, kv=None):
    bsz, t, _ = x1p.shape
    tm = _tile_rows(t)
    ins = [x1p, y, g2p, ada_l, ng_l, wq, qn, s_mat, st_mat]
    specs = [_tok_spec(tm), _tok_spec(tm), _per_batch_spec(1), _per_batch_spec(6), _const_spec((2, D_MODEL)),
             _const_spec((D_MODEL, KV_WIDTH)), _const_spec((1, D_MODEL)),
             _const_spec((D_MODEL, LANES)), _const_spec((LANES, D_MODEL))]
    out_shapes = [jax.ShapeDtypeStruct((bsz, t, D_MODEL), F32), jax.ShapeDtypeStruct((bsz, t, PAD_W), BF16)]
    out_specs = [_tok_spec(tm), _tok_spec(tm, PAD_W)]
    if kv is not None:
        kva, kvn, wkv, wf, bf, kn = kv
        ins += [kva, kvn, wkv, wf, bf, kn]
        specs += [_per_batch_spec(2), _const_spec((1, D_MODEL)), _const_spec((D_MODEL, 2 * KV_WIDTH)),
                  _const_spec((D_MODEL, LANES)), _const_spec((1, LANES)), _const_spec((1, D_MODEL))]
        out_shapes += [jax.ShapeDtypeStruct((bsz, t, D_MODEL), F32)] * 2
        out_shapes += [jax.ShapeDtypeStruct((bsz, t, D_MODEL), BF16), jax.ShapeDtypeStruct((bsz, D_MODEL, t), BF16)]
        out_shapes += [jax.ShapeDtypeStruct((bsz, t, N_HEADS), F32), jax.ShapeDtypeStruct((bsz, t, LANES), F32)]
        out_specs += [_tok_spec(tm)] * 2
        out_specs += [_tok_spec(tm), pl.BlockSpec((None, D_MODEL, tm), lambda b, i: (b, 0, i))]
        out_specs += [_tok_spec(tm, N_HEADS), _tok_spec(tm, LANES)]
    return pl.pallas_call(
        functools.partial(_proj_body, kv is not None, tm),
        grid=(bsz, t // tm),
        in_specs=specs, out_specs=out_specs, out_shape=out_shapes,
        compiler_params=_cparams(("arbitrary", "arbitrary")),
        name="qkv_proj" if kv is not None else "q_proj",
    )(*ins)


def _decay_body(tc, lf_ref, o_ref, carry_ref):
    @pl.when(pl.program_id(1) == 0)
    def _():
        carry_ref[...] = jnp.zeros_like(carry_ref)

    lf = lf_ref[...]
    r = lax.broadcasted_iota(jnp.int32, (tc, tc), 0)
    c = lax.broadcasted_iota(jnp.int32, (tc, tc), 1)
    lower = jnp.where(r >= c, 1.0, 0.0).astype(BF16)
    hi, lo = _split(lf)
    f = (jnp.dot(lower, hi, preferred_element_type=F32) + jnp.dot(lower, lo, preferred_element_type=F32)
         + carry_ref[0:1, :])
    carry_ref[...] = jnp.broadcast_to(f[tc - 1:tc, :], carry_ref.shape)
    nb = f * (-LOG2E)
    p1 = nb.astype(BF16)
    r1 = nb - p1.astype(F32)
    p2 = r1.astype(BF16)
    p3 = (r1 - p2.astype(F32)).astype(BF16)
    hr = lax.broadcasted_iota(jnp.int32, (LANES, LANES), 0)
    lc = lax.broadcasted_iota(jnp.int32, (LANES, LANES), 1)
    out = None
    for i, p in enumerate((p1, p2, p3)):
        place = jnp.where((lc == DECAY_PIECES * hr + i) & (hr < N_HEADS), 1.0, 0.0).astype(BF16)
        term = jnp.dot(p, place, preferred_element_type=F32)
        out = term if out is None else out + term
    o_ref[...] = out.astype(BF16)


def _decay_call(lfw):
    bsz, tk, _ = lfw.shape
    tc = 512
    spec = pl.BlockSpec((None, tc, LANES), lambda b, t: (b, t, 0))
    return pl.pallas_call(
        functools.partial(_decay_body, tc),
        grid=(bsz, tk // tc), in_specs=[spec], out_specs=spec,
        out_shape=jax.ShapeDtypeStruct(lfw.shape, BF16),
        scratch_shapes=[pltpu.VMEM((8, LANES), F32)],
        compiler_params=_cparams(("arbitrary", "arbitrary")),
        name="decay_bias",
    )(lfw)


def _transpose_body(v_ref, o_ref):
    o_ref[...] = v_ref[...].T.astype(BF16)


def _transpose_call(v):
    bsz, p, _ = v.shape
    tp = 512
    return pl.pallas_call(
        _transpose_body, grid=(bsz, p // tp),
        in_specs=[pl.BlockSpec((None, tp, D_MODEL), lambda b, t: (b, t, 0))],
        out_specs=pl.BlockSpec((None, D_MODEL, tp), lambda b, t: (b, 0, t)),
        out_shape=jax.ShapeDtypeStruct((bsz, D_MODEL, p), BF16),
        compiler_params=_cparams(("arbitrary", "arbitrary")),
        name="value_transpose",
    )(v)


def _flash_body(tq, tkp, tks, n_past, n_self, *refs):
    it = iter(refs)
    q_ref = next(it)
    if n_past:
        kp_ref, vtp_ref, fpp_ref = next(it), next(it), next(it)
    ks_ref, vts_ref, fps_ref = next(it), next(it), next(it)
    o_ref, m_ref, l_ref, acc_ref = next(it), next(it), next(it), next(it)
    qi = pl.program_id(1)
    step = pl.program_id(2)

    @pl.when(step == 0)
    def _():
        m_ref[...] = jnp.full_like(m_ref, NEG_INF)
        l_ref[...] = jnp.zeros_like(l_ref)
        acc_ref[...] = jnp.zeros_like(acc_ref)

    def process(k_ref, vt_ref, fp_ref, tk, key_base):
        fp = fp_ref[...]
        lane = lax.broadcasted_iota(jnp.int32, (1, LANES), 1)
        if key_base is not None:
            kpos = key_base + lax.broadcasted_iota(jnp.int32, (tk, 1), 0)
            qpos = qi * tq + lax.broadcasted_iota(jnp.int32, (1, tq), 1)
            visible = kpos <= qpos

        pair_keys = {}

        def logits(h):
            d = lane - DECAY_PIECES * h
            ones = jnp.where((d >= 0) & (d < DECAY_PIECES), 1.0, 0.0).astype(BF16)
            hp = h // 2
            if hp not in pair_keys:
                pair_keys[hp] = k_ref[:, hp * LANES:(hp + 1) * LANES].astype(BF16)
            lhs = jnp.concatenate([pair_keys[hp], fp], axis=1)
            rhs = jnp.concatenate([q_ref[:, h * LANES:(h + 1) * LANES],
                                   jnp.broadcast_to(ones, (tq, LANES))], axis=1)
            s = lax.dot_general(lhs, rhs, _NT, preferred_element_type=F32)
            return s if key_base is None else jnp.where(visible, s, NEG_INF)

        ones_rows = jnp.ones((SUM_ROWS, tk), BF16)
        s_next = logits(0)
        for h in range(N_HEADS):
            s = s_next
            if h + 1 < N_HEADS:
                s_next = logits(h + 1)
            rows = slice(h * HEAD_DIM, (h + 1) * HEAD_DIM)
            m_old = m_ref[h]
            m_new = jnp.maximum(m_old, jnp.max(s, axis=0, keepdims=True))
            alpha = jnp.exp2(m_old - m_new)
            p = jnp.exp2(s - m_new).astype(BF16)
            m_ref[h] = m_new
            pv = jnp.dot(jnp.concatenate([vt_ref[rows, :], ones_rows], axis=0), p,
                         preferred_element_type=F32)
            l_ref[h] = alpha * l_ref[h] + pv[HEAD_DIM:HEAD_DIM + 1, :]
            acc_ref[rows, :] = alpha * acc_ref[rows, :] + pv[:HEAD_DIM, :]

    if n_past:
        @pl.when(step < n_past)
        def _():
            process(kp_ref, vtp_ref, fpp_ref, tkp, None)

    j = step - n_past
    last = (qi * tq + tq - 1) // tks
    has_hidden = (j + 1) * tks - 1 > qi * tq

    @pl.when((j >= 0) & (j <= last) & has_hidden)
    def _():
        process(ks_ref, vts_ref, fps_ref, tks, j * tks)

    @pl.when((j >= 0) & (j <= last) & jnp.logical_not(has_hidden))
    def _():
        process(ks_ref, vts_ref, fps_ref, tks, None)

    @pl.when(step == n_past + n_self - 1)
    def _():
        r = lax.broadcasted_iota(jnp.int32, (tq, tq), 0)
        c = lax.broadcasted_iota(jnp.int32, (tq, tq), 1)
        eye = jnp.where(r == c, 1.0, 0.0).astype(BF16)
        row = lax.broadcasted_iota(jnp.int32, (LANES, 1), 0)
        for hp in range(N_HEADS // 2):
            denom = jnp.where(row < HEAD_DIM, l_ref[2 * hp], l_ref[2 * hp + 1])
            o_t = (acc_ref[hp * LANES:(hp + 1) * LANES, :] / denom).astype(BF16)
            o_ref[:, hp * LANES:(hp + 1) * LANES] = lax.dot_general(
                eye, o_t, _NT, preferred_element_type=F32).astype(BF16)


def _flash_call(q, k_self, vt_self, past, fp, n_past_keys):
    bsz, t_q, _ = q.shape
    tq = 512 if t_q % 512 == 0 else t_q
    tks = 256 if t_q % 256 == 0 else t_q
    n_self = t_q // tks
    tkp = 512
    n_past = n_past_keys // tkp
    fp_self_base = n_past_keys // tks

    def self_idx(i, s):
        return jnp.clip(s - n_past, 0, (i * tq + tq - 1) // tks)

    ins = [q]
    specs = [pl.BlockSpec((None, tq, PAD_W), lambda b, i, s: (b, i, 0))]
    if n_past:
        past_idx = lambda s: jnp.minimum(s, n_past - 1)
        ins += [past[0], past[1], fp]
        specs += [pl.BlockSpec((None, tkp, D_MODEL), lambda b, i, s: (b, past_idx(s), 0)),
                  pl.BlockSpec((None, D_MODEL, tkp), lambda b, i, s: (b, 0, past_idx(s))),
                  pl.BlockSpec((None, tkp, LANES), lambda b, i, s: (b, past_idx(s), 0))]
    ins += [k_self, vt_self, fp]
    specs += [pl.BlockSpec((None, tks, D_MODEL), lambda b, i, s: (b, self_idx(i, s), 0)),
              pl.BlockSpec((None, D_MODEL, tks), lambda b, i, s: (b, 0, self_idx(i, s))),
              pl.BlockSpec((None, tks, LANES), lambda b, i, s: (b, fp_self_base + self_idx(i, s), 0))]
    return pl.pallas_call(
        functools.partial(_flash_body, tq, tkp, tks, n_past, n_self),
        grid=(bsz, t_q // tq, n_past + n_self),
        in_specs=specs,
        out_specs=pl.BlockSpec((None, tq, D_MODEL), lambda b, i, s: (b, i, 0)),
        out_shape=jax.ShapeDtypeStruct((bsz, t_q, D_MODEL), BF16),
        scratch_shapes=[pltpu.VMEM((N_HEADS, 1, tq), F32), pltpu.VMEM((N_HEADS, 1, tq), F32),
                        pltpu.VMEM((D_MODEL, tq), F32)],
        compiler_params=_cparams(("arbitrary", "arbitrary", "arbitrary")),
        name="fox_attention",
    )(*ins)


def _oproj_body(x_ref, o_ref, wo_ref, ada_ref, ng_ref, wrt_ref, br_ref,
                x1_ref, pay_ref, info_ref, cnt_ref, run_ref):
    @pl.when((pl.program_id(0) == 0) & (pl.program_id(1) == 0))
    def _():
        run_ref[...] = jnp.zeros_like(run_ref)

    ada = ada_ref[...]
    x1 = x_ref[...] + ada[2:3] * jnp.dot(o_ref[...], wo_ref[...], preferred_element_type=F32)
    x1_ref[...] = x1
    _moe_prep(x1, ng_ref[...][1:2], ada[3:4], ada[4:5], wrt_ref, br_ref, run_ref, pay_ref, info_ref, cnt_ref)


def _oproj_call(x, o, wo, ada_l, ng_l, wrt, br):
    bsz, t, _ = x.shape
    tm = _tile_rows(t)
    p_shapes, p_specs = _prep_out(bsz, t, tm)
    return pl.pallas_call(
        _oproj_body,
        grid=(bsz, t // tm),
        in_specs=[_tok_spec(tm), _tok_spec(tm), _const_spec((KV_WIDTH, D_MODEL)), _per_batch_spec(6),
                  _const_spec((2, D_MODEL)), _const_spec((N_EXPERTS, D_MODEL)), _const_spec((N_EXPERTS, 1))],
        out_specs=[_tok_spec(tm)] + p_specs,
        out_shape=[jax.ShapeDtypeStruct((bsz, t, D_MODEL), F32)] + p_shapes,
        scratch_shapes=[pltpu.VMEM((CLS_ROWS, LANES), F32)],
        compiler_params=_cparams(("arbitrary", "arbitrary")),
        name="attn_out_proj",
    )(x, o, wo, ada_l, ng_l, wrt, br)


def _final_body(x_ref, y_ref, g_ref, o_ref):
    o_ref[...] = x_ref[...] + g_ref[...] * y_ref[...]


def _final_call(x1, y, g2):
    bsz, t, _ = x1.shape
    tm = 1024 if t % 1024 == 0 else _tile_rows(t)
    return pl.pallas_call(
        _final_body, grid=(bsz, t // tm),
        in_specs=[_tok_spec(tm), _tok_spec(tm), _per_batch_spec(1)],
        out_specs=_tok_spec(tm),
        out_shape=jax.ShapeDtypeStruct(x1.shape, F32),
        compiler_params=_cparams(("arbitrary", "arbitrary")),
        name="final_residual",
    )(x1, y, g2)


def _sc_worker_loop(n_win, fn):
    wid = lax.axis_index("s") * SC_CORES + lax.axis_index("c")
    n_workers = SC_CORES * SC_SUBCORES

    @pl.loop(0, pl.cdiv(n_win, n_workers))
    def _(j):
        win = j * n_workers + wid

        @pl.when(win < n_win)
        def _():
            fn(pl.multiple_of(win * SC_WINDOW, SC_WINDOW))


def _sc_scatter_rows(rows, idx, n_out):
    n, width = rows.shape
    mesh = plsc.VectorSubcoreMesh(core_axis_name="c", subcore_axis_name="s")

    @functools.partial(
        pl.kernel, mesh=mesh, out_type=jax.ShapeDtypeStruct((n_out, width), rows.dtype),
        scratch_types=[pltpu.VMEM((SC_WINDOW,), jnp.int32), pltpu.VMEM((SC_WINDOW, width), rows.dtype)])
    def k(rows_hbm, idx_hbm, out_hbm, idx_v, rows_v):
        def one(base):
            pltpu.sync_copy(idx_hbm.at[pl.ds(base, SC_WINDOW)], idx_v)
            pltpu.sync_copy(rows_hbm.at[pl.ds(base, SC_WINDOW)], rows_v)
            pltpu.sync_copy(rows_v, out_hbm.at[idx_v])
        _sc_worker_loop(n // SC_WINDOW, one)

    return k(rows, idx)


def _sc_gather_rows(table, idx):
    n = idx.shape[0]
    width = table.shape[1]
    mesh = plsc.VectorSubcoreMesh(core_axis_name="c", subcore_axis_name="s")

    @functools.partial(
        pl.kernel, mesh=mesh, out_type=jax.ShapeDtypeStruct((n, width), table.dtype),
        scratch_types=[pltpu.VMEM((SC_WINDOW,), jnp.int32), pltpu.VMEM((SC_WINDOW, width), table.dtype)])
    def k(table_hbm, idx_hbm, out_hbm, idx_v, rows_v):
        def one(base):
            pltpu.sync_copy(idx_hbm.at[pl.ds(base, SC_WINDOW)], idx_v)
            pltpu.sync_copy(table_hbm.at[idx_v], rows_v)
            pltpu.sync_copy(rows_v, out_hbm.at[pl.ds(base, SC_WINDOW)])
        _sc_worker_loop(n // SC_WINDOW, one)

    return k(table, idx)


def _moe_body(e1_ref, e2_ref, na_ref, hs_ref, g1_ref, g2_ref, u1_ref, u2_ref, d1_ref, d2_ref, o_ref):
    @pl.when(pl.program_id(0) < na_ref[0])
    def _():
        blk = hs_ref[...]
        h = _unpack_bf16_pairs(blk[:, :PACK_W])
        y = None
        for lane, (g_ref, u_ref, d_ref) in enumerate(((g1_ref, u1_ref, d1_ref), (g2_ref, u2_ref, d2_ref))):
            gate = jnp.dot(h, g_ref[...], preferred_element_type=F32)
            up = jnp.dot(h, u_ref[...], preferred_element_type=F32)
            w = blk[:, PACK_W + lane:PACK_W + lane + 1]
            act = (gate * jax.nn.sigmoid(gate) * up * w).astype(BF16)
            term = jnp.dot(act, d_ref[...], preferred_element_type=F32)
            y = term if y is None else y + term
        o_ref[...] = y


def _moe_call(hs, tile_e1, tile_e2, n_active, wg, wu, wd, layer, tm):
    n_s = hs.shape[0]
    n_tiles = n_s // tm
    base = layer * N_EXPERTS

    def row_map(i, e1, e2, na):
        return (jnp.minimum(i, na[0] - 1), 0)

    def w_map(which):
        def m(i, e1, e2, na):
            e = (e1, e2)[which]
            return (base + e[jnp.minimum(i, na[0] - 1)], 0, 0)
        return m

    gu = lambda which: pl.BlockSpec((None, D_MODEL, D_EXPERT), w_map(which))
    dn = lambda which: pl.BlockSpec((None, D_EXPERT, D_MODEL), w_map(which))
    return pl.pallas_call(
        _moe_body,
        grid_spec=pltpu.PrefetchScalarGridSpec(
            num_scalar_prefetch=3, grid=(n_tiles,),
            in_specs=[pl.BlockSpec((tm, PAY_W), row_map), gu(0), gu(1), gu(0), gu(1), dn(0), dn(1)],
            out_specs=pl.BlockSpec((tm, D_MODEL), row_map)),
        out_shape=jax.ShapeDtypeStruct((n_s, D_MODEL), F32),
        compiler_params=_cparams(("arbitrary",)),
        name="grouped_experts",
    )(tile_e1, tile_e2, n_active, hs, wg, wg, wu, wu, wd, wd)


_PAIR_LO = (0, 0, 0, 1, 1, 2)
_PAIR_HI = (1, 2, 3, 2, 3, 3)


def _moe_layer(pay, info, counts, wg, wu, wd, layer):
    bsz, t, _ = pay.shape
    n = bsz * t
    tm = 256
    n_s = ((n + N_CLASSES * (tm - 1)) // tm + 1) * tm
    n_tiles = n_s // tm
    cls = info[:, 0, :].reshape(n)
    rank = info[:, 1, :].reshape(n)
    cnt = counts[:N_CLASSES, 0].astype(jnp.int32)
    padded = ((cnt + tm - 1) // tm) * tm
    ends = jnp.cumsum(padded)
    starts = ends - padded
    dest = starts[cls] + rank
    tile_start = jnp.arange(n_tiles, dtype=jnp.int32) * tm
    tile_cls = jnp.minimum(jnp.sum((tile_start[:, None] >= ends[None, :]).astype(jnp.int32), axis=1),
                           N_CLASSES - 1)
    grp = tile_cls // N_PAIRS
    pr = tile_cls % N_PAIRS
    tile_e1 = grp * EXPERTS_PER_GROUP + jnp.asarray(_PAIR_LO, jnp.int32)[pr]
    tile_e2 = grp * EXPERTS_PER_GROUP + jnp.asarray(_PAIR_HI, jnp.int32)[pr]
    n_active = (ends[-1:] // tm).astype(jnp.int32)
    hs = _sc_scatter_rows(pay.reshape(n, PAY_W), dest, n_s)
    ys = _moe_call(hs, tile_e1, tile_e2, n_active, wg, wu, wd, layer, tm)
    return _sc_gather_rows(ys, dest).reshape(bsz, t, D_MODEL)


def _trunk(x, ada, kva, hist, past, prm):
    bsz, t, _ = x.shape
    start_pos = 0 if past is None else past[0].shape[1]
    wrt, br = prm["wrt"], prm["br"]
    res = None
    new_hist = []
    for layer in range(N_A_LAYERS):
        if hist is None:
            h16 = jnp.zeros((bsz, HIST_ROWS, D_MODEL), F32)
        else:
            h16 = jnp.pad(hist[layer], ((0, 0), (1, 0), (0, 0)))
        x, pay, info, counts, hout = _mixer_call(
            x, res, h16, ada[layer], prm["norm_g"][layer], prm["w_pool"][layer], prm["pool_scale"][layer],
            wrt, br, start_pos)
        new_hist.append(hout)
        y = _moe_layer(pay, info, counts, prm["w_gate"], prm["w_up"], prm["w_down"], layer)
        res = (y, ada[layer][:, 5:6, :])

    shared = None
    for j in range(N_B_LAYERS):
        layer = N_A_LAYERS + j
        y, g2p = res
        if j == 0:
            kv = (kva, prm["kv_norm"], prm["w_kv"], prm["w_f"], prm["b_f"], prm["k_norm"])
            xr, q, k, v, kb, vt, logf, lfw = _proj_call(
                x, y, g2p, ada[layer], prm["norm_g"][layer], prm["w_q"][j], prm["q_norm"][j],
                prm["s_mat"], prm["st_mat"], kv)
            if past is None:
                past_kv = None
                fp = _decay_call(lfw)
            else:
                ck, cv, clogf = past
                n_past = ck.shape[1]
                pad = -(n_past + t) % 512
                lfw = jnp.concatenate([jnp.pad(clogf, ((0, 0), (0, 0), (0, LANES - N_HEADS))), lfw,
                                       jnp.zeros((bsz, pad, LANES), F32)], axis=1)
                fp = _decay_call(lfw)
                past_kv = (ck.reshape(bsz, n_past, D_MODEL), _transpose_call(cv.reshape(bsz, n_past, D_MODEL)))
            shared = (k, v, logf, kb, vt, past_kv, fp)
        else:
            xr, q = _proj_call(x, y, g2p, ada[layer], prm["norm_g"][layer], prm["w_q"][j], prm["q_norm"][j],
                               prm["s_mat"], prm["st_mat"])
        o = _flash_call(q, shared[3], shared[4], shared[5], shared[6], start_pos)
        x, pay, info, counts = _oproj_call(xr, o, prm["w_o"][j], ada[layer], prm["norm_g"][layer], wrt, br)
        y = _moe_layer(pay, info, counts, prm["w_gate"], prm["w_up"], prm["w_down"], layer)
        res = (y, ada[layer][:, 5:6, :])

    out = _final_call(x, res[0], res[1])
    k, v, logf = shared[:3]
    return (out, jnp.stack(new_hist), k.reshape(bsz, t, N_HEADS, HEAD_DIM),
            v.reshape(bsz, t, N_HEADS, HEAD_DIM), logf)


def kernel(x_prompt, x_sample, cache_pool, cache_k, cache_v, cache_logf, c_prompt, c_sample, ada_w, ada_b, norm_g, w_pool, pool_scale, kv_ada_w, kv_ada_b, kv_norm, w_kvf, b_f, k_norm, w_q, q_norm, w_o, w_router, b_router, w_gate, w_up, w_down):
    bp = x_prompt.shape[0]
    c_all = jnp.concatenate([c_prompt, c_sample], axis=0)
    bc = c_all.shape[0]
    ada = _ada_call(c_all, ada_w, ada_b).reshape(DEPTH, bc, 6, D_MODEL)
    kva = _ada_call(c_all, kv_ada_w[None], kv_ada_b[None]).reshape(bc, 2, D_MODEL)

    head_of_lane = jnp.arange(D_MODEL, dtype=jnp.int32) // HEAD_DIM
    s_mat = (head_of_lane[:, None] == jnp.arange(LANES, dtype=jnp.int32)[None, :]).astype(BF16)
    prm = {
        "norm_g": norm_g,
        "w_pool": w_pool.astype(BF16),
        "pool_scale": pool_scale.reshape(N_A_LAYERS, 1, D_MODEL),
        "kv_norm": kv_norm.reshape(1, D_MODEL),
        "w_kv": w_kvf[:, :2 * KV_WIDTH].astype(BF16),
        "w_f": jnp.pad(w_kvf[:, 2 * KV_WIDTH:], ((0, 0), (0, LANES - N_HEADS))),
        "b_f": jnp.pad(b_f, (0, LANES - N_HEADS)).reshape(1, LANES),
        "k_norm": jnp.tile(k_norm, N_HEADS).reshape(1, D_MODEL),
        "w_q": w_q.astype(BF16),
        "q_norm": jnp.tile(q_norm, (1, N_HEADS)).reshape(N_B_LAYERS, 1, D_MODEL),
        "w_o": w_o.astype(BF16),
        "wrt": w_router.T,
        "br": b_router.reshape(N_EXPERTS, 1),
        "w_gate": w_gate.astype(BF16).reshape(DEPTH * N_EXPERTS, D_MODEL, D_EXPERT),
        "w_up": w_up.astype(BF16).reshape(DEPTH * N_EXPERTS, D_MODEL, D_EXPERT),
        "w_down": w_down.astype(BF16).reshape(DEPTH * N_EXPERTS, D_EXPERT, D_MODEL),
        "s_mat": s_mat,
        "st_mat": s_mat.T,
    }
    outs_p = _trunk(x_prompt, ada[:, :bp], kva[:bp], None, None, prm)
    outs_s = _trunk(x_sample, ada[:, bp:], kva[bp:], cache_pool, (cache_k, cache_v, cache_logf), prm)
    return (outs_p[0], outs_s[0]) + outs_p[1:] + outs_s[1:]
```

```python
import functools

import jax
import jax.numpy as jnp
from jax import lax
from jax.experimental import pallas as pl
from jax.experimental.pallas import tpu as pltpu
from jax.experimental.pallas import tpu_sc as plsc

F32 = jnp.float32
BF16 = jnp.bfloat16

D_MODEL = 1024
DEPTH = 4
N_A_LAYERS = DEPTH // 2
N_B_LAYERS = DEPTH - N_A_LAYERS
POOL_WINDOWS = (2, 4, 8, 16)
N_POOL_GROUPS = len(POOL_WINDOWS)
POOL_GROUP_DIM = D_MODEL // N_POOL_GROUPS
POOL_BUF = max(POOL_WINDOWS) - 1
HIST_ROWS = POOL_BUF + 1
N_HEADS = 16
HEAD_DIM = D_MODEL // N_HEADS
KV_WIDTH = N_HEADS * HEAD_DIM
ATTN_SCALE = HEAD_DIM ** -0.5
N_EXPERTS = 16
N_EXPERT_GROUPS = 4
EXPERTS_PER_GROUP = N_EXPERTS // N_EXPERT_GROUPS
N_PAIRS = 6
N_CLASSES = N_EXPERT_GROUPS * N_PAIRS
D_EXPERT = D_MODEL // 2
EPS = 1e-6
NEG_INF = -1e30

LANES = 128
SC_CORES = 2
SC_SUBCORES = 16
SC_WINDOW = 64
PACK_W = D_MODEL // 2
PAY_W = PACK_W + LANES
PAD_W = N_HEADS * LANES
LOG2E = 1.4426950408889634
DECAY_PIECES = 3
SUM_ROWS = 16
CLS_ROWS = 32
VMEM_LIMIT = 48 * 1024 * 1024


def _cparams(sem, flags=None):
    return pltpu.CompilerParams(dimension_semantics=sem, vmem_limit_bytes=VMEM_LIMIT, flags=flags)


def _bdot(a, b):
    return jnp.dot(a.astype(BF16), b.astype(BF16), preferred_element_type=F32)


def _split(a):
    hi = a.astype(BF16)
    lo = (a - hi.astype(F32)).astype(BF16)
    return hi, lo


_NN = (((1,), (0,)), ((), ()))
_NT = (((1,), (1,)), ((), ()))


def _dot3(a, b, dims=_NN):
    ah, al = _split(a)
    bh, bl = _split(b)
    d = lambda x, y: lax.dot_general(x, y, dims, preferred_element_type=F32)
    return d(ah, bh) + (d(ah, bl) + d(al, bh))


def _dot2_exact_rhs(a, b_bf16):
    ah, al = _split(a)
    return (jnp.dot(ah, b_bf16, preferred_element_type=F32)
            + jnp.dot(al, b_bf16, preferred_element_type=F32))


def _rms_mod(x, gain, shift, scale):
    ms = jnp.mean(x * x, axis=-1, keepdims=True)
    return (x * lax.rsqrt(ms + EPS)) * (gain * (1.0 + scale)) + shift


def _head_rms(z, s_ref, st_ref, gain):
    ss = _dot2_exact_rhs(z * z, s_ref[...])
    inv = lax.rsqrt(ss * (1.0 / HEAD_DIM) + EPS)
    invf = _dot2_exact_rhs(inv, st_ref[...])
    return z * invf * gain


def _query_blocks(q):
    lane = lax.broadcasted_iota(jnp.int32, (1, LANES), 1)
    blocks = []
    for h in range(N_HEADS):
        pair = q[:, (h // 2) * LANES:(h // 2 + 1) * LANES]
        in_head = (lane >= HEAD_DIM) if h % 2 else (lane < HEAD_DIM)
        blocks.append(jnp.where(in_head, pair, 0.0))
    return blocks


_HI_MASK = 0xFFFF0000


def _pack_bf16_pairs(x):
    bits = lambda a: lax.bitcast_convert_type(a.astype(BF16).astype(F32), jnp.uint32)
    half = x.shape[1] // 2
    word = (bits(x[:, :half]) >> 16) | (bits(x[:, half:]) & jnp.uint32(_HI_MASK))
    return lax.bitcast_convert_type(word, F32)


def _unpack_bf16_pairs(w):
    word = lax.bitcast_convert_type(w, jnp.uint32)
    lo = lax.bitcast_convert_type(word << 16, F32).astype(BF16)
    hi = lax.bitcast_convert_type(word & jnp.uint32(_HI_MASK), F32).astype(BF16)
    return jnp.concatenate([lo, hi], axis=1)


def _pad_rows(a, rows):
    if a.shape[0] == rows:
        return a
    return jnp.concatenate([a, jnp.zeros((rows - a.shape[0], a.shape[1]), a.dtype)], axis=0)


def _route(lt, br):
    m = jnp.max(lt, axis=0, keepdims=True)
    p = jnp.exp(lt - m)
    scores = p / jnp.sum(p, axis=0, keepdims=True)
    sel = scores + br
    row = lambda a, e: a[e:e + 1, :]
    gs = []
    for g in range(N_EXPERT_GROUPS):
        v = [row(sel, g * EXPERTS_PER_GROUP + j) for j in range(EXPERTS_PER_GROUP)]
        best = None
        for i in range(EXPERTS_PER_GROUP):
            for j in range(i + 1, EXPERTS_PER_GROUP):
                s = v[i] + v[j]
                best = s if best is None else jnp.maximum(best, s)
        gs.append(best)
    bg = jnp.zeros_like(gs[0])
    bv = gs[0]
    for g in range(1, N_EXPERT_GROUPS):
        better = gs[g] > bv
        bg = jnp.where(better, float(g), bg)
        bv = jnp.where(better, gs[g], bv)

    def in_group(a, j):
        out = row(a, j)
        for g in range(1, N_EXPERT_GROUPS):
            out = jnp.where(bg == float(g), row(a, g * EXPERTS_PER_GROUP + j), out)
        return out

    sg = [in_group(sel, j) for j in range(EXPERTS_PER_GROUP)]
    cg = [in_group(scores, j) for j in range(EXPERTS_PER_GROUP)]

    def first_argmax(vals):
        mx = vals[0]
        for v in vals[1:]:
            mx = jnp.maximum(mx, v)
        idx = jnp.full_like(mx, float(len(vals) - 1))
        for j in range(len(vals) - 2, -1, -1):
            idx = jnp.where(vals[j] == mx, float(j), idx)
        return idx

    i1 = first_argmax(sg)
    i2 = first_argmax([jnp.where(i1 == float(j), -jnp.inf, sg[j]) for j in range(EXPERTS_PER_GROUP)])
    lo = jnp.minimum(i1, i2)
    hi = jnp.maximum(i1, i2)

    def pick(vals, idx):
        out = vals[0]
        for j in range(1, len(vals)):
            out = jnp.where(idx == float(j), vals[j], out)
        return out

    c_lo = pick(cg, lo)
    c_hi = pick(cg, hi)
    tot = c_lo + c_hi
    pair = jnp.where(lo == 0.0, hi - 1.0, jnp.where(lo == 1.0, hi + 1.0, 5.0))
    return bg * float(N_PAIRS) + pair, c_lo / tot, c_hi / tot


def _moe_prep(x1, ng2, sh2, sc2, wrt_ref, br_ref, run_ref, pay_ref, info_ref, cnt_ref):
    tm = x1.shape[0]
    tr = max(tm, LANES)
    h2 = _rms_mod(x1, ng2, sh2, sc2)
    lt = _dot3(wrt_ref[...], _pad_rows(h2, tr), _NT)
    cls, w_lo, w_hi = _route(lt, br_ref[...])

    r = lax.broadcasted_iota(jnp.int32, (LANES, tr), 0)
    wrows = jnp.where(r == 0, w_lo, jnp.where(r == 1, w_hi, 0.0))
    pay_ref[:, :PACK_W] = _pack_bf16_pairs(h2)
    pay_ref[:, PACK_W:] = wrows.T[:tm, :]

    crow = lax.broadcasted_iota(jnp.int32, (CLS_ROWS, tr), 0).astype(F32)
    lane = lax.broadcasted_iota(jnp.int32, (CLS_ROWS, tr), 1)
    onehot = jnp.where((crow == cls) & (lane < tm), 1.0, 0.0)
    us = lax.broadcasted_iota(jnp.int32, (tr, tr), 0)
    ut = lax.broadcasted_iota(jnp.int32, (tr, tr), 1)
    upper = jnp.where(us < ut, 1.0, 0.0).astype(BF16)
    before = jnp.dot(onehot.astype(BF16), upper, preferred_element_type=F32) + run_ref[:, 0:1]
    rank = jnp.sum(onehot * before, axis=0, keepdims=True)
    run_new = run_ref[...] + jnp.sum(onehot, axis=1, keepdims=True)
    run_ref[...] = run_new
    cnt_ref[...] = run_new
    ir = lax.broadcasted_iota(jnp.int32, (8, tr), 0)
    info = jnp.where(ir == 0, cls, jnp.where(ir == 1, rank, 0.0)).astype(jnp.int32)
    info_ref[...] = info[:, :tm]


def _ada_body(c_ref, w_ref, b_ref, o_ref):
    c = c_ref[...]
    o_ref[...] = _dot3(c * jax.nn.sigmoid(c), w_ref[...]) + b_ref[...]


def _ada_call(c_all, w, b):
    n_l, _, n_out = w.shape
    bc = c_all.shape[0]
    tn = 1536 if n_out % 1536 == 0 else 1024
    return pl.pallas_call(
        _ada_body,
        grid=(n_l, n_out // tn),
        in_specs=[pl.BlockSpec((bc, D_MODEL), lambda l, j: (0, 0)),
                  pl.BlockSpec((None, D_MODEL, tn), lambda l, j: (l, 0, j)),
                  pl.BlockSpec((None, 1, tn), lambda l, j: (l, 0, j))],
        out_specs=pl.BlockSpec((None, bc, tn), lambda l, j: (l, 0, j)),
        out_shape=jax.ShapeDtypeStruct((n_l, bc, n_out), F32),
        compiler_params=_cparams(("arbitrary", "arbitrary")),
        name="adaln",
    )(c_all, w, b.reshape(n_l, 1, n_out))


def _tile_rows(t, rows=512):
    return rows if t % rows == 0 else t


def _tok_spec(tm, width=D_MODEL):
    return pl.BlockSpec((None, tm, width), lambda b, t: (b, t, 0))


def _per_batch_spec(rows, width=D_MODEL):
    return pl.BlockSpec((None, rows, width), lambda b, t: (b, 0, 0))


def _const_spec(shape):
    nd = len(shape)
    return pl.BlockSpec(shape, lambda b, t: (0,) * nd)


def _prep_out(bsz, t, tm):
    shapes = [jax.ShapeDtypeStruct((bsz, t, PAY_W), F32),
              jax.ShapeDtypeStruct((bsz, 8, t), jnp.int32),
              jax.ShapeDtypeStruct((CLS_ROWS, LANES), F32)]
    specs = [_tok_spec(tm, PAY_W),
             pl.BlockSpec((None, 8, tm), lambda b, t: (b, 0, t)),
             _const_spec((CLS_ROWS, LANES))]
    return shapes, specs


def _mixer_body(has_res, tm, start_pos, *refs):
    it = iter(refs)
    x_ref, xp_ref = next(it), next(it)
    if has_res:
        y_ref, yp_ref, g2p_ref = next(it), next(it), next(it)
    hist_ref, ada_ref, ng_ref, wp_ref, ps_ref, wrt_ref, br_ref = (next(it) for _ in range(7))
    x1_ref, pay_ref, info_ref, cnt_ref, hout_ref = (next(it) for _ in range(5))
    run_ref = next(it)

    b = pl.program_id(0)
    t = pl.program_id(1)

    @pl.when((b == 0) & (t == 0))
    def _():
        run_ref[...] = jnp.zeros_like(run_ref)

    xin = x_ref[...]
    xp = xp_ref[...]
    if has_res:
        g2p = g2p_ref[...]
        xin = xin + g2p * y_ref[...]
        xp = xp + g2p * yp_ref[...]
    ada = ada_ref[...]
    sh1, sc1, g1, sh2, sc2 = (ada[i:i + 1] for i in range(5))
    ng = ng_ref[...]
    u = _rms_mod(xin, ng[0:1], sh1, sc1)
    up = _rms_mod(xp, ng[0:1], sh1, sc1)
    up = jnp.where(t == 0, hist_ref[...], up)
    level = jnp.concatenate([up, u], axis=0)
    sums = []
    for g, w in enumerate(POOL_WINDOWS):
        level = level + pltpu.roll(level, w // 2, 0)
        sums.append(level[HIST_ROWS:, :POOL_GROUP_DIM])
        if g + 1 < N_POOL_GROUPS:
            level = level[:, POOL_GROUP_DIM:]

    pos = start_pos + t * tm + lax.broadcasted_iota(jnp.int32, (tm, 1), 0)
    cols = []
    for g, w in enumerate(POOL_WINDOWS):
        sl = slice(g * POOL_GROUP_DIM, (g + 1) * POOL_GROUP_DIM)
        cnt = jnp.minimum(pos + 1, w).astype(F32)
        cols.append(_bdot(sums[g] / cnt - u[:, sl], wp_ref[g]))
    x1 = xin + (g1 * ps_ref[...]) * jnp.concatenate(cols, axis=1)
    x1_ref[...] = x1
    hout_ref[...] = u[tm - POOL_BUF:, :]
    _moe_prep(x1, ng[1:2], sh2, sc2, wrt_ref, br_ref, run_ref, pay_ref, info_ref, cnt_ref)


def _mixer_call(x, res, hist16, ada_l, ng_l, wp_l, ps_l, wrt, br, start_pos):
    bsz, t, _ = x.shape
    tm = _tile_rows(t)
    prev_spec = pl.BlockSpec((None, HIST_ROWS, D_MODEL),
                             lambda b, i: (b, jnp.maximum(i * (tm // HIST_ROWS) - 1, 0), 0))
    ins = [x, x]
    specs = [_tok_spec(tm), prev_spec]
    if res is not None:
        y, g2p = res
        ins += [y, y, g2p]
        specs += [_tok_spec(tm), prev_spec, _per_batch_spec(1)]
    ins += [hist16, ada_l, ng_l, wp_l, ps_l, wrt, br]
    specs += [_per_batch_spec(HIST_ROWS), _per_batch_spec(6), _const_spec((2, D_MODEL)),
              _const_spec((N_POOL_GROUPS, POOL_GROUP_DIM, POOL_GROUP_DIM)), _const_spec((1, D_MODEL)),
              _const_spec((N_EXPERTS, D_MODEL)), _const_spec((N_EXPERTS, 1))]
    p_shapes, p_specs = _prep_out(bsz, t, tm)
    return pl.pallas_call(
        functools.partial(_mixer_body, res is not None, tm, start_pos),
        grid=(bsz, t // tm),
        in_specs=specs,
        out_specs=[_tok_spec(tm)] + p_specs + [_per_batch_spec(POOL_BUF)],
        out_shape=[jax.ShapeDtypeStruct((bsz, t, D_MODEL), F32)] + p_shapes
                  + [jax.ShapeDtypeStruct((bsz, POOL_BUF, D_MODEL), F32)],
        scratch_shapes=[pltpu.VMEM((CLS_ROWS, LANES), F32)],
        compiler_params=_cparams(("arbitrary", "arbitrary")),
        name="pool_mixer",
    )(*ins)


def _proj_body(with_kv, tm, *refs):
    it = iter(refs)
    x1p_ref, y_ref, g2p_ref, ada_ref, ng_ref, wq_ref, qn_ref, s_ref, st_ref = (next(it) for _ in range(9))
    if with_kv:
        kva_ref, kvn_ref, wkv_ref, wf_ref, bf_ref, kn_ref = (next(it) for _ in range(6))
    x_ref, q_ref = next(it), next(it)
    if with_kv:
        k_ref, v_ref, kb_ref, vt_ref, lf_ref, lfw_ref = (next(it) for _ in range(6))

    x = x1p_ref[...] + g2p_ref[...] * y_ref[...]
    x_ref[...] = x
    ada = ada_ref[...]
    ng = ng_ref[...]
    h = _rms_mod(x, ng[0:1], ada[0:1], ada[1:2])
    q = _head_rms(_bdot(h, wq_ref[...]), s_ref, st_ref, qn_ref[...])
    for i, blk in enumerate(_query_blocks(q * (ATTN_SCALE * LOG2E))):
        q_ref[:, i * LANES:(i + 1) * LANES] = blk.astype(BF16)
    if with_kv:
        kva = kva_ref[...]
        hk = _rms_mod(x, kvn_ref[...], kva[0:1], kva[1:2])
        proj = _bdot(hk, wkv_ref[...])
        k = _head_rms(proj[:, :KV_WIDTH], s_ref, st_ref, kn_ref[...])
        v = proj[:, KV_WIDTH:]
        k_ref[...] = k
        v_ref[...] = v
        kb_ref[...] = k.astype(BF16)
        vt_ref[...] = _pad_rows(v, max(tm, LANES)).T[:, :tm].astype(BF16)
        z = _dot3(hk, wf_ref[...]) + bf_ref[...]
        lf = jnp.minimum(z, 0.0) - jnp.log(1.0 + jnp.exp(-jnp.abs(z)))
        lf_ref[...] = lf[:, :N_HEADS]
        lane = lax.broadcasted_iota(jnp.int32, (1, LANES), 1)
        lfw_ref[...] = jnp.where(lane < N_HEADS, lf, 0.0)


def _proj_call(x1p, y, g2p, ada_l, ng_l, wq, qn, s_mat, st_mat, kv=None):
    bsz, t, _ = x1p.shape
    tm = _tile_rows(t, 256)
    ins = [x1p, y, g2p, ada_l, ng_l, wq, qn, s_mat, st_mat]
    specs = [_tok_spec(tm), _tok_spec(tm), _per_batch_spec(1), _per_batch_spec(6), _const_spec((2, D_MODEL)),
             _const_spec((D_MODEL, KV_WIDTH)), _const_spec((1, D_MODEL)),
             _const_spec((D_MODEL, LANES)), _const_spec((LANES, D_MODEL))]
    out_shapes = [jax.ShapeDtypeStruct((bsz, t, D_MODEL), F32), jax.ShapeDtypeStruct((bsz, t, PAD_W), BF16)]
    out_specs = [_tok_spec(tm), _tok_spec(tm, PAD_W)]
    if kv is not None:
        kva, kvn, wkv, wf, bf, kn = kv
        ins += [kva, kvn, wkv, wf, bf, kn]
        specs += [_per_batch_spec(2), _const_spec((1, D_MODEL)), _const_spec((D_MODEL, 2 * KV_WIDTH)),
                  _const_spec((D_MODEL, LANES)), _const_spec((1, LANES)), _const_spec((1, D_MODEL))]
        out_shapes += [jax.ShapeDtypeStruct((bsz, t, D_MODEL), F32)] * 2
        out_shapes += [jax.ShapeDtypeStruct((bsz, t, D_MODEL), BF16), jax.ShapeDtypeStruct((bsz, D_MODEL, t), BF16)]
        out_shapes += [jax.ShapeDtypeStruct((bsz, t, N_HEADS), F32), jax.ShapeDtypeStruct((bsz, t, LANES), F32)]
        out_specs += [_tok_spec(tm)] * 2
        out_specs += [_tok_spec(tm), pl.BlockSpec((None, D_MODEL, tm), lambda b, i: (b, 0, i))]
        out_specs += [_tok_spec(tm, N_HEADS), _tok_spec(tm, LANES)]
    return pl.pallas_call(
        functools.partial(_proj_body, kv is not None, tm),
        grid=(bsz, t // tm),
        in_specs=specs, out_specs=out_specs, out_shape=out_shapes,
        compiler_params=_cparams(("arbitrary", "arbitrary")),
        name="qkv_proj" if kv is not None else "q_proj",
    )(*ins)


def _decay_body(tc, lf_ref, o_ref, carry_ref):
    @pl.when(pl.program_id(1) == 0)
    def _():
        carry_ref[...] = jnp.zeros_like(carry_ref)

    lf = lf_ref[...]
    r = lax.broadcasted_iota(jnp.int32, (tc, tc), 0)
    c = lax.broadcasted_iota(jnp.int32, (tc, tc), 1)
    lower = jnp.where(r >= c, 1.0, 0.0).astype(BF16)
    hi, lo = _split(lf)
    f = (jnp.dot(lower, hi, preferred_element_type=F32) + jnp.dot(lower, lo, preferred_element_type=F32)
         + carry_ref[0:1, :])
    carry_ref[...] = jnp.broadcast_to(f[tc - 1:tc, :], carry_ref.shape)
    nb = f * (-LOG2E)
    p1 = nb.astype(BF16)
    r1 = nb - p1.astype(F32)
    p2 = r1.astype(BF16)
    p3 = (r1 - p2.astype(F32)).astype(BF16)
    hr = lax.broadcasted_iota(jnp.int32, (LANES, LANES), 0)
    lc = lax.broadcasted_iota(jnp.int32, (LANES, LANES), 1)
    out = None
    for i, p in enumerate((p1, p2, p3)):
        place = jnp.where((lc == DECAY_PIECES * hr + i) & (hr < N_HEADS), 1.0, 0.0).astype(BF16)
        term = jnp.dot(p, place, preferred_element_type=F32)
        out = term if out is None else out + term
    o_ref[...] = out.astype(BF16)


def _decay_call(lfw):
    bsz, tk, _ = lfw.shape
    tc = 512
    spec = pl.BlockSpec((None, tc, LANES), lambda b, t: (b, t, 0))
    return pl.pallas_call(
        functools.partial(_decay_body, tc),
        grid=(bsz, tk // tc), in_specs=[spec], out_specs=spec,
        out_shape=jax.ShapeDtypeStruct(lfw.shape, BF16),
        scratch_shapes=[pltpu.VMEM((8, LANES), F32)],
        compiler_params=_cparams(("arbitrary", "arbitrary")),
        name="decay_bias",
    )(lfw)


def _transpose_body(v_ref, o_ref):
    o_ref[...] = v_ref[...].T.astype(BF16)


def _transpose_call(v):
    bsz, p, _ = v.shape
    tp = 512
    return pl.pallas_call(
        _transpose_body, grid=(bsz, p // tp),
        in_specs=[pl.BlockSpec((None, tp, D_MODEL), lambda b, t: (b, t, 0))],
        out_specs=pl.BlockSpec((None, D_MODEL, tp), lambda b, t: (b, 0, t)),
        out_shape=jax.ShapeDtypeStruct((bsz, D_MODEL, p), BF16),
        compiler_params=_cparams(("arbitrary", "arbitrary")),
        name="value_transpose",
    )(v)


def _flash_body(tq, tkp, tks, n_past, n_self, *refs):
    it = iter(refs)
    q_ref = next(it)
    if n_past:
        kp_ref, vtp_ref, fpp_ref = next(it), next(it), next(it)
    ks_ref, vts_ref, fps_ref = next(it), next(it), next(it)
    o_ref, m_ref, l_ref, acc_ref = next(it), next(it), next(it), next(it)
    qi = pl.program_id(1)
    step = pl.program_id(2)

    @pl.when(step == 0)
    def _():
        m_ref[...] = jnp.full_like(m_ref, NEG_INF)
        l_ref[...] = jnp.zeros_like(l_ref)
        acc_ref[...] = jnp.zeros_like(acc_ref)

    def process(k_ref, vt_ref, fp_ref, tk, key_base):
        fp = fp_ref[...]
        lane = lax.broadcasted_iota(jnp.int32, (1, LANES), 1)
        if key_base is not None:
            kpos = key_base + lax.broadcasted_iota(jnp.int32, (tk, 1), 0)
            qpos = qi * tq + lax.broadcasted_iota(jnp.int32, (1, tq), 1)
            visible = kpos <= qpos

        pair_keys = {}

        def logits(h):
            d = lane - DECAY_PIECES * h
            ones = jnp.where((d >= 0) & (d < DECAY_PIECES), 1.0, 0.0).astype(BF16)
            hp = h // 2
            if hp not in pair_keys:
                pair_keys[hp] = k_ref[:, hp * LANES:(hp + 1) * LANES].astype(BF16)
            lhs = jnp.concatenate([pair_keys[hp], fp], axis=1)
            rhs = jnp.concatenate([q_ref[:, h * LANES:(h + 1) * LANES],
                                   jnp.broadcast_to(ones, (tq, LANES))], axis=1)
            s = lax.dot_general(lhs, rhs, _NT, preferred_element_type=F32)
            return s if key_base is None else jnp.where(visible, s, NEG_INF)

        ones_rows = jnp.ones((SUM_ROWS, tk), BF16)
        s_next = logits(0)
        for h in range(N_HEADS):
            s = s_next
            if h + 1 < N_HEADS:
                s_next = logits(h + 1)
            rows = slice(h * HEAD_DIM, (h + 1) * HEAD_DIM)
            m_old = m_ref[h]
            m_new = jnp.maximum(m_old, jnp.max(s, axis=0, keepdims=True))
            alpha = jnp.exp2(m_old - m_new)
            p = jnp.exp2(s - m_new).astype(BF16)
            m_ref[h] = m_new
            pv = jnp.dot(jnp.concatenate([vt_ref[rows, :], ones_rows], axis=0), p,
                         preferred_element_type=F32)
            l_ref[h] = alpha * l_ref[h] + pv[HEAD_DIM:HEAD_DIM + 1, :]
            acc_ref[rows, :] = alpha * acc_ref[rows, :] + pv[:HEAD_DIM, :]

    if n_past:
        @pl.when(step < n_past)
        def _():
            process(kp_ref, vtp_ref, fpp_ref, tkp, None)

    j = step - n_past
    last = (qi * tq + tq - 1) // tks
    has_hidden = (j + 1) * tks - 1 > qi * tq

    @pl.when((j >= 0) & (j <= last) & has_hidden)
    def _():
        process(ks_ref, vts_ref, fps_ref, tks, j * tks)

    @pl.when((j >= 0) & (j <= last) & jnp.logical_not(has_hidden))
    def _():
        process(ks_ref, vts_ref, fps_ref, tks, None)

    @pl.when(step == n_past + n_self - 1)
    def _():
        r = lax.broadcasted_iota(jnp.int32, (tq, tq), 0)
        c = lax.broadcasted_iota(jnp.int32, (tq, tq), 1)
        eye = jnp.where(r == c, 1.0, 0.0).astype(BF16)
        row = lax.broadcasted_iota(jnp.int32, (LANES, 1), 0)
        for hp in range(N_HEADS // 2):
            denom = jnp.where(row < HEAD_DIM, l_ref[2 * hp], l_ref[2 * hp + 1])
            o_t = (acc_ref[hp * LANES:(hp + 1) * LANES, :] / denom).astype(BF16)
            o_ref[:, hp * LANES:(hp + 1) * LANES] = lax.dot_general(
                eye, o_t, _NT, preferred_element_type=F32).astype(BF16)


def _flash_call(q, k_self, vt_self, past, fp, n_past_keys):
    bsz, t_q, _ = q.shape
    tq = 512 if t_q % 512 == 0 else t_q
    tks = 256 if t_q % 256 == 0 else t_q
    n_self = t_q // tks
    tkp = 512
    n_past = n_past_keys // tkp
    fp_self_base = n_past_keys // tks

    def self_idx(i, s):
        return jnp.clip(s - n_past, 0, (i * tq + tq - 1) // tks)

    ins = [q]
    specs = [pl.BlockSpec((None, tq, PAD_W), lambda b, i, s: (b, i, 0))]
    if n_past:
        past_idx = lambda s: jnp.minimum(s, n_past - 1)
        ins += [past[0], past[1], fp]
        specs += [pl.BlockSpec((None, tkp, D_MODEL), lambda b, i, s: (b, past_idx(s), 0)),
                  pl.BlockSpec((None, D_MODEL, tkp), lambda b, i, s: (b, 0, past_idx(s))),
                  pl.BlockSpec((None, tkp, LANES), lambda b, i, s: (b, past_idx(s), 0))]
    ins += [k_self, vt_self, fp]
    specs += [pl.BlockSpec((None, tks, D_MODEL), lambda b, i, s: (b, self_idx(i, s), 0)),
              pl.BlockSpec((None, D_MODEL, tks), lambda b, i, s: (b, 0, self_idx(i, s))),
              pl.BlockSpec((None, tks, LANES), lambda b, i, s: (b, fp_self_base + self_idx(i, s), 0))]
    return pl.pallas_call(
        functools.partial(_flash_body, tq, tkp, tks, n_past, n_self),
        grid=(bsz, t_q // tq, n_past + n_self),
        in_specs=specs,
        out_specs=pl.BlockSpec((None, tq, D_MODEL), lambda b, i, s: (b, i, 0)),
        out_shape=jax.ShapeDtypeStruct((bsz, t_q, D_MODEL), BF16),
        scratch_shapes=[pltpu.VMEM((N_HEADS, 1, tq), F32), pltpu.VMEM((N_HEADS, 1, tq), F32),
                        pltpu.VMEM((D_MODEL, tq), F32)],
        compiler_params=_cparams(("arbitrary", "arbitrary", "arbitrary")),
        name="fox_attention",
    )(*ins)


def _oproj_body(x_ref, o_ref, wo_ref, ada_ref, ng_ref, wrt_ref, br_ref,
                x1_ref, pay_ref, info_ref, cnt_ref, run_ref):
    @pl.when((pl.program_id(0) == 0) & (pl.program_id(1) == 0))
    def _():
        run_ref[...] = jnp.zeros_like(run_ref)

    ada = ada_ref[...]
    x1 = x_ref[...] + ada[2:3] * jnp.dot(o_ref[...], wo_ref[...], preferred_element_type=F32)
    x1_ref[...] = x1
    _moe_prep(x1, ng_ref[...][1:2], ada[3:4], ada[4:5], wrt_ref, br_ref, run_ref, pay_ref, info_ref, cnt_ref)


def _oproj_call(x, o, wo, ada_l, ng_l, wrt, br):
    bsz, t, _ = x.shape
    tm = _tile_rows(t)
    p_shapes, p_specs = _prep_out(bsz, t, tm)
    return pl.pallas_call(
        _oproj_body,
        grid=(bsz, t // tm),
        in_specs=[_tok_spec(tm), _tok_spec(tm), _const_spec((KV_WIDTH, D_MODEL)), _per_batch_spec(6),
                  _const_spec((2, D_MODEL)), _const_spec((N_EXPERTS, D_MODEL)), _const_spec((N_EXPERTS, 1))],
        out_specs=[_tok_spec(tm)] + p_specs,
        out_shape=[jax.ShapeDtypeStruct((bsz, t, D_MODEL), F32)] + p_shapes,
        scratch_shapes=[pltpu.VMEM((CLS_ROWS, LANES), F32)],
        compiler_params=_cparams(("arbitrary", "arbitrary")),
        name="attn_out_proj",
    )(x, o, wo, ada_l, ng_l, wrt, br)


def _final_body(x_ref, y_ref, g_ref, o_ref):
    o_ref[...] = x_ref[...] + g_ref[...] * y_ref[...]


def _final_call(x1, y, g2):
    bsz, t, _ = x1.shape
    tm = _tile_rows(t, 1024)
    return pl.pallas_call(
        _final_body, grid=(bsz, t // tm),
        in_specs=[_tok_spec(tm), _tok_spec(tm), _per_batch_spec(1)],
        out_specs=_tok_spec(tm),
        out_shape=jax.ShapeDtypeStruct(x1.shape, F32),
        compiler_params=_cparams(("arbitrary", "arbitrary")),
        name="final_residual",
    )(x1, y, g2)


def _sc_worker_loop(n_win, fn):
    wid = lax.axis_index("s") * SC_CORES + lax.axis_index("c")
    n_workers = SC_CORES * SC_SUBCORES

    @pl.loop(0, pl.cdiv(n_win, n_workers))
    def _(j):
        win = j * n_workers + wid

        @pl.when(win < n_win)
        def _():
            fn(pl.multiple_of(win * SC_WINDOW, SC_WINDOW))


def _sc_scatter_rows(rows, idx, n_out):
    n, width = rows.shape
    mesh = plsc.VectorSubcoreMesh(core_axis_name="c", subcore_axis_name="s")

    @functools.partial(
        pl.kernel, mesh=mesh, out_type=jax.ShapeDtypeStruct((n_out, width), rows.dtype),
        scratch_types=[pltpu.VMEM((SC_WINDOW,), jnp.int32), pltpu.VMEM((SC_WINDOW, width), rows.dtype)])
    def k(rows_hbm, idx_hbm, out_hbm, idx_v, rows_v):
        def one(base):
            pltpu.sync_copy(idx_hbm.at[pl.ds(base, SC_WINDOW)], idx_v)
            pltpu.sync_copy(rows_hbm.at[pl.ds(base, SC_WINDOW)], rows_v)
            pltpu.sync_copy(rows_v, out_hbm.at[idx_v])
        _sc_worker_loop(n // SC_WINDOW, one)

    return k(rows, idx)


def _sc_gather_rows(table, idx):
    n = idx.shape[0]
    width = table.shape[1]
    mesh = plsc.VectorSubcoreMesh(core_axis_name="c", subcore_axis_name="s")

    @functools.partial(
        pl.kernel, mesh=mesh, out_type=jax.ShapeDtypeStruct((n, width), table.dtype),
        scratch_types=[pltpu.VMEM((SC_WINDOW,), jnp.int32), pltpu.VMEM((SC_WINDOW, width), table.dtype)])
    def k(table_hbm, idx_hbm, out_hbm, idx_v, rows_v):
        def one(base):
            pltpu.sync_copy(idx_hbm.at[pl.ds(base, SC_WINDOW)], idx_v)
            pltpu.sync_copy(table_hbm.at[idx_v], rows_v)
            pltpu.sync_copy(rows_v, out_hbm.at[pl.ds(base, SC_WINDOW)])
        _sc_worker_loop(n // SC_WINDOW, one)

    return k(table, idx)


def _moe_body(e1_ref, e2_ref, na_ref, hs_ref, g1_ref, g2_ref, u1_ref, u2_ref, d1_ref, d2_ref, o_ref):
    @pl.when(pl.program_id(0) < na_ref[0])
    def _():
        blk = hs_ref[...]
        h = _unpack_bf16_pairs(blk[:, :PACK_W])
        y = None
        for lane, (g_ref, u_ref, d_ref) in enumerate(((g1_ref, u1_ref, d1_ref), (g2_ref, u2_ref, d2_ref))):
            gate = jnp.dot(h, g_ref[...], preferred_element_type=F32)
            up = jnp.dot(h, u_ref[...], preferred_element_type=F32)
            w = blk[:, PACK_W + lane:PACK_W + lane + 1]
            act = (gate * jax.nn.sigmoid(gate) * up * w).astype(BF16)
            term = jnp.dot(act, d_ref[...], preferred_element_type=F32)
            y = term if y is None else y + term
        o_ref[...] = y


def _moe_call(hs, tile_e1, tile_e2, n_active, wg, wu, wd, layer, tm):
    n_s = hs.shape[0]
    n_tiles = n_s // tm
    base = layer * N_EXPERTS

    def row_map(i, e1, e2, na):
        return (jnp.minimum(i, na[0] - 1), 0)

    def w_map(which):
        def m(i, e1, e2, na):
            e = (e1, e2)[which]
            return (base + e[jnp.minimum(i, na[0] - 1)], 0, 0)
        return m

    gu = lambda which: pl.BlockSpec((None, D_MODEL, D_EXPERT), w_map(which))
    dn = lambda which: pl.BlockSpec((None, D_EXPERT, D_MODEL), w_map(which))
    return pl.pallas_call(
        _moe_body,
        grid_spec=pltpu.PrefetchScalarGridSpec(
            num_scalar_prefetch=3, grid=(n_tiles,),
            in_specs=[pl.BlockSpec((tm, PAY_W), row_map), gu(0), gu(1), gu(0), gu(1), dn(0), dn(1)],
            out_specs=pl.BlockSpec((tm, D_MODEL), row_map)),
        out_shape=jax.ShapeDtypeStruct((n_s, D_MODEL), F32),
        compiler_params=_cparams(("arbitrary",)),
        name="grouped_experts",
    )(tile_e1, tile_e2, n_active, hs, wg, wg, wu, wu, wd, wd)


_PAIR_LO = (0, 0, 0, 1, 1, 2)
_PAIR_HI = (1, 2, 3, 2, 3, 3)


def _moe_layer(pay, info, counts, wg, wu, wd, layer):
    bsz, t, _ = pay.shape
    n = bsz * t
    tm = 256
    n_s = ((n + N_CLASSES * (tm - 1)) // tm + 1) * tm
    n_tiles = n_s // tm
    cls = info[:, 0, :].reshape(n)
    rank = info[:, 1, :].reshape(n)
    cnt = counts[:N_CLASSES, 0].astype(jnp.int32)
    padded = ((cnt + tm - 1) // tm) * tm
    ends = jnp.cumsum(padded)
    starts = ends - padded
    dest = starts[cls] + rank
    tile_start = jnp.arange(n_tiles, dtype=jnp.int32) * tm
    tile_cls = jnp.minimum(jnp.sum((tile_start[:, None] >= ends[None, :]).astype(jnp.int32), axis=1),
                           N_CLASSES - 1)
    grp = tile_cls // N_PAIRS
    pr = tile_cls % N_PAIRS
    tile_e1 = grp * EXPERTS_PER_GROUP + jnp.asarray(_PAIR_LO, jnp.int32)[pr]
    tile_e2 = grp * EXPERTS_PER_GROUP + jnp.asarray(_PAIR_HI, jnp.int32)[pr]
    n_active = (ends[-1:] // tm).astype(jnp.int32)
    hs = _sc_scatter_rows(pay.reshape(n, PAY_W), dest, n_s)
    ys = _moe_call(hs, tile_e1, tile_e2, n_active, wg, wu, wd, layer, tm)
    return _sc_gather_rows(ys, dest).reshape(bsz, t, D_MODEL)


def _trunk(x, ada, kva, hist, past, prm):
    bsz, t, _ = x.shape
    start_pos = 0 if past is None else past[0].shape[1]
    wrt, br = prm["wrt"], prm["br"]
    res = None
    new_hist = []
    for layer in range(N_A_LAYERS):
        if hist is None:
            h16 = jnp.zeros((bsz, HIST_ROWS, D_MODEL), F32)
        else:
            h16 = jnp.pad(hist[layer], ((0, 0), (1, 0), (0, 0)))
        x, pay, info, counts, hout = _mixer_call(
            x, res, h16, ada[layer], prm["norm_g"][layer], prm["w_pool"][layer], prm["pool_scale"][layer],
            wrt, br, start_pos)
        new_hist.append(hout)
        y = _moe_layer(pay, info, counts, prm["w_gate"], prm["w_up"], prm["w_down"], layer)
        res = (y, ada[layer][:, 5:6, :])

    shared = None
    for j in range(N_B_LAYERS):
        layer = N_A_LAYERS + j
        y, g2p = res
        if j == 0:
            kv = (kva, prm["kv_norm"], prm["w_kv"], prm["w_f"], prm["b_f"], prm["k_norm"])
            xr, q, k, v, kb, vt, logf, lfw = _proj_call(
                x, y, g2p, ada[layer], prm["norm_g"][layer], prm["w_q"][j], prm["q_norm"][j],
                prm["s_mat"], prm["st_mat"], kv)
            if past is None:
                past_kv = None
                fp = _decay_call(lfw)
            else:
                ck, cv, clogf = past
                n_past = ck.shape[1]
                pad = -(n_past + t) % 512
                lfw = jnp.concatenate([jnp.pad(clogf, ((0, 0), (0, 0), (0, LANES - N_HEADS))), lfw,
                                       jnp.zeros((bsz, pad, LANES), F32)], axis=1)
                fp = _decay_call(lfw)
                past_kv = (ck.reshape(bsz, n_past, D_MODEL), _transpose_call(cv.reshape(bsz, n_past, D_MODEL)))
            shared = (k, v, logf, kb, vt, past_kv, fp)
        else:
            xr, q = _proj_call(x, y, g2p, ada[layer], prm["norm_g"][layer], prm["w_q"][j], prm["q_norm"][j],
                               prm["s_mat"], prm["st_mat"])
        o = _flash_call(q, shared[3], shared[4], shared[5], shared[6], start_pos)
        x, pay, info, counts = _oproj_call(xr, o, prm["w_o"][j], ada[layer], prm["norm_g"][layer], wrt, br)
        y = _moe_layer(pay, info, counts, prm["w_gate"], prm["w_up"], prm["w_down"], layer)
        res = (y, ada[layer][:, 5:6, :])

    out = _final_call(x, res[0], res[1])
    k, v, logf = shared[:3]
    return (out, jnp.stack(new_hist), k.reshape(bsz, t, N_HEADS, HEAD_DIM),
            v.reshape(bsz, t, N_HEADS, HEAD_DIM), logf)


def kernel(x_prompt, x_sample, cache_pool, cache_k, cache_v, cache_logf, c_prompt, c_sample, ada_w, ada_b, norm_g, w_pool, pool_scale, kv_ada_w, kv_ada_b, kv_norm, w_kvf, b_f, k_norm, w_q, q_norm, w_o, w_router, b_router, w_gate, w_up, w_down):
    bp = x_prompt.shape[0]
    c_all = jnp.concatenate([c_prompt, c_sample], axis=0)
    bc = c_all.shape[0]
    ada = _ada_call(c_all, ada_w, ada_b).reshape(DEPTH, bc, 6, D_MODEL)
    kva = _ada_call(c_all, kv_ada_w[None], kv_ada_b[None]).reshape(bc, 2, D_MODEL)

    head_of_lane = jnp.arange(D_MODEL, dtype=jnp.int32) // HEAD_DIM
    s_mat = (head_of_lane[:, None] == jnp.arange(LANES, dtype=jnp.int32)[None, :]).astype(BF16)
    prm = {
        "norm_g": norm_g,
        "w_pool": w_pool.astype(BF16),
        "pool_scale": pool_scale.reshape(N_A_LAYERS, 1, D_MODEL),
        "kv_norm": kv_norm.reshape(1, D_MODEL),
        "w_kv": w_kvf[:, :2 * KV_WIDTH].astype(BF16),
        "w_f": jnp.pad(w_kvf[:, 2 * KV_WIDTH:], ((0, 0), (0, LANES - N_HEADS))),
        "b_f": jnp.pad(b_f, (0, LANES - N_HEADS)).reshape(1, LANES),
        "k_norm": jnp.tile(k_norm, N_HEADS).reshape(1, D_MODEL),
        "w_q": w_q.astype(BF16),
        "q_norm": jnp.tile(q_norm, (1, N_HEADS)).reshape(N_B_LAYERS, 1, D_MODEL),
        "w_o": w_o.astype(BF16),
        "wrt": w_router.T,
        "br": b_router.reshape(N_EXPERTS, 1),
        "w_gate": w_gate.astype(BF16).reshape(DEPTH * N_EXPERTS, D_MODEL, D_EXPERT),
        "w_up": w_up.astype(BF16).reshape(DEPTH * N_EXPERTS, D_MODEL, D_EXPERT),
        "w_down": w_down.astype(BF16).reshape(DEPTH * N_EXPERTS, D_EXPERT, D_MODEL),
        "s_mat": s_mat,
        "st_mat": s_mat.T,
    }
    outs_p = _trunk(x_prompt, ada[:, :bp], kva[:bp], None, None, prm)
    outs_s = _trunk(x_sample, ada[:, bp:], kva[bp:], cache_pool, (cache_k, cache_v, cache_logf), prm)
    return (outs_p[0], outs_s[0]) + outs_p[1:] + outs_s[1:]
```

```python
import functools

import jax
import jax.numpy as jnp
from jax import lax
from jax.experimental import pallas as pl
from jax.experimental.pallas import tpu as pltpu
from jax.experimental.pallas import tpu_sc as plsc

F32 = jnp.float32
BF16 = jnp.bfloat16

D_MODEL = 1024
DEPTH = 4
N_A_LAYERS = DEPTH // 2
N_B_LAYERS = DEPTH - N_A_LAYERS
POOL_WINDOWS = (2, 4, 8, 16)
N_POOL_GROUPS = len(POOL_WINDOWS)
POOL_GROUP_DIM = D_MODEL // N_POOL_GROUPS
POOL_BUF = max(POOL_WINDOWS) - 1
HIST_ROWS = POOL_BUF + 1
N_HEADS = 16
HEAD_DIM = D_MODEL // N_HEADS
KV_WIDTH = N_HEADS * HEAD_DIM
ATTN_SCALE = HEAD_DIM ** -0.5
N_EXPERTS = 16
N_EXPERT_GROUPS = 4
EXPERTS_PER_GROUP = N_EXPERTS // N_EXPERT_GROUPS
N_PAIRS = 6
N_CLASSES = N_EXPERT_GROUPS * N_PAIRS
D_EXPERT = D_MODEL // 2
EPS = 1e-6
NEG_INF = -1e30

LANES = 128
MXU_COLS = 256
SC_CORES = 2
SC_SUBCORES = 16
SC_WINDOW = 64
PACK_W = D_MODEL // 2
PAY_W = PACK_W + LANES
PAD_W = N_HEADS * LANES
LOG2E = 1.4426950408889634
DECAY_PIECES = 3
SUM_ROWS = 16
CLS_ROWS = 32
VMEM_LIMIT = 48 * 1024 * 1024


def _cparams(sem, flags=None):
    return pltpu.CompilerParams(dimension_semantics=sem, vmem_limit_bytes=VMEM_LIMIT, flags=flags)


def _bdot(a, b):
    return jnp.dot(a.astype(BF16), b.astype(BF16), preferred_element_type=F32)


def _split(a):
    hi = a.astype(BF16)
    lo = (a - hi.astype(F32)).astype(BF16)
    return hi, lo


_NN = (((1,), (0,)), ((), ()))
_NT = (((1,), (1,)), ((), ()))


def _dot3(a, b, dims=_NN):
    ah, al = _split(a)
    bh, bl = _split(b)
    d = lambda x, y: lax.dot_general(x, y, dims, preferred_element_type=F32)
    return d(ah, bh) + (d(ah, bl) + d(al, bh))


def _dot2_exact_rhs(a, b_bf16):
    ah, al = _split(a)
    return (jnp.dot(ah, b_bf16, preferred_element_type=F32)
            + jnp.dot(al, b_bf16, preferred_element_type=F32))


def _rms_mod(x, gain, shift, scale):
    ms = jnp.mean(x * x, axis=-1, keepdims=True)
    return (x * lax.rsqrt(ms + EPS)) * (gain * (1.0 + scale)) + shift


def _head_rms(z, s_ref, st_ref, gain):
    ss = _dot2_exact_rhs(z * z, s_ref[...])
    inv = lax.rsqrt(ss * (1.0 / HEAD_DIM) + EPS)
    invf = _dot2_exact_rhs(inv, st_ref[...])
    return z * invf * gain


def _query_blocks(q):
    lane = lax.broadcasted_iota(jnp.int32, (1, LANES), 1)
    blocks = []
    for h in range(N_HEADS):
        pair = q[:, (h // 2) * LANES:(h // 2 + 1) * LANES]
        in_head = (lane >= HEAD_DIM) if h % 2 else (lane < HEAD_DIM)
        blocks.append(jnp.where(in_head, pair, 0.0))
    return blocks


_HI_MASK = 0xFFFF0000


def _pack_bf16_pairs(x):
    bits = lambda a: lax.bitcast_convert_type(a.astype(BF16).astype(F32), jnp.uint32)
    half = x.shape[1] // 2
    word = (bits(x[:, :half]) >> 16) | (bits(x[:, half:]) & jnp.uint32(_HI_MASK))
    return lax.bitcast_convert_type(word, F32)


def _unpack_bf16_pairs(w):
    word = lax.bitcast_convert_type(w, jnp.uint32)
    lo = lax.bitcast_convert_type(word << 16, F32).astype(BF16)
    hi = lax.bitcast_convert_type(word & jnp.uint32(_HI_MASK), F32).astype(BF16)
    return jnp.concatenate([lo, hi], axis=1)


def _pad_rows(a, rows):
    if a.shape[0] == rows:
        return a
    return jnp.concatenate([a, jnp.zeros((rows - a.shape[0], a.shape[1]), a.dtype)], axis=0)


def _route(lt, br):
    m = jnp.max(lt, axis=0, keepdims=True)
    p = jnp.exp(lt - m)
    scores = p / jnp.sum(p, axis=0, keepdims=True)
    sel = scores + br
    row = lambda a, e: a[e:e + 1, :]
    gs = []
    for g in range(N_EXPERT_GROUPS):
        v = [row(sel, g * EXPERTS_PER_GROUP + j) for j in range(EXPERTS_PER_GROUP)]
        best = None
        for i in range(EXPERTS_PER_GROUP):
            for j in range(i + 1, EXPERTS_PER_GROUP):
                s = v[i] + v[j]
                best = s if best is None else jnp.maximum(best, s)
        gs.append(best)
    bg = jnp.zeros_like(gs[0])
    bv = gs[0]
    for g in range(1, N_EXPERT_GROUPS):
        better = gs[g] > bv
        bg = jnp.where(better, float(g), bg)
        bv = jnp.where(better, gs[g], bv)

    def in_group(a, j):
        out = row(a, j)
        for g in range(1, N_EXPERT_GROUPS):
            out = jnp.where(bg == float(g), row(a, g * EXPERTS_PER_GROUP + j), out)
        return out

    sg = [in_group(sel, j) for j in range(EXPERTS_PER_GROUP)]
    cg = [in_group(scores, j) for j in range(EXPERTS_PER_GROUP)]

    def first_argmax(vals):
        mx = vals[0]
        for v in vals[1:]:
            mx = jnp.maximum(mx, v)
        idx = jnp.full_like(mx, float(len(vals) - 1))
        for j in range(len(vals) - 2, -1, -1):
            idx = jnp.where(vals[j] == mx, float(j), idx)
        return idx

    i1 = first_argmax(sg)
    i2 = first_argmax([jnp.where(i1 == float(j), -jnp.inf, sg[j]) for j in range(EXPERTS_PER_GROUP)])
    lo = jnp.minimum(i1, i2)
    hi = jnp.maximum(i1, i2)

    def pick(vals, idx):
        out = vals[0]
        for j in range(1, len(vals)):
            out = jnp.where(idx == float(j), vals[j], out)
        return out

    c_lo = pick(cg, lo)
    c_hi = pick(cg, hi)
    tot = c_lo + c_hi
    pair = jnp.where(lo == 0.0, hi - 1.0, jnp.where(lo == 1.0, hi + 1.0, 5.0))
    return bg * float(N_PAIRS) + pair, c_lo / tot, c_hi / tot


def _moe_prep(x1, ng2, sh2, sc2, wrt_ref, br_ref, run_ref, pay_ref, info_ref, cnt_ref):
    tm = x1.shape[0]
    tr = max(tm, LANES)
    h2 = _rms_mod(x1, ng2, sh2, sc2)
    lt = _dot3(wrt_ref[...], _pad_rows(h2, tr), _NT)
    cls, w_lo, w_hi = _route(lt, br_ref[...])

    r = lax.broadcasted_iota(jnp.int32, (LANES, tr), 0)
    wrows = jnp.where(r == 0, w_lo, jnp.where(r == 1, w_hi, 0.0))
    pay_ref[:, :PACK_W] = _pack_bf16_pairs(h2)
    pay_ref[:, PACK_W:] = wrows.T[:tm, :]

    crow = lax.broadcasted_iota(jnp.int32, (CLS_ROWS, tr), 0).astype(F32)
    lane = lax.broadcasted_iota(jnp.int32, (CLS_ROWS, tr), 1)
    onehot = jnp.where((crow == cls) & (lane < tm), 1.0, 0.0)
    us = lax.broadcasted_iota(jnp.int32, (tr, tr), 0)
    ut = lax.broadcasted_iota(jnp.int32, (tr, tr), 1)
    upper = jnp.where(us < ut, 1.0, 0.0).astype(BF16)
    before = jnp.dot(onehot.astype(BF16), upper, preferred_element_type=F32) + run_ref[:, 0:1]
    rank = jnp.sum(onehot * before, axis=0, keepdims=True)
    run_new = run_ref[...] + jnp.sum(onehot, axis=1, keepdims=True)
    run_ref[...] = run_new
    cnt_ref[...] = run_new
    ir = lax.broadcasted_iota(jnp.int32, (8, tr), 0)
    info = jnp.where(ir == 0, cls, jnp.where(ir == 1, rank, 0.0)).astype(jnp.int32)
    info_ref[...] = info[:, :tm]


def _ada_body(c_ref, w_ref, b_ref, o_ref):
    c = c_ref[...]
    o_ref[...] = _dot3(c * jax.nn.sigmoid(c), w_ref[...]) + b_ref[...]


def _ada_call(c_all, w, b):
    n_l, _, n_out = w.shape
    bc = c_all.shape[0]
    tn = 1536 if n_out % 1536 == 0 else 1024
    return pl.pallas_call(
        _ada_body,
        grid=(n_l, n_out // tn),
        in_specs=[pl.BlockSpec((bc, D_MODEL), lambda l, j: (0, 0)),
                  pl.BlockSpec((None, D_MODEL, tn), lambda l, j: (l, 0, j)),
                  pl.BlockSpec((None, 1, tn), lambda l, j: (l, 0, j))],
        out_specs=pl.BlockSpec((None, bc, tn), lambda l, j: (l, 0, j)),
        out_shape=jax.ShapeDtypeStruct((n_l, bc, n_out), F32),
        compiler_params=_cparams(("arbitrary", "arbitrary")),
        name="adaln",
    )(c_all, w, b.reshape(n_l, 1, n_out))


def _tile_rows(t, rows=512):
    return rows if t % rows == 0 else t


def _tok_spec(tm, width=D_MODEL):
    return pl.BlockSpec((None, tm, width), lambda b, t: (b, t, 0))


def _per_batch_spec(rows, width=D_MODEL):
    return pl.BlockSpec((None, rows, width), lambda b, t: (b, 0, 0))


def _const_spec(shape):
    nd = len(shape)
    return pl.BlockSpec(shape, lambda b, t: (0,) * nd)


def _prep_out(bsz, t, tm):
    shapes = [jax.ShapeDtypeStruct((bsz, t, PAY_W), F32),
              jax.ShapeDtypeStruct((bsz, 8, t), jnp.int32),
              jax.ShapeDtypeStruct((CLS_ROWS, LANES), F32)]
    specs = [_tok_spec(tm, PAY_W),
             pl.BlockSpec((None, 8, tm), lambda b, t: (b, 0, t)),
             _const_spec((CLS_ROWS, LANES))]
    return shapes, specs


def _mixer_body(has_res, tm, start_pos, *refs):
    it = iter(refs)
    x_ref, xp_ref = next(it), next(it)
    if has_res:
        y_ref, yp_ref, g2p_ref = next(it), next(it), next(it)
    hist_ref, ada_ref, ng_ref, wp_ref, ps_ref, wrt_ref, br_ref = (next(it) for _ in range(7))
    x1_ref, pay_ref, info_ref, cnt_ref, hout_ref = (next(it) for _ in range(5))
    run_ref = next(it)

    b = pl.program_id(0)
    t = pl.program_id(1)

    @pl.when((b == 0) & (t == 0))
    def _():
        run_ref[...] = jnp.zeros_like(run_ref)

    xin = x_ref[...]
    xp = xp_ref[...]
    if has_res:
        g2p = g2p_ref[...]
        xin = xin + g2p * y_ref[...]
        xp = xp + g2p * yp_ref[...]
    ada = ada_ref[...]
    sh1, sc1, g1, sh2, sc2 = (ada[i:i + 1] for i in range(5))
    ng = ng_ref[...]
    u = _rms_mod(xin, ng[0:1], sh1, sc1)
    up = _rms_mod(xp, ng[0:1], sh1, sc1)
    up = jnp.where(t == 0, hist_ref[...], up)
    level = jnp.concatenate([up, u], axis=0)
    sums = []
    for g, w in enumerate(POOL_WINDOWS):
        level = level + pltpu.roll(level, w // 2, 0)
        sums.append(level[HIST_ROWS:, :POOL_GROUP_DIM])
        if g + 1 < N_POOL_GROUPS:
            level = level[:, POOL_GROUP_DIM:]

    pos = start_pos + t * tm + lax.broadcasted_iota(jnp.int32, (tm, 1), 0)
    cols = []
    for g, w in enumerate(POOL_WINDOWS):
        sl = slice(g * POOL_GROUP_DIM, (g + 1) * POOL_GROUP_DIM)
        cnt = jnp.minimum(pos + 1, w).astype(F32)
        cols.append(_bdot(sums[g] / cnt - u[:, sl], wp_ref[g]))
    x1 = xin + (g1 * ps_ref[...]) * jnp.concatenate(cols, axis=1)
    x1_ref[...] = x1
    hout_ref[...] = u[tm - POOL_BUF:, :]
    _moe_prep(x1, ng[1:2], sh2, sc2, wrt_ref, br_ref, run_ref, pay_ref, info_ref, cnt_ref)


def _mixer_call(x, res, hist16, ada_l, ng_l, wp_l, ps_l, wrt, br, start_pos):
    bsz, t, _ = x.shape
    tm = _tile_rows(t)
    prev_spec = pl.BlockSpec((None, HIST_ROWS, D_MODEL),
                             lambda b, i: (b, jnp.maximum(i * (tm // HIST_ROWS) - 1, 0), 0))
    ins = [x, x]
    specs = [_tok_spec(tm), prev_spec]
    if res is not None:
        y, g2p = res
        ins += [y, y, g2p]
        specs += [_tok_spec(tm), prev_spec, _per_batch_spec(1)]
    ins += [hist16, ada_l, ng_l, wp_l, ps_l, wrt, br]
    specs += [_per_batch_spec(HIST_ROWS), _per_batch_spec(6), _const_spec((2, D_MODEL)),
              _const_spec((N_POOL_GROUPS, POOL_GROUP_DIM, POOL_GROUP_DIM)), _const_spec((1, D_MODEL)),
              _const_spec((N_EXPERTS, D_MODEL)), _const_spec((N_EXPERTS, 1))]
    p_shapes, p_specs = _prep_out(bsz, t, tm)
    return pl.pallas_call(
        functools.partial(_mixer_body, res is not None, tm, start_pos),
        grid=(bsz, t // tm),
        in_specs=specs,
        out_specs=[_tok_spec(tm)] + p_specs + [_per_batch_spec(POOL_BUF)],
        out_shape=[jax.ShapeDtypeStruct((bsz, t, D_MODEL), F32)] + p_shapes
                  + [jax.ShapeDtypeStruct((bsz, POOL_BUF, D_MODEL), F32)],
        scratch_shapes=[pltpu.VMEM((CLS_ROWS, LANES), F32)],
        compiler_params=_cparams(("arbitrary", "arbitrary")),
        name="pool_mixer",
    )(*ins)


def _proj_body(with_kv, tm, *refs):
    it = iter(refs)
    x1p_ref, y_ref, g2p_ref, ada_ref, ng_ref, wq_ref, qn_ref, s_ref, st_ref = (next(it) for _ in range(9))
    if with_kv:
        kva_ref, kvn_ref, wkv_ref, wf_ref, bf_ref, kn_ref = (next(it) for _ in range(6))
    x_ref, q_ref = next(it), next(it)
    if with_kv:
        k_ref, v_ref, kb_ref, vt_ref, lf_ref, lfw_ref = (next(it) for _ in range(6))

    x = x1p_ref[...] + g2p_ref[...] * y_ref[...]
    x_ref[...] = x
    ada = ada_ref[...]
    ng = ng_ref[...]
    h = _rms_mod(x, ng[0:1], ada[0:1], ada[1:2])
    q = _head_rms(_bdot(h, wq_ref[...]), s_ref, st_ref, qn_ref[...])
    for i, blk in enumerate(_query_blocks(q * (ATTN_SCALE * LOG2E))):
        q_ref[:, i * LANES:(i + 1) * LANES] = blk.astype(BF16)
    if with_kv:
        kva = kva_ref[...]
        hk = _rms_mod(x, kvn_ref[...], kva[0:1], kva[1:2])
        proj = _bdot(hk, wkv_ref[...])
        k = _head_rms(proj[:, :KV_WIDTH], s_ref, st_ref, kn_ref[...])
        v = proj[:, KV_WIDTH:]
        k_ref[...] = k
        v_ref[...] = v
        kb_ref[...] = k.astype(BF16)
        vt_ref[...] = _pad_rows(v, max(tm, LANES)).T[:, :tm].astype(BF16)
        z = _dot3(hk, wf_ref[...]) + bf_ref[...]
        lf = jnp.minimum(z, 0.0) - jnp.log(1.0 + jnp.exp(-jnp.abs(z)))
        lf_ref[...] = lf[:, :N_HEADS]
        lane = lax.broadcasted_iota(jnp.int32, (1, LANES), 1)
        lfw_ref[...] = jnp.where(lane < N_HEADS, lf, 0.0)


def _proj_call(x1p, y, g2p, ada_l, ng_l, wq, qn, s_mat, st_mat, kv=None):
    bsz, t, _ = x1p.shape
    tm = _tile_rows(t, 256)
    ins = [x1p, y, g2p, ada_l, ng_l, wq, qn, s_mat, st_mat]
    specs = [_tok_spec(tm), _tok_spec(tm), _per_batch_spec(1), _per_batch_spec(6), _const_spec((2, D_MODEL)),
             _const_spec((D_MODEL, KV_WIDTH)), _const_spec((1, D_MODEL)),
             _const_spec((D_MODEL, LANES)), _const_spec((LANES, D_MODEL))]
    out_shapes = [jax.ShapeDtypeStruct((bsz, t, D_MODEL), F32), jax.ShapeDtypeStruct((bsz, t, PAD_W), BF16)]
    out_specs = [_tok_spec(tm), _tok_spec(tm, PAD_W)]
    if kv is not None:
        kva, kvn, wkv, wf, bf, kn = kv
        ins += [kva, kvn, wkv, wf, bf, kn]
        specs += [_per_batch_spec(2), _const_spec((1, D_MODEL)), _const_spec((D_MODEL, 2 * KV_WIDTH)),
                  _const_spec((D_MODEL, LANES)), _const_spec((1, LANES)), _const_spec((1, D_MODEL))]
        out_shapes += [jax.ShapeDtypeStruct((bsz, t, D_MODEL), F32)] * 2
        out_shapes += [jax.ShapeDtypeStruct((bsz, t, D_MODEL), BF16), jax.ShapeDtypeStruct((bsz, D_MODEL, t), BF16)]
        out_shapes += [jax.ShapeDtypeStruct((bsz, t, N_HEADS), F32), jax.ShapeDtypeStruct((bsz, t, LANES), F32)]
        out_specs += [_tok_spec(tm)] * 2
        out_specs += [_tok_spec(tm), pl.BlockSpec((None, D_MODEL, tm), lambda b, i: (b, 0, i))]
        out_specs += [_tok_spec(tm, N_HEADS), _tok_spec(tm, LANES)]
    return pl.pallas_call(
        functools.partial(_proj_body, kv is not None, tm),
        grid=(bsz, t // tm),
        in_specs=specs, out_specs=out_specs, out_shape=out_shapes,
        compiler_params=_cparams(("arbitrary", "arbitrary")),
        name="qkv_proj" if kv is not None else "q_proj",
    )(*ins)


def _decay_body(tc, lf_ref, o_ref, carry_ref):
    @pl.when(pl.program_id(1) == 0)
    def _():
        carry_ref[...] = jnp.zeros_like(carry_ref)

    lf = lf_ref[...]
    r = lax.broadcasted_iota(jnp.int32, (tc, tc), 0)
    c = lax.broadcasted_iota(jnp.int32, (tc, tc), 1)
    lower = jnp.where(r >= c, 1.0, 0.0).astype(BF16)
    hi, lo = _split(lf)
    f = (jnp.dot(lower, hi, preferred_element_type=F32) + jnp.dot(lower, lo, preferred_element_type=F32)
         + carry_ref[0:1, :])
    carry_ref[...] = jnp.broadcast_to(f[tc - 1:tc, :], carry_ref.shape)
    nb = f * (-LOG2E)
    p1 = nb.astype(BF16)
    r1 = nb - p1.astype(F32)
    p2 = r1.astype(BF16)
    p3 = (r1 - p2.astype(F32)).astype(BF16)
    hr = lax.broadcasted_iota(jnp.int32, (LANES, LANES), 0)
    lc = lax.broadcasted_iota(jnp.int32, (LANES, LANES), 1)
    out = None
    for i, p in enumerate((p1, p2, p3)):
        place = jnp.where((lc == DECAY_PIECES * hr + i) & (hr < N_HEADS), 1.0, 0.0).astype(BF16)
        term = jnp.dot(p, place, preferred_element_type=F32)
        out = term if out is None else out + term
    o_ref[...] = out.astype(BF16)


def _decay_call(lfw):
    bsz, tk, _ = lfw.shape
    tc = 512
    spec = pl.BlockSpec((None, tc, LANES), lambda b, t: (b, t, 0))
    return pl.pallas_call(
        functools.partial(_decay_body, tc),
        grid=(bsz, tk // tc), in_specs=[spec], out_specs=spec,
        out_shape=jax.ShapeDtypeStruct(lfw.shape, BF16),
        scratch_shapes=[pltpu.VMEM((8, LANES), F32)],
        compiler_params=_cparams(("arbitrary", "arbitrary")),
        name="decay_bias",
    )(lfw)


def _cache_prep_body(tp, ck_ref, cv_ref, k_ref, vt_ref):
    def pair(ref, hp):
        return jnp.concatenate([ref[pl.ds(2 * hp + i, tp, stride=N_HEADS), :] for i in range(2)], axis=1)

    for hp in range(N_HEADS // 2):
        k_ref[:, hp * LANES:(hp + 1) * LANES] = pair(ck_ref, hp).astype(BF16)
        vt_ref[hp * LANES:(hp + 1) * LANES, :] = pair(cv_ref, hp).T.astype(BF16)


def _cache_prep_call(ck, cv):
    bsz, p, _, _ = ck.shape
    tp = 256
    spec = pl.BlockSpec((None, tp * N_HEADS, HEAD_DIM), lambda b, t: (b, t, 0))
    return pl.pallas_call(
        functools.partial(_cache_prep_body, tp), grid=(bsz, p // tp),
        in_specs=[spec, spec],
        out_specs=[pl.BlockSpec((None, tp, D_MODEL), lambda b, t: (b, t, 0)),
                   pl.BlockSpec((None, D_MODEL, tp), lambda b, t: (b, 0, t))],
        out_shape=[jax.ShapeDtypeStruct((bsz, p, D_MODEL), BF16), jax.ShapeDtypeStruct((bsz, D_MODEL, p), BF16)],
        compiler_params=_cparams(("arbitrary", "arbitrary")),
        name="cache_prep",
    )(ck.reshape(bsz, p * N_HEADS, HEAD_DIM), cv.reshape(bsz, p * N_HEADS, HEAD_DIM))


def _flash_body(tq, tkp, tks, n_past, n_self, *refs):
    it = iter(refs)
    q_ref = next(it)
    if n_past:
        kp_ref, vtp_ref, fpp_ref = next(it), next(it), next(it)
    ks_ref, vts_ref, fps_ref = next(it), next(it), next(it)
    o_ref, m_ref, l_ref, acc_ref = next(it), next(it), next(it), next(it)
    qi = pl.program_id(1)
    step = pl.program_id(2)

    @pl.when(step == 0)
    def _():
        m_ref[...] = jnp.full_like(m_ref, NEG_INF)
        l_ref[...] = jnp.zeros_like(l_ref)
        acc_ref[...] = jnp.zeros_like(acc_ref)

    def process(k_ref, vt_ref, fp_ref, tk, key_base):
        fp = fp_ref[...]
        lane = lax.broadcasted_iota(jnp.int32, (1, LANES), 1)
        if key_base is not None:
            kpos = key_base + lax.broadcasted_iota(jnp.int32, (tk, 1), 0)
            qpos = qi * tq + lax.broadcasted_iota(jnp.int32, (1, tq), 1)
            visible = kpos <= qpos

        pair_keys = {}

        def logits(h):
            d = lane - DECAY_PIECES * h
            ones = jnp.where((d >= 0) & (d < DECAY_PIECES), 1.0, 0.0).astype(BF16)
            hp = h // 2
            if hp not in pair_keys:
                pair_keys[hp] = k_ref[:, hp * LANES:(hp + 1) * LANES].astype(BF16)
            lhs = jnp.concatenate([pair_keys[hp], fp], axis=1)
            rhs = jnp.concatenate([q_ref[:, h * LANES:(h + 1) * LANES],
                                   jnp.broadcast_to(ones, (tq, LANES))], axis=1)
            if tq < 2 * MXU_COLS and tk % 2 == 0:
                half = tk // 2
                s = jnp.concatenate([lax.dot_general(lhs[:half], rhs, _NT, preferred_element_type=F32),
                                     lax.dot_general(lhs[half:], rhs, _NT, preferred_element_type=F32)], axis=0)
            else:
                s = lax.dot_general(lhs, rhs, _NT, preferred_element_type=F32)
            return s if key_base is None else jnp.where(visible, s, NEG_INF)

        ones_rows = jnp.ones((SUM_ROWS, tk), BF16)
        s_next = logits(0)
        for h in range(N_HEADS):
            s = s_next
            if h + 1 < N_HEADS:
                s_next = logits(h + 1)
            rows = slice(h * HEAD_DIM, (h + 1) * HEAD_DIM)
            m_old = m_ref[h]
            m_new = jnp.maximum(m_old, jnp.max(s, axis=0, keepdims=True))
            alpha = jnp.exp2(m_old - m_new)
            p = jnp.exp2(s - m_new).astype(BF16)
            m_ref[h] = m_new
            pv = jnp.dot(jnp.concatenate([vt_ref[rows, :], ones_rows], axis=0), p,
                         preferred_element_type=F32)
            l_ref[h] = alpha * l_ref[h] + pv[HEAD_DIM:HEAD_DIM + 1, :]
            acc_ref[rows, :] = alpha * acc_ref[rows, :] + pv[:HEAD_DIM, :]

    if n_past:
        @pl.when(step < n_past)
        def _():
            process(kp_ref, vtp_ref, fpp_ref, tkp, None)

    j = step - n_past
    last = (qi * tq + tq - 1) // tks
    has_hidden = (j + 1) * tks - 1 > qi * tq

    @pl.when((j >= 0) & (j <= last) & has_hidden)
    def _():
        process(ks_ref, vts_ref, fps_ref, tks, j * tks)

    @pl.when((j >= 0) & (j <= last) & jnp.logical_not(has_hidden))
    def _():
        process(ks_ref, vts_ref, fps_ref, tks, None)

    @pl.when(step == n_past + n_self - 1)
    def _():
        r = lax.broadcasted_iota(jnp.int32, (tq, tq), 0)
        c = lax.broadcasted_iota(jnp.int32, (tq, tq), 1)
        eye = jnp.where(r == c, 1.0, 0.0).astype(BF16)
        row = lax.broadcasted_iota(jnp.int32, (LANES, 1), 0)
        for hp in range(N_HEADS // 2):
            denom = jnp.where(row < HEAD_DIM, l_ref[2 * hp], l_ref[2 * hp + 1])
            o_t = (acc_ref[hp * LANES:(hp + 1) * LANES, :] / denom).astype(BF16)
            o_ref[:, hp * LANES:(hp + 1) * LANES] = lax.dot_general(
                eye, o_t, _NT, preferred_element_type=F32).astype(BF16)


def _flash_call(q, k_self, vt_self, past, fp, n_past_keys):
    bsz, t_q, _ = q.shape
    tq = 512 if t_q % 512 == 0 else t_q
    tks = 512 if t_q % 512 == 0 else t_q
    n_self = t_q // tks
    tkp = 512
    n_past = n_past_keys // tkp
    fp_self_base = n_past_keys // tks

    def self_idx(i, s):
        return jnp.clip(s - n_past, 0, (i * tq + tq - 1) // tks)

    ins = [q]
    specs = [pl.BlockSpec((None, tq, PAD_W), lambda b, i, s: (b, i, 0))]
    if n_past:
        past_idx = lambda s: jnp.minimum(s, n_past - 1)
        ins += [past[0], past[1], fp]
        specs += [pl.BlockSpec((None, tkp, D_MODEL), lambda b, i, s: (b, past_idx(s), 0)),
                  pl.BlockSpec((None, D_MODEL, tkp), lambda b, i, s: (b, 0, past_idx(s))),
                  pl.BlockSpec((None, tkp, LANES), lambda b, i, s: (b, past_idx(s), 0))]
    ins += [k_self, vt_self, fp]
    specs += [pl.BlockSpec((None, tks, D_MODEL), lambda b, i, s: (b, self_idx(i, s), 0)),
              pl.BlockSpec((None, D_MODEL, tks), lambda b, i, s: (b, 0, self_idx(i, s))),
              pl.BlockSpec((None, tks, LANES), lambda b, i, s: (b, fp_self_base + self_idx(i, s), 0))]
    return pl.pallas_call(
        functools.partial(_flash_body, tq, tkp, tks, n_past, n_self),
        grid=(bsz, t_q // tq, n_past + n_self),
        in_specs=specs,
        out_specs=pl.BlockSpec((None, tq, D_MODEL), lambda b, i, s: (b, i, 0)),
        out_shape=jax.ShapeDtypeStruct((bsz, t_q, D_MODEL), BF16),
        scratch_shapes=[pltpu.VMEM((N_HEADS, 1, tq), F32), pltpu.VMEM((N_HEADS, 1, tq), F32),
                        pltpu.VMEM((D_MODEL, tq), F32)],
        compiler_params=_cparams(("arbitrary", "arbitrary", "arbitrary")),
        name="fox_attention",
    )(*ins)


def _oproj_body(x_ref, o_ref, wo_ref, ada_ref, ng_ref, wrt_ref, br_ref,
                x1_ref, pay_ref, info_ref, cnt_ref, run_ref):
    @pl.when((pl.program_id(0) == 0) & (pl.program_id(1) == 0))
    def _():
        run_ref[...] = jnp.zeros_like(run_ref)

    ada = ada_ref[...]
    x1 = x_ref[...] + ada[2:3] * jnp.dot(o_ref[...], wo_ref[...], preferred_element_type=F32)
    x1_ref[...] = x1
    _moe_prep(x1, ng_ref[...][1:2], ada[3:4], ada[4:5], wrt_ref, br_ref, run_ref, pay_ref, info_ref, cnt_ref)


def _oproj_call(x, o, wo, ada_l, ng_l, wrt, br):
    bsz, t, _ = x.shape
    tm = _tile_rows(t)
    p_shapes, p_specs = _prep_out(bsz, t, tm)
    return pl.pallas_call(
        _oproj_body,
        grid=(bsz, t // tm),
        in_specs=[_tok_spec(tm), _tok_spec(tm), _const_spec((KV_WIDTH, D_MODEL)), _per_batch_spec(6),
                  _const_spec((2, D_MODEL)), _const_spec((N_EXPERTS, D_MODEL)), _const_spec((N_EXPERTS, 1))],
        out_specs=[_tok_spec(tm)] + p_specs,
        out_shape=[jax.ShapeDtypeStruct((bsz, t, D_MODEL), F32)] + p_shapes,
        scratch_shapes=[pltpu.VMEM((CLS_ROWS, LANES), F32)],
        compiler_params=_cparams(("arbitrary", "arbitrary")),
        name="attn_out_proj",
    )(x, o, wo, ada_l, ng_l, wrt, br)


def _final_body(x_ref, y_ref, g_ref, o_ref):
    o_ref[...] = x_ref[...] + g_ref[...] * y_ref[...]


def _final_call(x1, y, g2):
    bsz, t, _ = x1.shape
    tm = _tile_rows(t, 1024)
    return pl.pallas_call(
        _final_body, grid=(bsz, t // tm),
        in_specs=[_tok_spec(tm), _tok_spec(tm), _per_batch_spec(1)],
        out_specs=_tok_spec(tm),
        out_shape=jax.ShapeDtypeStruct(x1.shape, F32),
        compiler_params=_cparams(("arbitrary", "arbitrary")),
        name="final_residual",
    )(x1, y, g2)


def _sc_worker_loop(n_win, fn):
    wid = lax.axis_index("s") * SC_CORES + lax.axis_index("c")
    n_workers = SC_CORES * SC_SUBCORES

    @pl.loop(0, pl.cdiv(n_win, n_workers))
    def _(j):
        win = j * n_workers + wid

        @pl.when(win < n_win)
        def _():
            fn(pl.multiple_of(win * SC_WINDOW, SC_WINDOW))


def _sc_scatter_rows(rows, idx, n_out):
    n, width = rows.shape
    mesh = plsc.VectorSubcoreMesh(core_axis_name="c", subcore_axis_name="s")

    @functools.partial(
        pl.kernel, mesh=mesh, out_type=jax.ShapeDtypeStruct((n_out, width), rows.dtype),
        scratch_types=[pltpu.VMEM((SC_WINDOW,), jnp.int32), pltpu.VMEM((SC_WINDOW, width), rows.dtype)])
    def k(rows_hbm, idx_hbm, out_hbm, idx_v, rows_v):
        def one(base):
            pltpu.sync_copy(idx_hbm.at[pl.ds(base, SC_WINDOW)], idx_v)
            pltpu.sync_copy(rows_hbm.at[pl.ds(base, SC_WINDOW)], rows_v)
            pltpu.sync_copy(rows_v, out_hbm.at[idx_v])
        _sc_worker_loop(n // SC_WINDOW, one)

    return k(rows, idx)


def _sc_gather_rows(table, idx):
    n = idx.shape[0]
    width = table.shape[1]
    mesh = plsc.VectorSubcoreMesh(core_axis_name="c", subcore_axis_name="s")

    @functools.partial(
        pl.kernel, mesh=mesh, out_type=jax.ShapeDtypeStruct((n, width), table.dtype),
        scratch_types=[pltpu.VMEM((SC_WINDOW,), jnp.int32), pltpu.VMEM((SC_WINDOW, width), table.dtype)])
    def k(table_hbm, idx_hbm, out_hbm, idx_v, rows_v):
        def one(base):
            pltpu.sync_copy(idx_hbm.at[pl.ds(base, SC_WINDOW)], idx_v)
            pltpu.sync_copy(table_hbm.at[idx_v], rows_v)
            pltpu.sync_copy(rows_v, out_hbm.at[pl.ds(base, SC_WINDOW)])
        _sc_worker_loop(n // SC_WINDOW, one)

    return k(table, idx)


def _moe_body(e1_ref, e2_ref, na_ref, hs_ref, g1_ref, g2_ref, u1_ref, u2_ref, d1_ref, d2_ref, o_ref):
    @pl.when(pl.program_id(0) < na_ref[0])
    def _():
        blk = hs_ref[...]
        h = _unpack_bf16_pairs(blk[:, :PACK_W])
        y = None
        for lane, (g_ref, u_ref, d_ref) in enumerate(((g1_ref, u1_ref, d1_ref), (g2_ref, u2_ref, d2_ref))):
            gate = jnp.dot(h, g_ref[...].astype(BF16), preferred_element_type=F32)
            up = jnp.dot(h, u_ref[...].astype(BF16), preferred_element_type=F32)
            w = blk[:, PACK_W + lane:PACK_W + lane + 1]
            act = (gate * jax.nn.sigmoid(gate) * up * w).astype(BF16)
            term = jnp.dot(act, d_ref[...].astype(BF16), preferred_element_type=F32)
            y = term if y is None else y + term
        o_ref[...] = y


def _moe_call(hs, tile_e1, tile_e2, n_active, wg, wu, wd, layer, tm):
    n_s = hs.shape[0]
    n_tiles = n_s // tm
    base = layer * N_EXPERTS

    def row_map(i, e1, e2, na):
        return (jnp.minimum(i, na[0] - 1), 0)

    def w_map(which):
        def m(i, e1, e2, na):
            e = (e1, e2)[which]
            return (base + e[jnp.minimum(i, na[0] - 1)], 0, 0)
        return m

    gu = lambda which: pl.BlockSpec((None, D_MODEL, D_EXPERT), w_map(which))
    dn = lambda which: pl.BlockSpec((None, D_EXPERT, D_MODEL), w_map(which))
    return pl.pallas_call(
        _moe_body,
        grid_spec=pltpu.PrefetchScalarGridSpec(
            num_scalar_prefetch=3, grid=(n_tiles,),
            in_specs=[pl.BlockSpec((tm, PAY_W), row_map), gu(0), gu(1), gu(0), gu(1), dn(0), dn(1)],
            out_specs=pl.BlockSpec((tm, D_MODEL), row_map)),
        out_shape=jax.ShapeDtypeStruct((n_s, D_MODEL), F32),
        compiler_params=_cparams(("arbitrary",)),
        name="grouped_experts",
    )(tile_e1, tile_e2, n_active, hs, wg, wg, wu, wu, wd, wd)


_PAIR_LO = (0, 0, 0, 1, 1, 2)
_PAIR_HI = (1, 2, 3, 2, 3, 3)


def _moe_layer(pay, info, counts, wg, wu, wd, layer):
    bsz, t, _ = pay.shape
    n = bsz * t
    tm = 256
    n_s = ((n + N_CLASSES * (tm - 1)) // tm + 1) * tm
    n_tiles = n_s // tm
    cls = info[:, 0, :].reshape(n)
    rank = info[:, 1, :].reshape(n)
    cnt = counts[:N_CLASSES, 0].astype(jnp.int32)
    padded = ((cnt + tm - 1) // tm) * tm
    ends = jnp.cumsum(padded)
    starts = ends - padded
    dest = starts[cls] + rank
    tile_start = jnp.arange(n_tiles, dtype=jnp.int32) * tm
    tile_cls = jnp.minimum(jnp.sum((tile_start[:, None] >= ends[None, :]).astype(jnp.int32), axis=1),
                           N_CLASSES - 1)
    grp = tile_cls // N_PAIRS
    pr = tile_cls % N_PAIRS
    tile_e1 = grp * EXPERTS_PER_GROUP + jnp.asarray(_PAIR_LO, jnp.int32)[pr]
    tile_e2 = grp * EXPERTS_PER_GROUP + jnp.asarray(_PAIR_HI, jnp.int32)[pr]
    n_active = (ends[-1:] // tm).astype(jnp.int32)
    hs = _sc_scatter_rows(pay.reshape(n, PAY_W), dest, n_s)
    ys = _moe_call(hs, tile_e1, tile_e2, n_active, wg, wu, wd, layer, tm)
    return _sc_gather_rows(ys, dest).reshape(bsz, t, D_MODEL)


def _trunk(x, ada, kva, hist, past, prm):
    bsz, t, _ = x.shape
    start_pos = 0 if past is None else past[0].shape[1]
    wrt, br = prm["wrt"], prm["br"]
    res = None
    new_hist = []
    for layer in range(N_A_LAYERS):
        if hist is None:
            h16 = jnp.zeros((bsz, HIST_ROWS, D_MODEL), F32)
        else:
            h16 = jnp.pad(hist[layer], ((0, 0), (1, 0), (0, 0)))
        x, pay, info, counts, hout = _mixer_call(
            x, res, h16, ada[layer], prm["norm_g"][layer], prm["w_pool"][layer], prm["pool_scale"][layer],
            wrt, br, start_pos)
        new_hist.append(hout)
        y = _moe_layer(pay, info, counts, prm["w_gate"], prm["w_up"], prm["w_down"], layer)
        res = (y, ada[layer][:, 5:6, :])

    shared = None
    for j in range(N_B_LAYERS):
        layer = N_A_LAYERS + j
        y, g2p = res
        if j == 0:
            kv = (kva, prm["kv_norm"], prm["w_kv"], prm["w_f"], prm["b_f"], prm["k_norm"])
            xr, q, k, v, kb, vt, logf, lfw = _proj_call(
                x, y, g2p, ada[layer], prm["norm_g"][layer], prm["w_q"][j], prm["q_norm"][j],
                prm["s_mat"], prm["st_mat"], kv)
            if past is None:
                past_kv = None
                fp = _decay_call(lfw)
            else:
                ck, cv, clogf = past
                n_past = ck.shape[1]
                pad = -(n_past + t) % 512
                lfw = jnp.concatenate([jnp.pad(clogf, ((0, 0), (0, 0), (0, LANES - N_HEADS))), lfw,
                                       jnp.zeros((bsz, pad, LANES), F32)], axis=1)
                fp = _decay_call(lfw)
                past_kv = _cache_prep_call(ck, cv)
            shared = (k, v, logf, kb, vt, past_kv, fp)
        else:
            xr, q = _proj_call(x, y, g2p, ada[layer], prm["norm_g"][layer], prm["w_q"][j], prm["q_norm"][j],
                               prm["s_mat"], prm["st_mat"])
        o = _flash_call(q, shared[3], shared[4], shared[5], shared[6], start_pos)
        x, pay, info, counts = _oproj_call(xr, o, prm["w_o"][j], ada[layer], prm["norm_g"][layer], wrt, br)
        y = _moe_layer(pay, info, counts, prm["w_gate"], prm["w_up"], prm["w_down"], layer)
        res = (y, ada[layer][:, 5:6, :])

    out = _final_call(x, res[0], res[1])
    k, v, logf = shared[:3]
    return (out, jnp.stack(new_hist), k.reshape(bsz, t, N_HEADS, HEAD_DIM),
            v.reshape(bsz, t, N_HEADS, HEAD_DIM), logf)


def kernel(x_prompt, x_sample, cache_pool, cache_k, cache_v, cache_logf, c_prompt, c_sample, ada_w, ada_b, norm_g, w_pool, pool_scale, kv_ada_w, kv_ada_b, kv_norm, w_kvf, b_f, k_norm, w_q, q_norm, w_o, w_router, b_router, w_gate, w_up, w_down):
    bp = x_prompt.shape[0]
    c_all = jnp.concatenate([c_prompt, c_sample], axis=0)
    bc = c_all.shape[0]
    ada = _ada_call(c_all, ada_w, ada_b).reshape(DEPTH, bc, 6, D_MODEL)
    kva = _ada_call(c_all, kv_ada_w[None], kv_ada_b[None]).reshape(bc, 2, D_MODEL)

    head_of_lane = jnp.arange(D_MODEL, dtype=jnp.int32) // HEAD_DIM
    s_mat = (head_of_lane[:, None] == jnp.arange(LANES, dtype=jnp.int32)[None, :]).astype(BF16)
    prm = {
        "norm_g": norm_g,
        "w_pool": w_pool.astype(BF16),
        "pool_scale": pool_scale.reshape(N_A_LAYERS, 1, D_MODEL),
        "kv_norm": kv_norm.reshape(1, D_MODEL),
        "w_kv": w_kvf[:, :2 * KV_WIDTH].astype(BF16),
        "w_f": jnp.pad(w_kvf[:, 2 * KV_WIDTH:], ((0, 0), (0, LANES - N_HEADS))),
        "b_f": jnp.pad(b_f, (0, LANES - N_HEADS)).reshape(1, LANES),
        "k_norm": jnp.tile(k_norm, N_HEADS).reshape(1, D_MODEL),
        "w_q": w_q.astype(BF16),
        "q_norm": jnp.tile(q_norm, (1, N_HEADS)).reshape(N_B_LAYERS, 1, D_MODEL),
        "w_o": w_o.astype(BF16),
        "wrt": w_router.T,
        "br": b_router.reshape(N_EXPERTS, 1),
        "w_gate": w_gate.reshape(DEPTH * N_EXPERTS, D_MODEL, D_EXPERT),
        "w_up": w_up.reshape(DEPTH * N_EXPERTS, D_MODEL, D_EXPERT),
        "w_down": w_down.reshape(DEPTH * N_EXPERTS, D_EXPERT, D_MODEL),
        "s_mat": s_mat,
        "st_mat": s_mat.T,
    }
    outs_p = _trunk(x_prompt, ada[:, :bp], kva[:bp], None, None, prm)
    outs_s = _trunk(x_sample, ada[:, bp:], kva[bp:], cache_pool, (cache_k, cache_v, cache_logf), prm)
    return (outs_p[0], outs_s[0]) + outs_p[1:] + outs_s[1:]
```

```python
import functools

import jax
import jax.numpy as jnp
from jax import lax
from jax.experimental import pallas as pl
from jax.experimental.pallas import tpu as pltpu
from jax.experimental.pallas import tpu_sc as plsc

F32 = jnp.float32
BF16 = jnp.bfloat16

D_MODEL = 1024
DEPTH = 4
N_A_LAYERS = DEPTH // 2
N_B_LAYERS = DEPTH - N_A_LAYERS
POOL_WINDOWS = (2, 4, 8, 16)
N_POOL_GROUPS = len(POOL_WINDOWS)
POOL_GROUP_DIM = D_MODEL // N_POOL_GROUPS
POOL_BUF = max(POOL_WINDOWS) - 1
HIST_ROWS = POOL_BUF + 1
N_HEADS = 16
HEAD_DIM = D_MODEL // N_HEADS
KV_WIDTH = N_HEADS * HEAD_DIM
ATTN_SCALE = HEAD_DIM ** -0.5
N_EXPERTS = 16
N_EXPERT_GROUPS = 4
EXPERTS_PER_GROUP = N_EXPERTS // N_EXPERT_GROUPS
N_PAIRS = 6
N_CLASSES = N_EXPERT_GROUPS * N_PAIRS
D_EXPERT = D_MODEL // 2
EPS = 1e-6
NEG_INF = -1e30

LANES = 128
MXU_COLS = 256
SC_CORES = 2
SC_SUBCORES = 16
SC_WINDOW = 64
PACK_W = D_MODEL // 2
PAY_W = PACK_W + LANES
PAD_W = N_HEADS * LANES
LOG2E = 1.4426950408889634
DECAY_PIECES = 3
SUM_ROWS = 16
CLS_ROWS = 32
VMEM_LIMIT = 48 * 1024 * 1024


def _cparams(sem, flags=None):
    return pltpu.CompilerParams(dimension_semantics=sem, vmem_limit_bytes=VMEM_LIMIT, flags=flags)


def _bdot(a, b):
    return jnp.dot(a.astype(BF16), b.astype(BF16), preferred_element_type=F32)


def _split(a):
    hi = a.astype(BF16)
    lo = (a - hi.astype(F32)).astype(BF16)
    return hi, lo


_NN = (((1,), (0,)), ((), ()))
_NT = (((1,), (1,)), ((), ()))


def _dot3(a, b, dims=_NN):
    ah, al = _split(a)
    bh, bl = _split(b)
    d = lambda x, y: lax.dot_general(x, y, dims, preferred_element_type=F32)
    return d(ah, bh) + (d(ah, bl) + d(al, bh))


def _dot2_exact_rhs(a, b_bf16):
    ah, al = _split(a)
    return (jnp.dot(ah, b_bf16, preferred_element_type=F32)
            + jnp.dot(al, b_bf16, preferred_element_type=F32))


def _rms_mod(x, gain, shift, scale):
    ms = jnp.mean(x * x, axis=-1, keepdims=True)
    return (x * lax.rsqrt(ms + EPS)) * (gain * (1.0 + scale)) + shift


def _head_rms(z, s_ref, st_ref, gain):
    ss = _dot2_exact_rhs(z * z, s_ref[...])
    inv = lax.rsqrt(ss * (1.0 / HEAD_DIM) + EPS)
    invf = _dot2_exact_rhs(inv, st_ref[...])
    return z * invf * gain


def _query_blocks(q):
    lane = lax.broadcasted_iota(jnp.int32, (1, LANES), 1)
    blocks = []
    for h in range(N_HEADS):
        pair = q[:, (h // 2) * LANES:(h // 2 + 1) * LANES]
        in_head = (lane >= HEAD_DIM) if h % 2 else (lane < HEAD_DIM)
        blocks.append(jnp.where(in_head, pair, 0.0))
    return blocks


_HI_MASK = 0xFFFF0000


def _pack_bf16_pairs(x):
    bits = lambda a: lax.bitcast_convert_type(a.astype(BF16).astype(F32), jnp.uint32)
    half = x.shape[1] // 2
    word = (bits(x[:, :half]) >> 16) | (bits(x[:, half:]) & jnp.uint32(_HI_MASK))
    return lax.bitcast_convert_type(word, F32)


def _unpack_bf16_pairs(w):
    word = lax.bitcast_convert_type(w, jnp.uint32)
    lo = lax.bitcast_convert_type(word << 16, F32).astype(BF16)
    hi = lax.bitcast_convert_type(word & jnp.uint32(_HI_MASK), F32).astype(BF16)
    return jnp.concatenate([lo, hi], axis=1)


def _pad_rows(a, rows):
    if a.shape[0] == rows:
        return a
    return jnp.concatenate([a, jnp.zeros((rows - a.shape[0], a.shape[1]), a.dtype)], axis=0)


def _route(lt, br):
    m = jnp.max(lt, axis=0, keepdims=True)
    p = jnp.exp(lt - m)
    scores = p / jnp.sum(p, axis=0, keepdims=True)
    sel = scores + br
    row = lambda a, e: a[e:e + 1, :]
    gs = []
    for g in range(N_EXPERT_GROUPS):
        v = [row(sel, g * EXPERTS_PER_GROUP + j) for j in range(EXPERTS_PER_GROUP)]
        best = None
        for i in range(EXPERTS_PER_GROUP):
            for j in range(i + 1, EXPERTS_PER_GROUP):
                s = v[i] + v[j]
                best = s if best is None else jnp.maximum(best, s)
        gs.append(best)
    bg = jnp.zeros_like(gs[0])
    bv = gs[0]
    for g in range(1, N_EXPERT_GROUPS):
        better = gs[g] > bv
        bg = jnp.where(better, float(g), bg)
        bv = jnp.where(better, gs[g], bv)

    def in_group(a, j):
        out = row(a, j)
        for g in range(1, N_EXPERT_GROUPS):
            out = jnp.where(bg == float(g), row(a, g * EXPERTS_PER_GROUP + j), out)
        return out

    sg = [in_group(sel, j) for j in range(EXPERTS_PER_GROUP)]
    cg = [in_group(scores, j) for j in range(EXPERTS_PER_GROUP)]

    def first_argmax(vals):
        mx = vals[0]
        for v in vals[1:]:
            mx = jnp.maximum(mx, v)
        idx = jnp.full_like(mx, float(len(vals) - 1))
        for j in range(len(vals) - 2, -1, -1):
            idx = jnp.where(vals[j] == mx, float(j), idx)
        return idx

    i1 = first_argmax(sg)
    i2 = first_argmax([jnp.where(i1 == float(j), -jnp.inf, sg[j]) for j in range(EXPERTS_PER_GROUP)])
    lo = jnp.minimum(i1, i2)
    hi = jnp.maximum(i1, i2)

    def pick(vals, idx):
        out = vals[0]
        for j in range(1, len(vals)):
            out = jnp.where(idx == float(j), vals[j], out)
        return out

    c_lo = pick(cg, lo)
    c_hi = pick(cg, hi)
    tot = c_lo + c_hi
    pair = jnp.where(lo == 0.0, hi - 1.0, jnp.where(lo == 1.0, hi + 1.0, 5.0))
    return bg * float(N_PAIRS) + pair, c_lo / tot, c_hi / tot


def _moe_prep(x1, ng2, sh2, sc2, wrt_ref, br_ref, run_ref, pay_ref, info_ref, cnt_ref):
    tm = x1.shape[0]
    tr = max(tm, LANES)
    h2 = _rms_mod(x1, ng2, sh2, sc2)
    lt = _dot3(wrt_ref[...], _pad_rows(h2, tr), _NT)
    cls, w_lo, w_hi = _route(lt, br_ref[...])

    r = lax.broadcasted_iota(jnp.int32, (LANES, tr), 0)
    wrows = jnp.where(r == 0, w_lo, jnp.where(r == 1, w_hi, 0.0))
    pay_ref[:, :PACK_W] = _pack_bf16_pairs(h2)
    pay_ref[:, PACK_W:] = wrows.T[:tm, :]

    crow = lax.broadcasted_iota(jnp.int32, (CLS_ROWS, tr), 0).astype(F32)
    lane = lax.broadcasted_iota(jnp.int32, (CLS_ROWS, tr), 1)
    onehot = jnp.where((crow == cls) & (lane < tm), 1.0, 0.0)
    us = lax.broadcasted_iota(jnp.int32, (tr, tr), 0)
    ut = lax.broadcasted_iota(jnp.int32, (tr, tr), 1)
    upper = jnp.where(us < ut, 1.0, 0.0).astype(BF16)
    before = jnp.dot(onehot.astype(BF16), upper, preferred_element_type=F32) + run_ref[:, 0:1]
    rank = jnp.sum(onehot * before, axis=0, keepdims=True)
    run_new = run_ref[...] + jnp.sum(onehot, axis=1, keepdims=True)
    run_ref[...] = run_new
    cnt_ref[...] = run_new
    ir = lax.broadcasted_iota(jnp.int32, (8, tr), 0)
    info = jnp.where(ir == 0, cls, jnp.where(ir == 1, rank, 0.0)).astype(jnp.int32)
    info_ref[...] = info[:, :tm]


def _ada_body(c_ref, w_ref, b_ref, o_ref):
    c = c_ref[...]
    o_ref[...] = _dot3(c * jax.nn.sigmoid(c), w_ref[...]) + b_ref[...]


def _ada_call(c_all, w, b):
    n_l, _, n_out = w.shape
    bc = c_all.shape[0]
    tn = 1536 if n_out % 1536 == 0 else 1024
    return pl.pallas_call(
        _ada_body,
        grid=(n_l, n_out // tn),
        in_specs=[pl.BlockSpec((bc, D_MODEL), lambda l, j: (0, 0)),
                  pl.BlockSpec((None, D_MODEL, tn), lambda l, j: (l, 0, j)),
                  pl.BlockSpec((None, 1, tn), lambda l, j: (l, 0, j))],
        out_specs=pl.BlockSpec((None, bc, tn), lambda l, j: (l, 0, j)),
        out_shape=jax.ShapeDtypeStruct((n_l, bc, n_out), F32),
        compiler_params=_cparams(("arbitrary", "arbitrary")),
        name="adaln",
    )(c_all, w, b.reshape(n_l, 1, n_out))


def _tile_rows(t, rows=512):
    return rows if t % rows == 0 else t


def _tok_spec(tm, width=D_MODEL):
    return pl.BlockSpec((None, tm, width), lambda b, t: (b, t, 0))


def _per_batch_spec(rows, width=D_MODEL):
    return pl.BlockSpec((None, rows, width), lambda b, t: (b, 0, 0))


def _const_spec(shape):
    nd = len(shape)
    return pl.BlockSpec(shape, lambda b, t: (0,) * nd)


def _prep_out(bsz, t, tm):
    shapes = [jax.ShapeDtypeStruct((bsz, t, PAY_W), F32),
              jax.ShapeDtypeStruct((bsz, 8, t), jnp.int32),
              jax.ShapeDtypeStruct((CLS_ROWS, LANES), F32)]
    specs = [_tok_spec(tm, PAY_W),
             pl.BlockSpec((None, 8, tm), lambda b, t: (b, 0, t)),
             _const_spec((CLS_ROWS, LANES))]
    return shapes, specs


def _mixer_body(has_res, tm, start_pos, *refs):
    it = iter(refs)
    x_ref, xp_ref = next(it), next(it)
    if has_res:
        y_ref, yp_ref, g2p_ref = next(it), next(it), next(it)
    hist_ref, ada_ref, ng_ref, wp_ref, ps_ref, wrt_ref, br_ref = (next(it) for _ in range(7))
    x1_ref, pay_ref, info_ref, cnt_ref, hout_ref = (next(it) for _ in range(5))
    run_ref = next(it)

    b = pl.program_id(0)
    t = pl.program_id(1)

    @pl.when((b == 0) & (t == 0))
    def _():
        run_ref[...] = jnp.zeros_like(run_ref)

    xin = x_ref[...]
    xp = xp_ref[...]
    if has_res:
        g2p = g2p_ref[...]
        xin = xin + g2p * y_ref[...]
        xp = xp + g2p * yp_ref[...]
    ada = ada_ref[...]
    sh1, sc1, g1, sh2, sc2 = (ada[i:i + 1] for i in range(5))
    ng = ng_ref[...]
    u = _rms_mod(xin, ng[0:1], sh1, sc1)
    up = _rms_mod(xp, ng[0:1], sh1, sc1)
    up = jnp.where(t == 0, hist_ref[...], up)
    level = jnp.concatenate([up, u], axis=0)
    sums = []
    for g, w in enumerate(POOL_WINDOWS):
        level = level + pltpu.roll(level, w // 2, 0)
        sums.append(level[HIST_ROWS:, :POOL_GROUP_DIM])
        if g + 1 < N_POOL_GROUPS:
            level = level[:, POOL_GROUP_DIM:]

    pos = start_pos + t * tm + lax.broadcasted_iota(jnp.int32, (tm, 1), 0)
    cols = []
    for g, w in enumerate(POOL_WINDOWS):
        sl = slice(g * POOL_GROUP_DIM, (g + 1) * POOL_GROUP_DIM)
        cnt = jnp.minimum(pos + 1, w).astype(F32)
        cols.append(_bdot(sums[g] / cnt - u[:, sl], wp_ref[g]))
    x1 = xin + (g1 * ps_ref[...]) * jnp.concatenate(cols, axis=1)
    x1_ref[...] = x1
    hout_ref[...] = u[tm - POOL_BUF:, :]
    _moe_prep(x1, ng[1:2], sh2, sc2, wrt_ref, br_ref, run_ref, pay_ref, info_ref, cnt_ref)


def _mixer_call(x, res, hist16, ada_l, ng_l, wp_l, ps_l, wrt, br, start_pos):
    bsz, t, _ = x.shape
    tm = _tile_rows(t)
    prev_spec = pl.BlockSpec((None, HIST_ROWS, D_MODEL),
                             lambda b, i: (b, jnp.maximum(i * (tm // HIST_ROWS) - 1, 0), 0))
    ins = [x, x]
    specs = [_tok_spec(tm), prev_spec]
    if res is not None:
        y, g2p = res
        ins += [y, y, g2p]
        specs += [_tok_spec(tm), prev_spec, _per_batch_spec(1)]
    ins += [hist16, ada_l, ng_l, wp_l, ps_l, wrt, br]
    specs += [_per_batch_spec(HIST_ROWS), _per_batch_spec(6), _const_spec((2, D_MODEL)),
              _const_spec((N_POOL_GROUPS, POOL_GROUP_DIM, POOL_GROUP_DIM)), _const_spec((1, D_MODEL)),
              _const_spec((N_EXPERTS, D_MODEL)), _const_spec((N_EXPERTS, 1))]
    p_shapes, p_specs = _prep_out(bsz, t, tm)
    return pl.pallas_call(
        functools.partial(_mixer_body, res is not None, tm, start_pos),
        grid=(bsz, t // tm),
        in_specs=specs,
        out_specs=[_tok_spec(tm)] + p_specs + [_per_batch_spec(POOL_BUF)],
        out_shape=[jax.ShapeDtypeStruct((bsz, t, D_MODEL), F32)] + p_shapes
                  + [jax.ShapeDtypeStruct((bsz, POOL_BUF, D_MODEL), F32)],
        scratch_shapes=[pltpu.VMEM((CLS_ROWS, LANES), F32)],
        compiler_params=_cparams(("arbitrary", "arbitrary")),
        name="pool_mixer",
    )(*ins)


def _proj_body(with_kv, tm, *refs):
    it = iter(refs)
    x1p_ref, y_ref, g2p_ref, ada_ref, ng_ref, wq_ref, qn_ref, s_ref, st_ref = (next(it) for _ in range(9))
    if with_kv:
        kva_ref, kvn_ref, wkv_ref, wf_ref, bf_ref, kn_ref = (next(it) for _ in range(6))
    x_ref, q_ref = next(it), next(it)
    if with_kv:
        k_ref, v_ref, kb_ref, vt_ref, lf_ref, lfw_ref = (next(it) for _ in range(6))

    x = x1p_ref[...] + g2p_ref[...] * y_ref[...]
    x_ref[...] = x
    ada = ada_ref[...]
    ng = ng_ref[...]
    h = _rms_mod(x, ng[0:1], ada[0:1], ada[1:2])
    q = _head_rms(_bdot(h, wq_ref[...]), s_ref, st_ref, qn_ref[...])
    rows = max(tm, LANES)
    for i, blk in enumerate(_query_blocks(q * (ATTN_SCALE * LOG2E))):
        q_ref[i * LANES:(i + 1) * LANES, :] = _pad_rows(blk, rows).T[:, :tm].astype(BF16)
    if with_kv:
        kva = kva_ref[...]
        hk = _rms_mod(x, kvn_ref[...], kva[0:1], kva[1:2])
        proj = _bdot(hk, wkv_ref[...])
        k = _head_rms(proj[:, :KV_WIDTH], s_ref, st_ref, kn_ref[...])
        v = proj[:, KV_WIDTH:]
        k_ref[...] = k
        v_ref[...] = v
        kb_ref[...] = k.astype(BF16)
        vt_ref[...] = _pad_rows(v, max(tm, LANES)).T[:, :tm].astype(BF16)
        z = _dot3(hk, wf_ref[...]) + bf_ref[...]
        lf = jnp.minimum(z, 0.0) - jnp.log(1.0 + jnp.exp(-jnp.abs(z)))
        lf_ref[...] = lf[:, :N_HEADS]
        lane = lax.broadcasted_iota(jnp.int32, (1, LANES), 1)
        lfw_ref[...] = jnp.where(lane < N_HEADS, lf, 0.0)


def _proj_call(x1p, y, g2p, ada_l, ng_l, wq, qn, s_mat, st_mat, kv=None):
    bsz, t, _ = x1p.shape
    tm = _tile_rows(t, 256)
    ins = [x1p, y, g2p, ada_l, ng_l, wq, qn, s_mat, st_mat]
    specs = [_tok_spec(tm), _tok_spec(tm), _per_batch_spec(1), _per_batch_spec(6), _const_spec((2, D_MODEL)),
             _const_spec((D_MODEL, KV_WIDTH)), _const_spec((1, D_MODEL)),
             _const_spec((D_MODEL, LANES)), _const_spec((LANES, D_MODEL))]
    out_shapes = [jax.ShapeDtypeStruct((bsz, t, D_MODEL), F32), jax.ShapeDtypeStruct((bsz, PAD_W, t), BF16)]
    out_specs = [_tok_spec(tm), pl.BlockSpec((None, PAD_W, tm), lambda b, i: (b, 0, i))]
    if kv is not None:
        kva, kvn, wkv, wf, bf, kn = kv
        ins += [kva, kvn, wkv, wf, bf, kn]
        specs += [_per_batch_spec(2), _const_spec((1, D_MODEL)), _const_spec((D_MODEL, 2 * KV_WIDTH)),
                  _const_spec((D_MODEL, LANES)), _const_spec((1, LANES)), _const_spec((1, D_MODEL))]
        out_shapes += [jax.ShapeDtypeStruct((bsz, t, D_MODEL), F32)] * 2
        out_shapes += [jax.ShapeDtypeStruct((bsz, t, D_MODEL), BF16), jax.ShapeDtypeStruct((bsz, D_MODEL, t), BF16)]
        out_shapes += [jax.ShapeDtypeStruct((bsz, t, N_HEADS), F32), jax.ShapeDtypeStruct((bsz, t, LANES), F32)]
        out_specs += [_tok_spec(tm)] * 2
        out_specs += [_tok_spec(tm), pl.BlockSpec((None, D_MODEL, tm), lambda b, i: (b, 0, i))]
        out_specs += [_tok_spec(tm, N_HEADS), _tok_spec(tm, LANES)]
    return pl.pallas_call(
        functools.partial(_proj_body, kv is not None, tm),
        grid=(bsz, t // tm),
        in_specs=specs, out_specs=out_specs, out_shape=out_shapes,
        compiler_params=_cparams(("arbitrary", "arbitrary")),
        name="qkv_proj" if kv is not None else "q_proj",
    )(*ins)


def _decay_body(tc, lf_ref, o_ref, carry_ref):
    @pl.when(pl.program_id(1) == 0)
    def _():
        carry_ref[...] = jnp.zeros_like(carry_ref)

    lf = lf_ref[...]
    r = lax.broadcasted_iota(jnp.int32, (tc, tc), 0)
    c = lax.broadcasted_iota(jnp.int32, (tc, tc), 1)
    lower = jnp.where(r >= c, 1.0, 0.0).astype(BF16)
    hi, lo = _split(lf)
    f = (jnp.dot(lower, hi, preferred_element_type=F32) + jnp.dot(lower, lo, preferred_element_type=F32)
         + carry_ref[0:1, :])
    carry_ref[...] = jnp.broadcast_to(f[tc - 1:tc, :], carry_ref.shape)
    nb = f * (-LOG2E)
    p1 = nb.astype(BF16)
    r1 = nb - p1.astype(F32)
    p2 = r1.astype(BF16)
    p3 = (r1 - p2.astype(F32)).astype(BF16)
    hr = lax.broadcasted_iota(jnp.int32, (LANES, LANES), 0)
    lc = lax.broadcasted_iota(jnp.int32, (LANES, LANES), 1)
    out = None
    for i, p in enumerate((p1, p2, p3)):
        place = jnp.where((lc == DECAY_PIECES * hr + i) & (hr < N_HEADS), 1.0, 0.0).astype(BF16)
        term = jnp.dot(p, place, preferred_element_type=F32)
        out = term if out is None else out + term
    o_ref[...] = out.astype(BF16)


def _decay_call(lfw):
    bsz, tk, _ = lfw.shape
    tc = 512
    spec = pl.BlockSpec((None, tc, LANES), lambda b, t: (b, t, 0))
    return pl.pallas_call(
        functools.partial(_decay_body, tc),
        grid=(bsz, tk // tc), in_specs=[spec], out_specs=spec,
        out_shape=jax.ShapeDtypeStruct(lfw.shape, BF16),
        scratch_shapes=[pltpu.VMEM((8, LANES), F32)],
        compiler_params=_cparams(("arbitrary", "arbitrary")),
        name="decay_bias",
    )(lfw)


def _transpose_body(v_ref, o_ref):
    o_ref[...] = v_ref[...].T.astype(BF16)


def _transpose_call(v):
    bsz, p, _ = v.shape
    tp = 512
    return pl.pallas_call(
        _transpose_body, grid=(bsz, p // tp),
        in_specs=[pl.BlockSpec((None, tp, D_MODEL), lambda b, t: (b, t, 0))],
        out_specs=pl.BlockSpec((None, D_MODEL, tp), lambda b, t: (b, 0, t)),
        out_shape=jax.ShapeDtypeStruct((bsz, D_MODEL, p), BF16),
        compiler_params=_cparams(("arbitrary", "arbitrary")),
        name="value_transpose",
    )(v)


def _flash_body(tq, tkp, tks, n_past, n_self, *refs):
    it = iter(refs)
    q_ref = next(it)
    if n_past:
        kp_ref, vtp_ref, fpp_ref = next(it), next(it), next(it)
    ks_ref, vts_ref, fps_ref = next(it), next(it), next(it)
    o_ref, m_ref, l_ref, acc_ref = next(it), next(it), next(it), next(it)
    qi = pl.program_id(1)
    step = pl.program_id(2)

    @pl.when(step == 0)
    def _():
        m_ref[...] = jnp.full_like(m_ref, NEG_INF)
        l_ref[...] = jnp.zeros_like(l_ref)
        acc_ref[...] = jnp.zeros_like(acc_ref)

    def process(k_ref, vt_ref, fp_ref, tk, key_base):
        fp = fp_ref[...]
        piece_row = lax.broadcasted_iota(jnp.int32, (LANES, 1), 0)
        if key_base is not None:
            kpos = key_base + lax.broadcasted_iota(jnp.int32, (tk, 1), 0)
            qpos = qi * tq + lax.broadcasted_iota(jnp.int32, (1, tq), 1)
            visible = kpos <= qpos

        pair_keys = {}

        def logits(h):
            d = piece_row - DECAY_PIECES * h
            ones = jnp.where((d >= 0) & (d < DECAY_PIECES), 1.0, 0.0).astype(BF16)
            hp = h // 2
            if hp not in pair_keys:
                pair_keys[hp] = k_ref[:, hp * LANES:(hp + 1) * LANES].astype(BF16)
            lhs = jnp.concatenate([pair_keys[hp], fp], axis=1)
            rhs = jnp.concatenate([q_ref[h * LANES:(h + 1) * LANES, :],
                                   jnp.broadcast_to(ones, (LANES, tq))], axis=0)
            if tq < 2 * MXU_COLS and tk % 2 == 0:
                half = tk // 2
                s = jnp.concatenate([jnp.dot(lhs[:half], rhs, preferred_element_type=F32),
                                     jnp.dot(lhs[half:], rhs, preferred_element_type=F32)], axis=0)
            else:
                s = jnp.dot(lhs, rhs, preferred_element_type=F32)
            return s if key_base is None else jnp.where(visible, s, NEG_INF)

        ones_rows = jnp.ones((SUM_ROWS, tk), BF16)
        s_next = logits(0)
        for h in range(N_HEADS):
            s = s_next
            if h + 1 < N_HEADS:
                s_next = logits(h + 1)
            rows = slice(h * HEAD_DIM, (h + 1) * HEAD_DIM)
            m_old = m_ref[h]
            m_new = jnp.maximum(m_old, jnp.max(s, axis=0, keepdims=True))
            alpha = jnp.exp2(m_old - m_new)
            p = jnp.exp2(s - m_new).astype(BF16)
            m_ref[h] = m_new
            pv = jnp.dot(jnp.concatenate([vt_ref[rows, :], ones_rows], axis=0), p,
                         preferred_element_type=F32)
            l_ref[h] = alpha * l_ref[h] + pv[HEAD_DIM:HEAD_DIM + 1, :]
            acc_ref[rows, :] = alpha * acc_ref[rows, :] + pv[:HEAD_DIM, :]

    if n_past:
        @pl.when(step < n_past)
        def _():
            process(kp_ref, vtp_ref, fpp_ref, tkp, None)

    j = step - n_past
    last = (qi * tq + tq - 1) // tks
    has_hidden = (j + 1) * tks - 1 > qi * tq

    @pl.when((j >= 0) & (j <= last) & has_hidden)
    def _():
        process(ks_ref, vts_ref, fps_ref, tks, j * tks)

    @pl.when((j >= 0) & (j <= last) & jnp.logical_not(has_hidden))
    def _():
        process(ks_ref, vts_ref, fps_ref, tks, None)

    @pl.when(step == n_past + n_self - 1)
    def _():
        r = lax.broadcasted_iota(jnp.int32, (tq, tq), 0)
        c = lax.broadcasted_iota(jnp.int32, (tq, tq), 1)
        eye = jnp.where(r == c, 1.0, 0.0).astype(BF16)
        row = lax.broadcasted_iota(jnp.int32, (LANES, 1), 0)
        for hp in range(N_HEADS // 2):
            denom = jnp.where(row < HEAD_DIM, l_ref[2 * hp], l_ref[2 * hp + 1])
            o_t = (acc_ref[hp * LANES:(hp + 1) * LANES, :] / denom).astype(BF16)
            o_ref[:, hp * LANES:(hp + 1) * LANES] = lax.dot_general(
                eye, o_t, _NT, preferred_element_type=F32).astype(BF16)


def _flash_call(q, k_self, vt_self, past, fp, n_past_keys):
    bsz, _, t_q = q.shape
    tq = 512 if t_q % 512 == 0 else t_q
    tks = 512 if t_q % 512 == 0 else t_q
    n_self = t_q // tks
    tkp = 512
    n_past = n_past_keys // tkp
    fp_self_base = n_past_keys // tks

    def self_idx(i, s):
        return jnp.clip(s - n_past, 0, (i * tq + tq - 1) // tks)

    ins = [q]
    specs = [pl.BlockSpec((None, PAD_W, tq), lambda b, i, s: (b, 0, i))]
    if n_past:
        past_idx = lambda s: jnp.minimum(s, n_past - 1)
        ins += [past[0], past[1], fp]
        specs += [pl.BlockSpec((None, tkp, D_MODEL), lambda b, i, s: (b, past_idx(s), 0)),
                  pl.BlockSpec((None, D_MODEL, tkp), lambda b, i, s: (b, 0, past_idx(s))),
                  pl.BlockSpec((None, tkp, LANES), lambda b, i, s: (b, past_idx(s), 0))]
    ins += [k_self, vt_self, fp]
    specs += [pl.BlockSpec((None, tks, D_MODEL), lambda b, i, s: (b, self_idx(i, s), 0)),
              pl.BlockSpec((None, D_MODEL, tks), lambda b, i, s: (b, 0, self_idx(i, s))),
              pl.BlockSpec((None, tks, LANES), lambda b, i, s: (b, fp_self_base + self_idx(i, s), 0))]
    return pl.pallas_call(
        functools.partial(_flash_body, tq, tkp, tks, n_past, n_self),
        grid=(bsz, t_q // tq, n_past + n_self),
        in_specs=specs,
        out_specs=pl.BlockSpec((None, tq, D_MODEL), lambda b, i, s: (b, i, 0)),
        out_shape=jax.ShapeDtypeStruct((bsz, t_q, D_MODEL), BF16),
        scratch_shapes=[pltpu.VMEM((N_HEADS, 1, tq), F32), pltpu.VMEM((N_HEADS, 1, tq), F32),
                        pltpu.VMEM((D_MODEL, tq), F32)],
        compiler_params=_cparams(("arbitrary", "arbitrary", "arbitrary")),
        name="fox_attention",
    )(*ins)


def _oproj_body(x_ref, o_ref, wo_ref, ada_ref, ng_ref, wrt_ref, br_ref,
                x1_ref, pay_ref, info_ref, cnt_ref, run_ref):
    @pl.when((pl.program_id(0) == 0) & (pl.program_id(1) == 0))
    def _():
        run_ref[...] = jnp.zeros_like(run_ref)

    ada = ada_ref[...]
    x1 = x_ref[...] + ada[2:3] * jnp.dot(o_ref[...], wo_ref[...], preferred_element_type=F32)
    x1_ref[...] = x1
    _moe_prep(x1, ng_ref[...][1:2], ada[3:4], ada[4:5], wrt_ref, br_ref, run_ref, pay_ref, info_ref, cnt_ref)


def _oproj_call(x, o, wo, ada_l, ng_l, wrt, br):
    bsz, t, _ = x.shape
    tm = _tile_rows(t)
    p_shapes, p_specs = _prep_out(bsz, t, tm)
    return pl.pallas_call(
        _oproj_body,
        grid=(bsz, t // tm),
        in_specs=[_tok_spec(tm), _tok_spec(tm), _const_spec((KV_WIDTH, D_MODEL)), _per_batch_spec(6),
                  _const_spec((2, D_MODEL)), _const_spec((N_EXPERTS, D_MODEL)), _const_spec((N_EXPERTS, 1))],
        out_specs=[_tok_spec(tm)] + p_specs,
        out_shape=[jax.ShapeDtypeStruct((bsz, t, D_MODEL), F32)] + p_shapes,
        scratch_shapes=[pltpu.VMEM((CLS_ROWS, LANES), F32)],
        compiler_params=_cparams(("arbitrary", "arbitrary")),
        name="attn_out_proj",
    )(x, o, wo, ada_l, ng_l, wrt, br)


def _final_body(x_ref, y_ref, g_ref, o_ref):
    o_ref[...] = x_ref[...] + g_ref[...] * y_ref[...]


def _final_call(x1, y, g2):
    bsz, t, _ = x1.shape
    tm = _tile_rows(t, 1024)
    return pl.pallas_call(
        _final_body, grid=(bsz, t // tm),
        in_specs=[_tok_spec(tm), _tok_spec(tm), _per_batch_spec(1)],
        out_specs=_tok_spec(tm),
        out_shape=jax.ShapeDtypeStruct(x1.shape, F32),
        compiler_params=_cparams(("arbitrary", "arbitrary")),
        name="final_residual",
    )(x1, y, g2)


def _sc_worker_loop(n_win, fn):
    wid = lax.axis_index("s") * SC_CORES + lax.axis_index("c")
    n_workers = SC_CORES * SC_SUBCORES

    @pl.loop(0, pl.cdiv(n_win, n_workers))
    def _(j):
        win = j * n_workers + wid

        @pl.when(win < n_win)
        def _():
            fn(pl.multiple_of(win * SC_WINDOW, SC_WINDOW))


def _sc_scatter_rows(rows, idx, n_out):
    n, width = rows.shape
    mesh = plsc.VectorSubcoreMesh(core_axis_name="c", subcore_axis_name="s")

    @functools.partial(
        pl.kernel, mesh=mesh, out_type=jax.ShapeDtypeStruct((n_out, width), rows.dtype),
        scratch_types=[pltpu.VMEM((SC_WINDOW,), jnp.int32), pltpu.VMEM((SC_WINDOW, width), rows.dtype)])
    def k(rows_hbm, idx_hbm, out_hbm, idx_v, rows_v):
        def one(base):
            pltpu.sync_copy(idx_hbm.at[pl.ds(base, SC_WINDOW)], idx_v)
            pltpu.sync_copy(rows_hbm.at[pl.ds(base, SC_WINDOW)], rows_v)
            pltpu.sync_copy(rows_v, out_hbm.at[idx_v])
        _sc_worker_loop(n // SC_WINDOW, one)

    return k(rows, idx)


def _sc_gather_rows(table, idx):
    n = idx.shape[0]
    width = table.shape[1]
    mesh = plsc.VectorSubcoreMesh(core_axis_name="c", subcore_axis_name="s")

    @functools.partial(
        pl.kernel, mesh=mesh, out_type=jax.ShapeDtypeStruct((n, width), table.dtype),
        scratch_types=[pltpu.VMEM((SC_WINDOW,), jnp.int32), pltpu.VMEM((SC_WINDOW, width), table.dtype)])
    def k(table_hbm, idx_hbm, out_hbm, idx_v, rows_v):
        def one(base):
            pltpu.sync_copy(idx_hbm.at[pl.ds(base, SC_WINDOW)], idx_v)
            pltpu.sync_copy(table_hbm.at[idx_v], rows_v)
            pltpu.sync_copy(rows_v, out_hbm.at[pl.ds(base, SC_WINDOW)])
        _sc_worker_loop(n // SC_WINDOW, one)

    return k(table, idx)


def _moe_body(e1_ref, e2_ref, na_ref, hs_ref, g1_ref, g2_ref, u1_ref, u2_ref, d1_ref, d2_ref, o_ref):
    @pl.when(pl.program_id(0) < na_ref[0])
    def _():
        blk = hs_ref[...]
        h = _unpack_bf16_pairs(blk[:, :PACK_W])
        y = None
        for lane, (g_ref, u_ref, d_ref) in enumerate(((g1_ref, u1_ref, d1_ref), (g2_ref, u2_ref, d2_ref))):
            gate = jnp.dot(h, g_ref[...], preferred_element_type=F32)
            up = jnp.dot(h, u_ref[...], preferred_element_type=F32)
            w = blk[:, PACK_W + lane:PACK_W + lane + 1]
            act = (gate * jax.nn.sigmoid(gate) * up * w).astype(BF16)
            term = jnp.dot(act, d_ref[...], preferred_element_type=F32)
            y = term if y is None else y + term
        o_ref[...] = y


def _moe_call(hs, tile_e1, tile_e2, n_active, wg, wu, wd, layer, tm):
    n_s = hs.shape[0]
    n_tiles = n_s // tm
    base = layer * N_EXPERTS

    def row_map(i, e1, e2, na):
        return (jnp.minimum(i, na[0] - 1), 0)

    def w_map(which):
        def m(i, e1, e2, na):
            e = (e1, e2)[which]
            return (base + e[jnp.minimum(i, na[0] - 1)], 0, 0)
        return m

    gu = lambda which: pl.BlockSpec((None, D_MODEL, D_EXPERT), w_map(which))
    dn = lambda which: pl.BlockSpec((None, D_EXPERT, D_MODEL), w_map(which))
    return pl.pallas_call(
        _moe_body,
        grid_spec=pltpu.PrefetchScalarGridSpec(
            num_scalar_prefetch=3, grid=(n_tiles,),
            in_specs=[pl.BlockSpec((tm, PAY_W), row_map), gu(0), gu(1), gu(0), gu(1), dn(0), dn(1)],
            out_specs=pl.BlockSpec((tm, D_MODEL), row_map)),
        out_shape=jax.ShapeDtypeStruct((n_s, D_MODEL), F32),
        compiler_params=_cparams(("arbitrary",)),
        name="grouped_experts",
    )(tile_e1, tile_e2, n_active, hs, wg, wg, wu, wu, wd, wd)


_PAIR_LO = (0, 0, 0, 1, 1, 2)
_PAIR_HI = (1, 2, 3, 2, 3, 3)


def _moe_layer(pay, info, counts, wg, wu, wd, layer):
    bsz, t, _ = pay.shape
    n = bsz * t
    tm = 256
    n_s = ((n + N_CLASSES * (tm - 1)) // tm + 1) * tm
    n_tiles = n_s // tm
    cls = info[:, 0, :].reshape(n)
    rank = info[:, 1, :].reshape(n)
    cnt = counts[:N_CLASSES, 0].astype(jnp.int32)
    padded = ((cnt + tm - 1) // tm) * tm
    ends = jnp.cumsum(padded)
    starts = ends - padded
    dest = starts[cls] + rank
    tile_start = jnp.arange(n_tiles, dtype=jnp.int32) * tm
    tile_cls = jnp.minimum(jnp.sum((tile_start[:, None] >= ends[None, :]).astype(jnp.int32), axis=1),
                           N_CLASSES - 1)
    grp = tile_cls // N_PAIRS
    pr = tile_cls % N_PAIRS
    tile_e1 = grp * EXPERTS_PER_GROUP + jnp.asarray(_PAIR_LO, jnp.int32)[pr]
    tile_e2 = grp * EXPERTS_PER_GROUP + jnp.asarray(_PAIR_HI, jnp.int32)[pr]
    n_active = (ends[-1:] // tm).astype(jnp.int32)
    hs = _sc_scatter_rows(pay.reshape(n, PAY_W), dest, n_s)
    ys = _moe_call(hs, tile_e1, tile_e2, n_active, wg, wu, wd, layer, tm)
    return _sc_gather_rows(ys, dest).reshape(bsz, t, D_MODEL)


def _trunk(x, ada, kva, hist, past, prm):
    bsz, t, _ = x.shape
    start_pos = 0 if past is None else past[0].shape[1]
    wrt, br = prm["wrt"], prm["br"]
    res = None
    new_hist = []
    for layer in range(N_A_LAYERS):
        if hist is None:
            h16 = jnp.zeros((bsz, HIST_ROWS, D_MODEL), F32)
        else:
            h16 = jnp.pad(hist[layer], ((0, 0), (1, 0), (0, 0)))
        x, pay, info, counts, hout = _mixer_call(
            x, res, h16, ada[layer], prm["norm_g"][layer], prm["w_pool"][layer], prm["pool_scale"][layer],
            wrt, br, start_pos)
        new_hist.append(hout)
        y = _moe_layer(pay, info, counts, prm["w_gate"], prm["w_up"], prm["w_down"], layer)
        res = (y, ada[layer][:, 5:6, :])

    shared = None
    for j in range(N_B_LAYERS):
        layer = N_A_LAYERS + j
        y, g2p = res
        if j == 0:
            kv = (kva, prm["kv_norm"], prm["w_kv"], prm["w_f"], prm["b_f"], prm["k_norm"])
            xr, q, k, v, kb, vt, logf, lfw = _proj_call(
                x, y, g2p, ada[layer], prm["norm_g"][layer], prm["w_q"][j], prm["q_norm"][j],
                prm["s_mat"], prm["st_mat"], kv)
            if past is None:
                past_kv = None
                fp = _decay_call(lfw)
            else:
                ck, cv, clogf = past
                n_past = ck.shape[1]
                pad = -(n_past + t) % 512
                lfw = jnp.concatenate([jnp.pad(clogf, ((0, 0), (0, 0), (0, LANES - N_HEADS))), lfw,
                                       jnp.zeros((bsz, pad, LANES), F32)], axis=1)
                fp = _decay_call(lfw)
                past_kv = (ck.reshape(bsz, n_past, D_MODEL), _transpose_call(cv.reshape(bsz, n_past, D_MODEL)))
            shared = (k, v, logf, kb, vt, past_kv, fp)
        else:
            xr, q = _proj_call(x, y, g2p, ada[layer], prm["norm_g"][layer], prm["w_q"][j], prm["q_norm"][j],
                               prm["s_mat"], prm["st_mat"])
        o = _flash_call(q, shared[3], shared[4], shared[5], shared[6], start_pos)
        x, pay, info, counts = _oproj_call(xr, o, prm["w_o"][j], ada[layer], prm["norm_g"][layer], wrt, br)
        y = _moe_layer(pay, info, counts, prm["w_gate"], prm["w_up"], prm["w_down"], layer)
        res = (y, ada[layer][:, 5:6, :])

    out = _final_call(x, res[0], res[1])
    k, v, logf = shared[:3]
    return (out, jnp.stack(new_hist), k.reshape(bsz, t, N_HEADS, HEAD_DIM),
            v.reshape(bsz, t, N_HEADS, HEAD_DIM), logf)


def kernel(x_prompt, x_sample, cache_pool, cache_k, cache_v, cache_logf, c_prompt, c_sample, ada_w, ada_b, norm_g, w_pool, pool_scale, kv_ada_w, kv_ada_b, kv_norm, w_kvf, b_f, k_norm, w_q, q_norm, w_o, w_router, b_router, w_gate, w_up, w_down):
    bp = x_prompt.shape[0]
    c_all = jnp.concatenate([c_prompt, c_sample], axis=0)
    bc = c_all.shape[0]
    ada = _ada_call(c_all, ada_w, ada_b).reshape(DEPTH, bc, 6, D_MODEL)
    kva = _ada_call(c_all, kv_ada_w[None], kv_ada_b[None]).reshape(bc, 2, D_MODEL)

    head_of_lane = jnp.arange(D_MODEL, dtype=jnp.int32) // HEAD_DIM
    s_mat = (head_of_lane[:, None] == jnp.arange(LANES, dtype=jnp.int32)[None, :]).astype(BF16)
    prm = {
        "norm_g": norm_g,
        "w_pool": w_pool.astype(BF16),
        "pool_scale": pool_scale.reshape(N_A_LAYERS, 1, D_MODEL),
        "kv_norm": kv_norm.reshape(1, D_MODEL),
        "w_kv": w_kvf[:, :2 * KV_WIDTH].astype(BF16),
        "w_f": jnp.pad(w_kvf[:, 2 * KV_WIDTH:], ((0, 0), (0, LANES - N_HEADS))),
        "b_f": jnp.pad(b_f, (0, LANES - N_HEADS)).reshape(1, LANES),
        "k_norm": jnp.tile(k_norm, N_HEADS).reshape(1, D_MODEL),
        "w_q": w_q.astype(BF16),
        "q_norm": jnp.tile(q_norm, (1, N_HEADS)).reshape(N_B_LAYERS, 1, D_MODEL),
        "w_o": w_o.astype(BF16),
        "wrt": w_router.T,
        "br": b_router.reshape(N_EXPERTS, 1),
        "w_gate": w_gate.astype(BF16).reshape(DEPTH * N_EXPERTS, D_MODEL, D_EXPERT),
        "w_up": w_up.astype(BF16).reshape(DEPTH * N_EXPERTS, D_MODEL, D_EXPERT),
        "w_down": w_down.astype(BF16).reshape(DEPTH * N_EXPERTS, D_EXPERT, D_MODEL),
        "s_mat": s_mat,
        "st_mat": s_mat.T,
    }
    outs_p = _trunk(x_prompt, ada[:, :bp], kva[:bp], None, None, prm)
    outs_s = _trunk(x_sample, ada[:, bp:], kva[bp:], cache_pool, (cache_k, cache_v, cache_logf), prm)
    return (outs_p[0], outs_s[0]) + outs_p[1:] + outs_s[1:]
```

```python
import functools

import jax
import jax.numpy as jnp
from jax import lax
from jax.experimental import pallas as pl
from jax.experimental.pallas import tpu as pltpu
from jax.experimental.pallas import tpu_sc as plsc

F32 = jnp.float32
BF16 = jnp.bfloat16

D_MODEL = 1024
DEPTH = 4
N_A_LAYERS = DEPTH // 2
N_B_LAYERS = DEPTH - N_A_LAYERS
POOL_WINDOWS = (2, 4, 8, 16)
N_POOL_GROUPS = len(POOL_WINDOWS)
POOL_GROUP_DIM = D_MODEL // N_POOL_GROUPS
POOL_BUF = max(POOL_WINDOWS) - 1
HIST_ROWS = POOL_BUF + 1
N_HEADS = 16
HEAD_DIM = D_MODEL // N_HEADS
KV_WIDTH = N_HEADS * HEAD_DIM
ATTN_SCALE = HEAD_DIM ** -0.5
N_EXPERTS = 16
N_EXPERT_GROUPS = 4
EXPERTS_PER_GROUP = N_EXPERTS // N_EXPERT_GROUPS
N_PAIRS = 6
N_CLASSES = N_EXPERT_GROUPS * N_PAIRS
D_EXPERT = D_MODEL // 2
EPS = 1e-6
NEG_INF = -1e30

LANES = 128
MXU_COLS = 256
SC_CORES = 2
SC_SUBCORES = 16
SC_WINDOW = 64
PACK_W = D_MODEL // 2
PAY_W = PACK_W + LANES
PAD_W = N_HEADS * LANES
LOG2E = 1.4426950408889634
DECAY_PIECES = 3
SUM_ROWS = 16
CLS_ROWS = 32
VMEM_LIMIT = 48 * 1024 * 1024


def _cparams(sem, flags=None):
    return pltpu.CompilerParams(dimension_semantics=sem, vmem_limit_bytes=VMEM_LIMIT, flags=flags)


def _bdot(a, b):
    return jnp.dot(a.astype(BF16), b.astype(BF16), preferred_element_type=F32)


def _split(a):
    hi = a.astype(BF16)
    lo = (a - hi.astype(F32)).astype(BF16)
    return hi, lo


_NN = (((1,), (0,)), ((), ()))
_NT = (((1,), (1,)), ((), ()))


def _dot3(a, b, dims=_NN):
    ah, al = _split(a)
    bh, bl = _split(b)
    d = lambda x, y: lax.dot_general(x, y, dims, preferred_element_type=F32)
    return d(ah, bh) + (d(ah, bl) + d(al, bh))


def _dot2_exact_rhs(a, b_bf16):
    ah, al = _split(a)
    return (jnp.dot(ah, b_bf16, preferred_element_type=F32)
            + jnp.dot(al, b_bf16, preferred_element_type=F32))


def _rms_mod(x, gain, shift, scale):
    ms = jnp.mean(x * x, axis=-1, keepdims=True)
    return (x * lax.rsqrt(ms + EPS)) * (gain * (1.0 + scale)) + shift


def _head_rms(z, s_ref, st_ref, gain):
    ss = jnp.dot((z * z).astype(BF16), s_ref[...], preferred_element_type=F32)
    inv = lax.rsqrt(ss * (1.0 / HEAD_DIM) + EPS)
    invf = _dot2_exact_rhs(inv, st_ref[...])
    return z * invf * gain


def _query_blocks(q):
    lane = lax.broadcasted_iota(jnp.int32, (1, LANES), 1)
    blocks = []
    for h in range(N_HEADS):
        pair = q[:, (h // 2) * LANES:(h // 2 + 1) * LANES]
        in_head = (lane >= HEAD_DIM) if h % 2 else (lane < HEAD_DIM)
        blocks.append(jnp.where(in_head, pair, 0.0))
    return blocks


_HI_MASK = 0xFFFF0000


def _pack_bf16_pairs(x):
    bits = lambda a: lax.bitcast_convert_type(a.astype(BF16).astype(F32), jnp.uint32)
    half = x.shape[1] // 2
    word = (bits(x[:, :half]) >> 16) | (bits(x[:, half:]) & jnp.uint32(_HI_MASK))
    return lax.bitcast_convert_type(word, F32)


def _unpack_bf16_pairs(w):
    word = lax.bitcast_convert_type(w, jnp.uint32)
    lo = lax.bitcast_convert_type(word << 16, F32).astype(BF16)
    hi = lax.bitcast_convert_type(word & jnp.uint32(_HI_MASK), F32).astype(BF16)
    return jnp.concatenate([lo, hi], axis=1)


def _pad_rows(a, rows):
    if a.shape[0] == rows:
        return a
    return jnp.concatenate([a, jnp.zeros((rows - a.shape[0], a.shape[1]), a.dtype)], axis=0)


def _route(lt, br):
    m = jnp.max(lt, axis=0, keepdims=True)
    p = jnp.exp(lt - m)
    scores = p / jnp.sum(p, axis=0, keepdims=True)
    sel = scores + br
    row = lambda a, e: a[e:e + 1, :]
    gs = []
    for g in range(N_EXPERT_GROUPS):
        v = [row(sel, g * EXPERTS_PER_GROUP + j) for j in range(EXPERTS_PER_GROUP)]
        best = None
        for i in range(EXPERTS_PER_GROUP):
            for j in range(i + 1, EXPERTS_PER_GROUP):
                s = v[i] + v[j]
                best = s if best is None else jnp.maximum(best, s)
        gs.append(best)
    bg = jnp.zeros_like(gs[0])
    bv = gs[0]
    for g in range(1, N_EXPERT_GROUPS):
        better = gs[g] > bv
        bg = jnp.where(better, float(g), bg)
        bv = jnp.where(better, gs[g], bv)

    def in_group(a, j):
        out = row(a, j)
        for g in range(1, N_EXPERT_GROUPS):
            out = jnp.where(bg == float(g), row(a, g * EXPERTS_PER_GROUP + j), out)
        return out

    sg = [in_group(sel, j) for j in range(EXPERTS_PER_GROUP)]
    cg = [in_group(scores, j) for j in range(EXPERTS_PER_GROUP)]

    def first_argmax(vals):
        mx = vals[0]
        for v in vals[1:]:
            mx = jnp.maximum(mx, v)
        idx = jnp.full_like(mx, float(len(vals) - 1))
        for j in range(len(vals) - 2, -1, -1):
            idx = jnp.where(vals[j] == mx, float(j), idx)
        return idx

    i1 = first_argmax(sg)
    i2 = first_argmax([jnp.where(i1 == float(j), -jnp.inf, sg[j]) for j in range(EXPERTS_PER_GROUP)])
    lo = jnp.minimum(i1, i2)
    hi = jnp.maximum(i1, i2)

    def pick(vals, idx):
        out = vals[0]
        for j in range(1, len(vals)):
            out = jnp.where(idx == float(j), vals[j], out)
        return out

    c_lo = pick(cg, lo)
    c_hi = pick(cg, hi)
    tot = c_lo + c_hi
    pair = jnp.where(lo == 0.0, hi - 1.0, jnp.where(lo == 1.0, hi + 1.0, 5.0))
    return bg * float(N_PAIRS) + pair, c_lo / tot, c_hi / tot


def _moe_prep(x1, ng2, sh2, sc2, wrt_ref, br_ref, run_ref, pay_ref, info_ref, cnt_ref):
    tm = x1.shape[0]
    tr = max(tm, LANES)
    h2 = _rms_mod(x1, ng2, sh2, sc2)
    lt = _dot3(wrt_ref[...], _pad_rows(h2, tr), _NT)
    cls, w_lo, w_hi = _route(lt, br_ref[...])

    r = lax.broadcasted_iota(jnp.int32, (LANES, tr), 0)
    wrows = jnp.where(r == 0, w_lo, jnp.where(r == 1, w_hi, 0.0))
    pay_ref[:, :PACK_W] = _pack_bf16_pairs(h2)
    pay_ref[:, PACK_W:] = wrows.T[:tm, :]

    crow = lax.broadcasted_iota(jnp.int32, (CLS_ROWS, tr), 0).astype(F32)
    lane = lax.broadcasted_iota(jnp.int32, (CLS_ROWS, tr), 1)
    onehot = jnp.where((crow == cls) & (lane < tm), 1.0, 0.0)
    us = lax.broadcasted_iota(jnp.int32, (tr, tr), 0)
    ut = lax.broadcasted_iota(jnp.int32, (tr, tr), 1)
    upper = jnp.where(us < ut, 1.0, 0.0).astype(BF16)
    before = jnp.dot(onehot.astype(BF16), upper, preferred_element_type=F32) + run_ref[:, 0:1]
    rank = jnp.sum(onehot * before, axis=0, keepdims=True)
    run_new = run_ref[...] + jnp.sum(onehot, axis=1, keepdims=True)
    run_ref[...] = run_new
    cnt_ref[...] = run_new
    ir = lax.broadcasted_iota(jnp.int32, (8, tr), 0)
    info = jnp.where(ir == 0, cls, jnp.where(ir == 1, rank, 0.0)).astype(jnp.int32)
    info_ref[...] = info[:, :tm]


def _ada_body(c_ref, w_ref, b_ref, o_ref):
    c = c_ref[...]
    o_ref[...] = _dot3(c * jax.nn.sigmoid(c), w_ref[...]) + b_ref[...]


def _ada_call(c_all, w, b):
    n_l, _, n_out = w.shape
    bc = c_all.shape[0]
    tn = 1536 if n_out % 1536 == 0 else 1024
    return pl.pallas_call(
        _ada_body,
        grid=(n_l, n_out // tn),
        in_specs=[pl.BlockSpec((bc, D_MODEL), lambda l, j: (0, 0)),
                  pl.BlockSpec((None, D_MODEL, tn), lambda l, j: (l, 0, j)),
                  pl.BlockSpec((None, 1, tn), lambda l, j: (l, 0, j))],
        out_specs=pl.BlockSpec((None, bc, tn), lambda l, j: (l, 0, j)),
        out_shape=jax.ShapeDtypeStruct((n_l, bc, n_out), F32),
        compiler_params=_cparams(("arbitrary", "arbitrary")),
        name="adaln",
    )(c_all, w, b.reshape(n_l, 1, n_out))


def _tile_rows(t, rows=512):
    return rows if t % rows == 0 else t


def _tok_spec(tm, width=D_MODEL):
    return pl.BlockSpec((None, tm, width), lambda b, t: (b, t, 0))


def _per_batch_spec(rows, width=D_MODEL):
    return pl.BlockSpec((None, rows, width), lambda b, t: (b, 0, 0))


def _const_spec(shape):
    nd = len(shape)
    return pl.BlockSpec(shape, lambda b, t: (0,) * nd)


def _prep_out(bsz, t, tm):
    shapes = [jax.ShapeDtypeStruct((bsz, t, PAY_W), F32),
              jax.ShapeDtypeStruct((bsz, 8, t), jnp.int32),
              jax.ShapeDtypeStruct((CLS_ROWS, LANES), F32)]
    specs = [_tok_spec(tm, PAY_W),
             pl.BlockSpec((None, 8, tm), lambda b, t: (b, 0, t)),
             _const_spec((CLS_ROWS, LANES))]
    return shapes, specs


def _mixer_body(has_res, tm, start_pos, *refs):
    it = iter(refs)
    x_ref, xp_ref = next(it), next(it)
    if has_res:
        y_ref, yp_ref, g2p_ref = next(it), next(it), next(it)
    hist_ref, ada_ref, ng_ref, wp_ref, ps_ref, wrt_ref, br_ref = (next(it) for _ in range(7))
    x1_ref, pay_ref, info_ref, cnt_ref, hout_ref = (next(it) for _ in range(5))
    run_ref = next(it)

    b = pl.program_id(0)
    t = pl.program_id(1)

    @pl.when((b == 0) & (t == 0))
    def _():
        run_ref[...] = jnp.zeros_like(run_ref)

    xin = x_ref[...]
    xp = xp_ref[...]
    if has_res:
        g2p = g2p_ref[...]
        xin = xin + g2p * y_ref[...]
        xp = xp + g2p * yp_ref[...]
    ada = ada_ref[...]
    sh1, sc1, g1, sh2, sc2 = (ada[i:i + 1] for i in range(5))
    ng = ng_ref[...]
    u = _rms_mod(xin, ng[0:1], sh1, sc1)
    up = _rms_mod(xp, ng[0:1], sh1, sc1)
    up = jnp.where(t == 0, hist_ref[...], up)
    level = jnp.concatenate([up, u], axis=0)
    sums = []
    for g, w in enumerate(POOL_WINDOWS):
        level = level + pltpu.roll(level, w // 2, 0)
        sums.append(level[HIST_ROWS:, :POOL_GROUP_DIM])
        if g + 1 < N_POOL_GROUPS:
            level = level[:, POOL_GROUP_DIM:]

    pos = start_pos + t * tm + lax.broadcasted_iota(jnp.int32, (tm, 1), 0)
    cols = []
    for g, w in enumerate(POOL_WINDOWS):
        sl = slice(g * POOL_GROUP_DIM, (g + 1) * POOL_GROUP_DIM)
        cnt = jnp.minimum(pos + 1, w).astype(F32)
        cols.append(_bdot(sums[g] / cnt - u[:, sl], wp_ref[g]))
    x1 = xin + (g1 * ps_ref[...]) * jnp.concatenate(cols, axis=1)
    x1_ref[...] = x1
    hout_ref[...] = u[tm - POOL_BUF:, :]
    _moe_prep(x1, ng[1:2], sh2, sc2, wrt_ref, br_ref, run_ref, pay_ref, info_ref, cnt_ref)


def _mixer_call(x, res, hist16, ada_l, ng_l, wp_l, ps_l, wrt, br, start_pos):
    bsz, t, _ = x.shape
    tm = _tile_rows(t)
    prev_spec = pl.BlockSpec((None, HIST_ROWS, D_MODEL),
                             lambda b, i: (b, jnp.maximum(i * (tm // HIST_ROWS) - 1, 0), 0))
    ins = [x, x]
    specs = [_tok_spec(tm), prev_spec]
    if res is not None:
        y, g2p = res
        ins += [y, y, g2p]
        specs += [_tok_spec(tm), prev_spec, _per_batch_spec(1)]
    ins += [hist16, ada_l, ng_l, wp_l, ps_l, wrt, br]
    specs += [_per_batch_spec(HIST_ROWS), _per_batch_spec(6), _const_spec((2, D_MODEL)),
              _const_spec((N_POOL_GROUPS, POOL_GROUP_DIM, POOL_GROUP_DIM)), _const_spec((1, D_MODEL)),
              _const_spec((N_EXPERTS, D_MODEL)), _const_spec((N_EXPERTS, 1))]
    p_shapes, p_specs = _prep_out(bsz, t, tm)
    return pl.pallas_call(
        functools.partial(_mixer_body, res is not None, tm, start_pos),
        grid=(bsz, t // tm),
        in_specs=specs,
        out_specs=[_tok_spec(tm)] + p_specs + [_per_batch_spec(POOL_BUF)],
        out_shape=[jax.ShapeDtypeStruct((bsz, t, D_MODEL), F32)] + p_shapes
                  + [jax.ShapeDtypeStruct((bsz, POOL_BUF, D_MODEL), F32)],
        scratch_shapes=[pltpu.VMEM((CLS_ROWS, LANES), F32)],
        compiler_params=_cparams(("arbitrary", "arbitrary")),
        name="pool_mixer",
    )(*ins)


def _proj_body(with_kv, tm, *refs):
    it = iter(refs)
    x1p_ref, y_ref, g2p_ref, ada_ref, ng_ref, wq_ref, qn_ref, s_ref, st_ref = (next(it) for _ in range(9))
    if with_kv:
        kva_ref, kvn_ref, wkv_ref, wf_ref, bf_ref, kn_ref = (next(it) for _ in range(6))
    x_ref, q_ref = next(it), next(it)
    if with_kv:
        k_ref, v_ref, kb_ref, vt_ref, lf_ref, lfw_ref = (next(it) for _ in range(6))

    x = x1p_ref[...] + g2p_ref[...] * y_ref[...]
    x_ref[...] = x
    ada = ada_ref[...]
    ng = ng_ref[...]
    h = _rms_mod(x, ng[0:1], ada[0:1], ada[1:2])
    q = _head_rms(_bdot(h, wq_ref[...]), s_ref, st_ref, qn_ref[...])
    rows = max(tm, LANES)
    for i, blk in enumerate(_query_blocks(q * (ATTN_SCALE * LOG2E))):
        q_ref[i * LANES:(i + 1) * LANES, :] = _pad_rows(blk, rows).T[:, :tm].astype(BF16)
    if with_kv:
        kva = kva_ref[...]
        hk = _rms_mod(x, kvn_ref[...], kva[0:1], kva[1:2])
        proj = _bdot(hk, wkv_ref[...])
        k = _head_rms(proj[:, :KV_WIDTH], s_ref, st_ref, kn_ref[...])
        v = proj[:, KV_WIDTH:]
        k_ref[...] = k
        v_ref[...] = v
        kb_ref[...] = k.astype(BF16)
        vt_ref[...] = _pad_rows(v, max(tm, LANES)).T[:, :tm].astype(BF16)
        z = _dot3(hk, wf_ref[...]) + bf_ref[...]
        lf = jnp.minimum(z, 0.0) - jnp.log(1.0 + jnp.exp(-jnp.abs(z)))
        lf_ref[...] = lf[:, :N_HEADS]
        lane = lax.broadcasted_iota(jnp.int32, (1, LANES), 1)
        lfw_ref[...] = jnp.where(lane < N_HEADS, lf, 0.0)


def _proj_call(x1p, y, g2p, ada_l, ng_l, wq, qn, s_mat, st_mat, kv=None):
    bsz, t, _ = x1p.shape
    tm = _tile_rows(t, 256)
    ins = [x1p, y, g2p, ada_l, ng_l, wq, qn, s_mat, st_mat]
    specs = [_tok_spec(tm), _tok_spec(tm), _per_batch_spec(1), _per_batch_spec(6), _const_spec((2, D_MODEL)),
             _const_spec((D_MODEL, KV_WIDTH)), _const_spec((1, D_MODEL)),
             _const_spec((D_MODEL, LANES)), _const_spec((LANES, D_MODEL))]
    out_shapes = [jax.ShapeDtypeStruct((bsz, t, D_MODEL), F32), jax.ShapeDtypeStruct((bsz, PAD_W, t), BF16)]
    out_specs = [_tok_spec(tm), pl.BlockSpec((None, PAD_W, tm), lambda b, i: (b, 0, i))]
    if kv is not None:
        kva, kvn, wkv, wf, bf, kn = kv
        ins += [kva, kvn, wkv, wf, bf, kn]
        specs += [_per_batch_spec(2), _const_spec((1, D_MODEL)), _const_spec((D_MODEL, 2 * KV_WIDTH)),
                  _const_spec((D_MODEL, LANES)), _const_spec((1, LANES)), _const_spec((1, D_MODEL))]
        out_shapes += [jax.ShapeDtypeStruct((bsz, t, D_MODEL), F32)] * 2
        out_shapes += [jax.ShapeDtypeStruct((bsz, t, D_MODEL), BF16), jax.ShapeDtypeStruct((bsz, D_MODEL, t), BF16)]
        out_shapes += [jax.ShapeDtypeStruct((bsz, t, N_HEADS), F32), jax.ShapeDtypeStruct((bsz, t, LANES), F32)]
        out_specs += [_tok_spec(tm)] * 2
        out_specs += [_tok_spec(tm), pl.BlockSpec((None, D_MODEL, tm), lambda b, i: (b, 0, i))]
        out_specs += [_tok_spec(tm, N_HEADS), _tok_spec(tm, LANES)]
    return pl.pallas_call(
        functools.partial(_proj_body, kv is not None, tm),
        grid=(bsz, t // tm),
        in_specs=specs, out_specs=out_specs, out_shape=out_shapes,
        compiler_params=_cparams(("arbitrary", "arbitrary")),
        name="qkv_proj" if kv is not None else "q_proj",
    )(*ins)


def _decay_body(tc, nb, lf_ref, o_ref, carry_ref):
    @pl.when(pl.program_id(1) == 0)
    def _():
        carry_ref[...] = jnp.zeros_like(carry_ref)

    lf = jnp.concatenate([lf_ref[i] for i in range(nb)], axis=1)
    r = lax.broadcasted_iota(jnp.int32, (tc, tc), 0)
    c = lax.broadcasted_iota(jnp.int32, (tc, tc), 1)
    lower = jnp.where(r >= c, 1.0, 0.0).astype(BF16)
    hi, lo = _split(lf)
    f = (jnp.dot(lower, hi, preferred_element_type=F32) + jnp.dot(lower, lo, preferred_element_type=F32)
         + carry_ref[0:1, :])
    carry_ref[...] = jnp.broadcast_to(f[tc - 1:tc, :], carry_ref.shape)
    bias = f * (-LOG2E)
    p1 = bias.astype(BF16)
    r1 = bias - p1.astype(F32)
    p2 = r1.astype(BF16)
    p3 = (r1 - p2.astype(F32)).astype(BF16)
    hr = lax.broadcasted_iota(jnp.int32, (LANES, LANES), 0)
    lc = lax.broadcasted_iota(jnp.int32, (LANES, LANES), 1)
    places = [jnp.where((lc == DECAY_PIECES * hr + i) & (hr < N_HEADS), 1.0, 0.0).astype(BF16)
              for i in range(DECAY_PIECES)]
    for b in range(nb):
        out = None
        for p, place in zip((p1, p2, p3), places):
            term = jnp.dot(p[:, b * LANES:(b + 1) * LANES], place, preferred_element_type=F32)
            out = term if out is None else out + term
        o_ref[b] = out.astype(BF16)


def _decay_call(lfw):
    bsz, tk, _ = lfw.shape
    tc = 512
    nb = 8 if bsz % 8 == 0 else 1
    spec = pl.BlockSpec((nb, tc, LANES), lambda b, t: (b, t, 0))
    return pl.pallas_call(
        functools.partial(_decay_body, tc, nb),
        grid=(bsz // nb, tk // tc), in_specs=[spec], out_specs=spec,
        out_shape=jax.ShapeDtypeStruct(lfw.shape, BF16),
        scratch_shapes=[pltpu.VMEM((8, nb * LANES), F32)],
        compiler_params=_cparams(("arbitrary", "arbitrary")),
        name="decay_bias",
    )(lfw)


def _transpose_body(v_ref, o_ref):
    o_ref[...] = v_ref[...].T.astype(BF16)


def _transpose_call(v):
    bsz, p, _ = v.shape
    tp = 512
    return pl.pallas_call(
        _transpose_body, grid=(bsz, p // tp),
        in_specs=[pl.BlockSpec((None, tp, D_MODEL), lambda b, t: (b, t, 0))],
        out_specs=pl.BlockSpec((None, D_MODEL, tp), lambda b, t: (b, 0, t)),
        out_shape=jax.ShapeDtypeStruct((bsz, D_MODEL, p), BF16),
        compiler_params=_cparams(("arbitrary", "arbitrary")),
        name="value_transpose",
    )(v)


def _flash_body(tq, tkp, tks, n_past, n_self, *refs):
    it = iter(refs)
    q_ref = next(it)
    if n_past:
        kp_ref, vtp_ref, fpp_ref = next(it), next(it), next(it)
    ks_ref, vts_ref, fps_ref = next(it), next(it), next(it)
    o_ref, m_ref, l_ref, acc_ref = next(it), next(it), next(it), next(it)
    qi = pl.program_id(1)
    step = pl.program_id(2)

    @pl.when(step == 0)
    def _():
        m_ref[...] = jnp.full_like(m_ref, NEG_INF)
        l_ref[...] = jnp.zeros_like(l_ref)
        acc_ref[...] = jnp.zeros_like(acc_ref)

    def process(k_ref, vt_ref, fp_ref, tk, key_base):
        fp = fp_ref[...]
        piece_row = lax.broadcasted_iota(jnp.int32, (LANES, 1), 0)
        if key_base is not None:
            kpos = key_base + lax.broadcasted_iota(jnp.int32, (tk, 1), 0)
            qpos = qi * tq + lax.broadcasted_iota(jnp.int32, (1, tq), 1)
            visible = kpos <= qpos

        pair_keys = {}

        def logits(h):
            d = piece_row - DECAY_PIECES * h
            ones = jnp.where((d >= 0) & (d < DECAY_PIECES), 1.0, 0.0).astype(BF16)
            hp = h // 2
            if hp not in pair_keys:
                pair_keys[hp] = k_ref[:, hp * LANES:(hp + 1) * LANES].astype(BF16)
            lhs = jnp.concatenate([pair_keys[hp], fp], axis=1)
            rhs = jnp.concatenate([q_ref[h * LANES:(h + 1) * LANES, :],
                                   jnp.broadcast_to(ones, (LANES, tq))], axis=0)
            if tq < 2 * MXU_COLS and tk % 2 == 0:
                half = tk // 2
                s = jnp.concatenate([jnp.dot(lhs[:half], rhs, preferred_element_type=F32),
                                     jnp.dot(lhs[half:], rhs, preferred_element_type=F32)], axis=0)
            else:
                s = jnp.dot(lhs, rhs, preferred_element_type=F32)
            return s if key_base is None else jnp.where(visible, s, NEG_INF)

        ones_rows = jnp.ones((SUM_ROWS, tk), BF16)
        s_next = logits(0)
        for h in range(N_HEADS):
            s = s_next
            if h + 1 < N_HEADS:
                s_next = logits(h + 1)
            rows = slice(h * HEAD_DIM, (h + 1) * HEAD_DIM)
            m_old = m_ref[h]
            m_new = jnp.maximum(m_old, jnp.max(s, axis=0, keepdims=True))
            alpha = jnp.exp2(m_old - m_new)
            p = jnp.exp2(s - m_new).astype(BF16)
            m_ref[h] = m_new
            pv = jnp.dot(jnp.concatenate([vt_ref[rows, :], ones_rows], axis=0), p,
                         preferred_element_type=F32)
            l_ref[h] = alpha * l_ref[h] + pv[HEAD_DIM:HEAD_DIM + 1, :]
            acc_ref[rows, :] = alpha * acc_ref[rows, :] + pv[:HEAD_DIM, :]

    if n_past:
        @pl.when(step < n_past)
        def _():
            process(kp_ref, vtp_ref, fpp_ref, tkp, None)

    j = step - n_past
    last = (qi * tq + tq - 1) // tks
    has_hidden = (j + 1) * tks - 1 > qi * tq

    @pl.when((j >= 0) & (j <= last) & has_hidden)
    def _():
        process(ks_ref, vts_ref, fps_ref, tks, j * tks)

    @pl.when((j >= 0) & (j <= last) & jnp.logical_not(has_hidden))
    def _():
        process(ks_ref, vts_ref, fps_ref, tks, None)

    @pl.when(step == n_past + n_self - 1)
    def _():
        r = lax.broadcasted_iota(jnp.int32, (tq, tq), 0)
        c = lax.broadcasted_iota(jnp.int32, (tq, tq), 1)
        eye = jnp.where(r == c, 1.0, 0.0).astype(BF16)
        row = lax.broadcasted_iota(jnp.int32, (LANES, 1), 0)
        for hp in range(N_HEADS // 2):
            denom = jnp.where(row < HEAD_DIM, l_ref[2 * hp], l_ref[2 * hp + 1])
            o_t = (acc_ref[hp * LANES:(hp + 1) * LANES, :] / denom).astype(BF16)
            o_ref[:, hp * LANES:(hp + 1) * LANES] = lax.dot_general(
                eye, o_t, _NT, preferred_element_type=F32).astype(BF16)


def _flash_call(q, k_self, vt_self, past, fp, n_past_keys):
    bsz, _, t_q = q.shape
    tq = 512 if t_q % 512 == 0 else t_q
    tks = 512 if t_q % 512 == 0 else t_q
    n_self = t_q // tks
    tkp = 512
    n_past = n_past_keys // tkp
    fp_self_base = n_past_keys // tks

    def self_idx(i, s):
        return jnp.clip(s - n_past, 0, (i * tq + tq - 1) // tks)

    ins = [q]
    specs = [pl.BlockSpec((None, PAD_W, tq), lambda b, i, s: (b, 0, i))]
    if n_past:
        past_idx = lambda s: jnp.minimum(s, n_past - 1)
        ins += [past[0], past[1], fp]
        specs += [pl.BlockSpec((None, tkp, D_MODEL), lambda b, i, s: (b, past_idx(s), 0)),
                  pl.BlockSpec((None, D_MODEL, tkp), lambda b, i, s: (b, 0, past_idx(s))),
                  pl.BlockSpec((None, tkp, LANES), lambda b, i, s: (b, past_idx(s), 0))]
    ins += [k_self, vt_self, fp]
    specs += [pl.BlockSpec((None, tks, D_MODEL), lambda b, i, s: (b, self_idx(i, s), 0)),
              pl.BlockSpec((None, D_MODEL, tks), lambda b, i, s: (b, 0, self_idx(i, s))),
              pl.BlockSpec((None, tks, LANES), lambda b, i, s: (b, fp_self_base + self_idx(i, s), 0))]
    return pl.pallas_call(
        functools.partial(_flash_body, tq, tkp, tks, n_past, n_self),
        grid=(bsz, t_q // tq, n_past + n_self),
        in_specs=specs,
        out_specs=pl.BlockSpec((None, tq, D_MODEL), lambda b, i, s: (b, i, 0)),
        out_shape=jax.ShapeDtypeStruct((bsz, t_q, D_MODEL), BF16),
        scratch_shapes=[pltpu.VMEM((N_HEADS, 1, tq), F32), pltpu.VMEM((N_HEADS, 1, tq), F32),
                        pltpu.VMEM((D_MODEL, tq), F32)],
        compiler_params=_cparams(("arbitrary", "arbitrary", "arbitrary")),
        name="fox_attention",
    )(*ins)


def _oproj_body(x_ref, o_ref, wo_ref, ada_ref, ng_ref, wrt_ref, br_ref,
                x1_ref, pay_ref, info_ref, cnt_ref, run_ref):
    @pl.when((pl.program_id(0) == 0) & (pl.program_id(1) == 0))
    def _():
        run_ref[...] = jnp.zeros_like(run_ref)

    ada = ada_ref[...]
    x1 = x_ref[...] + ada[2:3] * jnp.dot(o_ref[...], wo_ref[...], preferred_element_type=F32)
    x1_ref[...] = x1
    _moe_prep(x1, ng_ref[...][1:2], ada[3:4], ada[4:5], wrt_ref, br_ref, run_ref, pay_ref, info_ref, cnt_ref)


def _oproj_call(x, o, wo, ada_l, ng_l, wrt, br):
    bsz, t, _ = x.shape
    tm = _tile_rows(t)
    p_shapes, p_specs = _prep_out(bsz, t, tm)
    return pl.pallas_call(
        _oproj_body,
        grid=(bsz, t // tm),
        in_specs=[_tok_spec(tm), _tok_spec(tm), _const_spec((KV_WIDTH, D_MODEL)), _per_batch_spec(6),
                  _const_spec((2, D_MODEL)), _const_spec((N_EXPERTS, D_MODEL)), _const_spec((N_EXPERTS, 1))],
        out_specs=[_tok_spec(tm)] + p_specs,
        out_shape=[jax.ShapeDtypeStruct((bsz, t, D_MODEL), F32)] + p_shapes,
        scratch_shapes=[pltpu.VMEM((CLS_ROWS, LANES), F32)],
        compiler_params=_cparams(("arbitrary", "arbitrary")),
        name="attn_out_proj",
    )(x, o, wo, ada_l, ng_l, wrt, br)


def _final_body(x_ref, y_ref, g_ref, o_ref):
    o_ref[...] = x_ref[...] + g_ref[...] * y_ref[...]


def _final_call(x1, y, g2):
    bsz, t, _ = x1.shape
    tm = _tile_rows(t, 1024)
    return pl.pallas_call(
        _final_body, grid=(bsz, t // tm),
        in_specs=[_tok_spec(tm), _tok_spec(tm), _per_batch_spec(1)],
        out_specs=_tok_spec(tm),
        out_shape=jax.ShapeDtypeStruct(x1.shape, F32),
        compiler_params=_cparams(("arbitrary", "arbitrary")),
        name="final_residual",
    )(x1, y, g2)


def _sc_worker_loop(n_win, fn):
    wid = lax.axis_index("s") * SC_CORES + lax.axis_index("c")
    n_workers = SC_CORES * SC_SUBCORES

    @pl.loop(0, pl.cdiv(n_win, n_workers))
    def _(j):
        win = j * n_workers + wid

        @pl.when(win < n_win)
        def _():
            fn(pl.multiple_of(win * SC_WINDOW, SC_WINDOW))


def _sc_scatter_rows(rows, idx, n_out):
    n, width = rows.shape
    mesh = plsc.VectorSubcoreMesh(core_axis_name="c", subcore_axis_name="s")

    @functools.partial(
        pl.kernel, mesh=mesh, out_type=jax.ShapeDtypeStruct((n_out, width), rows.dtype),
        scratch_types=[pltpu.VMEM((SC_WINDOW,), jnp.int32), pltpu.VMEM((SC_WINDOW, width), rows.dtype)])
    def k(rows_hbm, idx_hbm, out_hbm, idx_v, rows_v):
        def one(base):
            pltpu.sync_copy(idx_hbm.at[pl.ds(base, SC_WINDOW)], idx_v)
            pltpu.sync_copy(rows_hbm.at[pl.ds(base, SC_WINDOW)], rows_v)
            pltpu.sync_copy(rows_v, out_hbm.at[idx_v])
        _sc_worker_loop(n // SC_WINDOW, one)

    return k(rows, idx)


def _sc_gather_rows(table, idx):
    n = idx.shape[0]
    width = table.shape[1]
    mesh = plsc.VectorSubcoreMesh(core_axis_name="c", subcore_axis_name="s")

    @functools.partial(
        pl.kernel, mesh=mesh, out_type=jax.ShapeDtypeStruct((n, width), table.dtype),
        scratch_types=[pltpu.VMEM((SC_WINDOW,), jnp.int32), pltpu.VMEM((SC_WINDOW, width), table.dtype)])
    def k(table_hbm, idx_hbm, out_hbm, idx_v, rows_v):
        def one(base):
            pltpu.sync_copy(idx_hbm.at[pl.ds(base, SC_WINDOW)], idx_v)
            pltpu.sync_copy(table_hbm.at[idx_v], rows_v)
            pltpu.sync_copy(rows_v, out_hbm.at[pl.ds(base, SC_WINDOW)])
        _sc_worker_loop(n // SC_WINDOW, one)

    return k(table, idx)


def _moe_body(e1_ref, e2_ref, na_ref, hs_ref, g1_ref, g2_ref, u1_ref, u2_ref, d1_ref, d2_ref, o_ref):
    @pl.when(pl.program_id(0) < na_ref[0])
    def _():
        blk = hs_ref[...]
        h = _unpack_bf16_pairs(blk[:, :PACK_W])
        y = None
        for lane, (g_ref, u_ref, d_ref) in enumerate(((g1_ref, u1_ref, d1_ref), (g2_ref, u2_ref, d2_ref))):
            gate = jnp.dot(h, g_ref[...], preferred_element_type=F32)
            up = jnp.dot(h, u_ref[...], preferred_element_type=F32)
            w = blk[:, PACK_W + lane:PACK_W + lane + 1]
            act = (gate * jax.nn.sigmoid(gate) * up * w).astype(BF16)
            term = jnp.dot(act, d_ref[...], preferred_element_type=F32)
            y = term if y is None else y + term
        o_ref[...] = y


def _moe_call(hs, tile_e1, tile_e2, n_active, wg, wu, wd, layer, tm):
    n_s = hs.shape[0]
    n_tiles = n_s // tm
    base = layer * N_EXPERTS

    def row_map(i, e1, e2, na):
        return (jnp.minimum(i, na[0] - 1), 0)

    def w_map(which):
        def m(i, e1, e2, na):
            e = (e1, e2)[which]
            return (base + e[jnp.minimum(i, na[0] - 1)], 0, 0)
        return m

    gu = lambda which: pl.BlockSpec((None, D_MODEL, D_EXPERT), w_map(which))
    dn = lambda which: pl.BlockSpec((None, D_EXPERT, D_MODEL), w_map(which))
    return pl.pallas_call(
        _moe_body,
        grid_spec=pltpu.PrefetchScalarGridSpec(
            num_scalar_prefetch=3, grid=(n_tiles,),
            in_specs=[pl.BlockSpec((tm, PAY_W), row_map), gu(0), gu(1), gu(0), gu(1), dn(0), dn(1)],
            out_specs=pl.BlockSpec((tm, D_MODEL), row_map)),
        out_shape=jax.ShapeDtypeStruct((n_s, D_MODEL), F32),
        compiler_params=_cparams(("arbitrary",)),
        name="grouped_experts",
    )(tile_e1, tile_e2, n_active, hs, wg, wg, wu, wu, wd, wd)


_PAIR_LO = (0, 0, 0, 1, 1, 2)
_PAIR_HI = (1, 2, 3, 2, 3, 3)


def _moe_layer(pay, info, counts, wg, wu, wd, layer):
    bsz, t, _ = pay.shape
    n = bsz * t
    tm = 256
    n_s =((n + N_CLASSES * (tm - 1)) // tm + 1) * tm
    n_tiles = n_s // tm
    cls = info[:, 0, :].reshape(n)
    rank = info[:, 1, :].reshape(n)
    cnt = counts[:N_CLASSES, 0].astype(jnp.int32)
    padded = ((cnt + tm - 1) // tm) * tm
    ends = jnp.cumsum(padded)
    starts = ends - padded
    dest = starts[cls] + rank
    tile_start = jnp.arange(n_tiles, dtype=jnp.int32) * tm
    tile_cls = jnp.minimum(jnp.sum((tile_start[:, None] >= ends[None, :]).astype(jnp.int32), axis=1),
                           N_CLASSES - 1)
    grp = tile_cls // N_PAIRS
    pr = tile_cls % N_PAIRS
    tile_e1 = grp * EXPERTS_PER_GROUP + jnp.asarray(_PAIR_LO, jnp.int32)[pr]
    tile_e2 = grp * EXPERTS_PER_GROUP + jnp.asarray(_PAIR_HI, jnp.int32)[pr]
    n_active = (ends[-1:] // tm).astype(jnp.int32)
    hs = _sc_scatter_rows(pay.reshape(n, PAY_W), dest, n_s)
    yield
    ys = _moe_call(hs, tile_e1, tile_e2, n_active, wg, wu, wd, layer, tm)
    yield
    return _sc_gather_rows(ys, dest).reshape(bsz, t, D_MODEL)


def _alternate(first, second):
    live = {0: first, 1: second}
    results = {}

    def advance(i):
        try:
            next(live[i])
        except StopIteration as done:
            results[i] = done.value
            del live[i]

    advance(0)
    while live:
        for i in (0, 1):
            if i in live:
                advance(i)
    return results[0], results[1]


def _trunk(x, ada, kva, hist, past, prm):
    bsz, t, _ = x.shape
    start_pos = 0 if past is None else past[0].shape[1]
    wrt, br = prm["wrt"], prm["br"]
    res = None
    new_hist = []
    for layer in range(N_A_LAYERS):
        if hist is None:
            h16 = jnp.zeros((bsz, HIST_ROWS, D_MODEL), F32)
        else:
            h16 = jnp.pad(hist[layer], ((0, 0), (1, 0), (0, 0)))
        x, pay, info, counts, hout = _mixer_call(
            x, res, h16, ada[layer], prm["norm_g"][layer], prm["w_pool"][layer], prm["pool_scale"][layer],
            wrt, br, start_pos)
        new_hist.append(hout)
        yield
        y = yield from _moe_layer(pay, info, counts, prm["w_gate"], prm["w_up"], prm["w_down"], layer)
        yield
        res = (y, ada[layer][:, 5:6, :])

    shared = None
    for j in range(N_B_LAYERS):
        layer = N_A_LAYERS + j
        y, g2p = res
        if j == 0:
            kv = (kva, prm["kv_norm"], prm["w_kv"], prm["w_f"], prm["b_f"], prm["k_norm"])
            xr, q, k, v, kb, vt, logf, lfw = _proj_call(
                x, y, g2p, ada[layer], prm["norm_g"][layer], prm["w_q"][j], prm["q_norm"][j],
                prm["s_mat"], prm["st_mat"], kv)
            if past is None:
                past_kv = None
                fp = _decay_call(lfw)
            else:
                ck, cv, clogf = past
                n_past = ck.shape[1]
                pad = -(n_past + t) % 512
                lfw = jnp.concatenate([jnp.pad(clogf, ((0, 0), (0, 0), (0, LANES - N_HEADS))), lfw,
                                       jnp.zeros((bsz, pad, LANES), F32)], axis=1)
                fp = _decay_call(lfw)
                past_kv = (ck.reshape(bsz, n_past, D_MODEL), _transpose_call(cv.reshape(bsz, n_past, D_MODEL)))
            shared = (k, v, logf, kb, vt, past_kv, fp)
        else:
            xr, q = _proj_call(x, y, g2p, ada[layer], prm["norm_g"][layer], prm["w_q"][j], prm["q_norm"][j],
                               prm["s_mat"], prm["st_mat"])
        yield
        o = _flash_call(q, shared[3], shared[4], shared[5], shared[6], start_pos)
        yield
        x, pay, info, counts = _oproj_call(xr, o, prm["w_o"][j], ada[layer], prm["norm_g"][layer], wrt, br)
        yield
        y = yield from _moe_layer(pay, info, counts, prm["w_gate"], prm["w_up"], prm["w_down"], layer)
        yield
        res = (y, ada[layer][:, 5:6, :])

    out = _final_call(x, res[0], res[1])
    k, v, logf = shared[:3]
    return (out, jnp.stack(new_hist), k.reshape(bsz, t, N_HEADS, HEAD_DIM),
            v.reshape(bsz, t, N_HEADS, HEAD_DIM), logf)


def kernel(x_prompt, x_sample, cache_pool, cache_k, cache_v, cache_logf, c_prompt, c_sample, ada_w, ada_b, norm_g, w_pool, pool_scale, kv_ada_w, kv_ada_b, kv_norm, w_kvf, b_f, k_norm, w_q, q_norm, w_o, w_router, b_router, w_gate, w_up, w_down):
    bp = x_prompt.shape[0]
    c_all = jnp.concatenate([c_prompt, c_sample], axis=0)
    bc = c_all.shape[0]
    ada = _ada_call(c_all, ada_w, ada_b).reshape(DEPTH, bc, 6, D_MODEL)
    kva = _ada_call(c_all, kv_ada_w[None], kv_ada_b[None]).reshape(bc, 2, D_MODEL)

    head_of_lane = jnp.arange(D_MODEL, dtype=jnp.int32) // HEAD_DIM
    s_mat = (head_of_lane[:, None] == jnp.arange(LANES, dtype=jnp.int32)[None, :]).astype(BF16)
    prm = {
        "norm_g": norm_g,
        "w_pool": w_pool.astype(BF16),
        "pool_scale": pool_scale.reshape(N_A_LAYERS, 1, D_MODEL),
        "kv_norm": kv_norm.reshape(1, D_MODEL),
        "w_kv": w_kvf[:, :2 * KV_WIDTH].astype(BF16),
        "w_f": jnp.pad(w_kvf[:, 2 * KV_WIDTH:], ((0, 0), (0, LANES - N_HEADS))),
        "b_f": jnp.pad(b_f, (0, LANES - N_HEADS)).reshape(1, LANES),
        "k_norm": jnp.tile(k_norm, N_HEADS).reshape(1, D_MODEL),
        "w_q": w_q.astype(BF16),
        "q_norm": jnp.tile(q_norm, (1, N_HEADS)).reshape(N_B_LAYERS, 1, D_MODEL),
        "w_o": w_o.astype(BF16),
        "wrt": w_router.T,
        "br": b_router.reshape(N_EXPERTS, 1),
        "w_gate": w_gate.astype(BF16).reshape(DEPTH * N_EXPERTS, D_MODEL, D_EXPERT),
        "w_up": w_up.astype(BF16).reshape(DEPTH * N_EXPERTS, D_MODEL, D_EXPERT),
        "w_down": w_down.astype(BF16).reshape(DEPTH * N_EXPERTS, D_EXPERT, D_MODEL),
        "s_mat": s_mat,
        "st_mat": s_mat.T,
    }
    outs_p, outs_s = _alternate(
        _trunk(x_prompt, ada[:, :bp], kva[:bp], None, None, prm),
        _trunk(x_sample, ada[:, bp:], kva[bp:], cache_pool, (cache_k, cache_v, cache_logf), prm))
    return (outs_p[0], outs_s[0]) + outs_p[1:] + outs_s[1:]
```

```python
import functools

import jax
import jax.numpy as jnp
from jax import lax
from jax.experimental import pallas as pl
from jax.experimental.pallas import tpu as pltpu
from jax.experimental.pallas import tpu_sc as plsc

F32 = jnp.float32
BF16 = jnp.bfloat16

D_MODEL = 1024
DEPTH = 4
N_A_LAYERS = DEPTH // 2
N_B_LAYERS = DEPTH - N_A_LAYERS
POOL_WINDOWS = (2, 4, 8, 16)
N_POOL_GROUPS = len(POOL_WINDOWS)
POOL_GROUP_DIM = D_MODEL // N_POOL_GROUPS
POOL_BUF = max(POOL_WINDOWS) - 1
HIST_ROWS = POOL_BUF + 1
N_HEADS = 16
HEAD_DIM = D_MODEL // N_HEADS
KV_WIDTH = N_HEADS * HEAD_DIM
ATTN_SCALE = HEAD_DIM ** -0.5
N_EXPERTS = 16
N_EXPERT_GROUPS = 4
EXPERTS_PER_GROUP = N_EXPERTS // N_EXPERT_GROUPS
N_PAIRS = 6
N_CLASSES = N_EXPERT_GROUPS * N_PAIRS
D_EXPERT = D_MODEL // 2
EPS = 1e-6
NEG_INF = -1e30

LANES = 128
MXU_COLS = 256
SC_CORES = 2
SC_SUBCORES = 16
SC_WINDOW = 64
PACK_W = D_MODEL // 2
PAY_W = PACK_W + LANES
PAD_W = N_HEADS * LANES
LOG2E = 1.4426950408889634
DECAY_PIECES = 3
SUM_ROWS = 16
CLS_ROWS = 32
VMEM_LIMIT = 48 * 1024 * 1024


def _cparams(sem, flags=None):
    return pltpu.CompilerParams(dimension_semantics=sem, vmem_limit_bytes=VMEM_LIMIT, flags=flags)


def _bdot(a, b):
    return jnp.dot(a.astype(BF16), b.astype(BF16), preferred_element_type=F32)


def _split(a):
    hi = a.astype(BF16)
    lo = (a - hi.astype(F32)).astype(BF16)
    return hi, lo


_NN = (((1,), (0,)), ((), ()))
_NT = (((1,), (1,)), ((), ()))


def _dot3(a, b, dims=_NN):
    ah, al = _split(a)
    bh, bl = _split(b)
    d = lambda x, y: lax.dot_general(x, y, dims, preferred_element_type=F32)
    return d(ah, bh) + (d(ah, bl) + d(al, bh))


def _dot2_exact_rhs(a, b_bf16):
    ah, al = _split(a)
    return (jnp.dot(ah, b_bf16, preferred_element_type=F32)
            + jnp.dot(al, b_bf16, preferred_element_type=F32))


def _rms_mod(x, gain, shift, scale):
    ms = jnp.mean(x * x, axis=-1, keepdims=True)
    return (x * lax.rsqrt(ms + EPS)) * (gain * (1.0 + scale)) + shift


def _head_rms(z, s_ref, st_ref, gain):
    ss = jnp.dot((z * z).astype(BF16), s_ref[...], preferred_element_type=F32)
    inv = lax.rsqrt(ss * (1.0 / HEAD_DIM) + EPS)
    invf = _dot2_exact_rhs(inv, st_ref[...])
    return z * invf * gain


def _piece_lane(h, i):
    return jnp.where(h % 2 == 0, HEAD_DIM, 0) + DECAY_PIECES * (h // 2) + i


def _query_blocks(q):
    lane = lax.broadcasted_iota(jnp.int32, (1, LANES), 1)
    blocks = []
    for h in range(N_HEADS):
        pair = q[:, (h // 2) * LANES:(h // 2 + 1) * LANES]
        in_head = (lane >= HEAD_DIM) if h % 2 else (lane < HEAD_DIM)
        first = (0 if h % 2 else HEAD_DIM) + DECAY_PIECES * (h // 2)
        is_piece = (lane >= first) & (lane < first + DECAY_PIECES)
        blocks.append(jnp.where(in_head, pair, jnp.where(is_piece, 1.0, 0.0)))
    return blocks


_HI_MASK = 0xFFFF0000


def _pack_bf16_pairs(x):
    bits = lambda a: lax.bitcast_convert_type(a.astype(BF16).astype(F32), jnp.uint32)
    half = x.shape[1] // 2
    word = (bits(x[:, :half]) >> 16) | (bits(x[:, half:]) & jnp.uint32(_HI_MASK))
    return lax.bitcast_convert_type(word, F32)


def _unpack_bf16_pairs(w):
    word = lax.bitcast_convert_type(w, jnp.uint32)
    lo = lax.bitcast_convert_type(word << 16, F32).astype(BF16)
    hi = lax.bitcast_convert_type(word & jnp.uint32(_HI_MASK), F32).astype(BF16)
    return jnp.concatenate([lo, hi], axis=1)


def _pad_rows(a, rows):
    if a.shape[0] == rows:
        return a
    return jnp.concatenate([a, jnp.zeros((rows - a.shape[0], a.shape[1]), a.dtype)], axis=0)


def _route(lt, br):
    m = jnp.max(lt, axis=0, keepdims=True)
    p = jnp.exp(lt - m)
    scores = p / jnp.sum(p, axis=0, keepdims=True)
    sel = scores + br
    row = lambda a, e: a[e:e + 1, :]
    gs = []
    for g in range(N_EXPERT_GROUPS):
        v = [row(sel, g * EXPERTS_PER_GROUP + j) for j in range(EXPERTS_PER_GROUP)]
        best = None
        for i in range(EXPERTS_PER_GROUP):
            for j in range(i + 1, EXPERTS_PER_GROUP):
                s = v[i] + v[j]
                best = s if best is None else jnp.maximum(best, s)
        gs.append(best)
    bg = jnp.zeros_like(gs[0])
    bv = gs[0]
    for g in range(1, N_EXPERT_GROUPS):
        better = gs[g] > bv
        bg = jnp.where(better, float(g), bg)
        bv = jnp.where(better, gs[g], bv)

    def in_group(a, j):
        out = row(a, j)
        for g in range(1, N_EXPERT_GROUPS):
            out = jnp.where(bg == float(g), row(a, g * EXPERTS_PER_GROUP + j), out)
        return out

    sg = [in_group(sel, j) for j in range(EXPERTS_PER_GROUP)]
    cg = [in_group(scores, j) for j in range(EXPERTS_PER_GROUP)]

    def first_argmax(vals):
        mx = vals[0]
        for v in vals[1:]:
            mx = jnp.maximum(mx, v)
        idx = jnp.full_like(mx, float(len(vals) - 1))
        for j in range(len(vals) - 2, -1, -1):
            idx = jnp.where(vals[j] == mx, float(j), idx)
        return idx

    i1 = first_argmax(sg)
    i2 = first_argmax([jnp.where(i1 == float(j), -jnp.inf, sg[j]) for j in range(EXPERTS_PER_GROUP)])
    lo = jnp.minimum(i1, i2)
    hi = jnp.maximum(i1, i2)

    def pick(vals, idx):
        out = vals[0]
        for j in range(1, len(vals)):
            out = jnp.where(idx == float(j), vals[j], out)
        return out

    c_lo = pick(cg, lo)
    c_hi = pick(cg, hi)
    tot = c_lo + c_hi
    pair = jnp.where(lo == 0.0, hi - 1.0, jnp.where(lo == 1.0, hi + 1.0, 5.0))
    return bg * float(N_PAIRS) + pair, c_lo / tot, c_hi / tot


def _moe_prep(x1, ng2, sh2, sc2, wrt_ref, br_ref, run_ref, pay_ref, info_ref, cnt_ref):
    tm = x1.shape[0]
    tr = max(tm, LANES)
    h2 = _rms_mod(x1, ng2, sh2, sc2)
    lt = _dot3(wrt_ref[...], _pad_rows(h2, tr), _NT)
    cls, w_lo, w_hi = _route(lt, br_ref[...])

    r = lax.broadcasted_iota(jnp.int32, (LANES, tr), 0)
    wrows = jnp.where(r == 0, w_lo, jnp.where(r == 1, w_hi, 0.0))
    pay_ref[:, :PACK_W] = _pack_bf16_pairs(h2)
    pay_ref[:, PACK_W:] = wrows.T[:tm, :]

    crow = lax.broadcasted_iota(jnp.int32, (CLS_ROWS, tr), 0).astype(F32)
    lane = lax.broadcasted_iota(jnp.int32, (CLS_ROWS, tr), 1)
    onehot = jnp.where((crow == cls) & (lane < tm), 1.0, 0.0)
    us = lax.broadcasted_iota(jnp.int32, (tr, tr), 0)
    ut = lax.broadcasted_iota(jnp.int32, (tr, tr), 1)
    upper = jnp.where(us < ut, 1.0, 0.0).astype(BF16)
    before = jnp.dot(onehot.astype(BF16), upper, preferred_element_type=F32) + run_ref[:, 0:1]
    rank = jnp.sum(onehot * before, axis=0, keepdims=True)
    run_new = run_ref[...] + jnp.sum(onehot, axis=1, keepdims=True)
    run_ref[...] = run_new
    cnt_ref[...] = run_new
    ir = lax.broadcasted_iota(jnp.int32, (8, tr), 0)
    info = jnp.where(ir == 0, cls, jnp.where(ir == 1, rank, 0.0)).astype(jnp.int32)
    info_ref[...] = info[:, :tm]


def _ada_body(c_ref, w_ref, b_ref, o_ref):
    c = c_ref[...]
    o_ref[...] = _dot3(c * jax.nn.sigmoid(c), w_ref[...]) + b_ref[...]


def _ada_call(c_all, w, b):
    n_l, _, n_out = w.shape
    bc = c_all.shape[0]
    tn = 1536 if n_out % 1536 == 0 else 1024
    return pl.pallas_call(
        _ada_body,
        grid=(n_l, n_out // tn),
        in_specs=[pl.BlockSpec((bc, D_MODEL), lambda l, j: (0, 0)),
                  pl.BlockSpec((None, D_MODEL, tn), lambda l, j: (l, 0, j)),
                  pl.BlockSpec((None, 1, tn), lambda l, j: (l, 0, j))],
        out_specs=pl.BlockSpec((None, bc, tn), lambda l, j: (l, 0, j)),
        out_shape=jax.ShapeDtypeStruct((n_l, bc, n_out), F32),
        compiler_params=_cparams(("arbitrary", "arbitrary")),
        name="adaln",
    )(c_all, w, b.reshape(n_l, 1, n_out))


def _tile_rows(t, rows=512):
    return rows if t % rows == 0 else t


def _tok_spec(tm, width=D_MODEL):
    return pl.BlockSpec((None, tm, width), lambda b, t: (b, t, 0))


def _per_batch_spec(rows, width=D_MODEL):
    return pl.BlockSpec((None, rows, width), lambda b, t: (b, 0, 0))


def _const_spec(shape):
    nd = len(shape)
    return pl.BlockSpec(shape, lambda b, t: (0,) * nd)


def _prep_out(bsz, t, tm):
    shapes = [jax.ShapeDtypeStruct((bsz, t, PAY_W), F32),
              jax.ShapeDtypeStruct((bsz, 8, t), jnp.int32),
              jax.ShapeDtypeStruct((CLS_ROWS, LANES), F32)]
    specs = [_tok_spec(tm, PAY_W),
             pl.BlockSpec((None, 8, tm), lambda b, t: (b, 0, t)),
             _const_spec((CLS_ROWS, LANES))]
    return shapes, specs


def _mixer_body(has_res, tm, start_pos, *refs):
    it = iter(refs)
    x_ref, xp_ref = next(it), next(it)
    if has_res:
        y_ref, yp_ref, g2p_ref = next(it), next(it), next(it)
    hist_ref, ada_ref, ng_ref, wp_ref, ps_ref, wrt_ref, br_ref = (next(it) for _ in range(7))
    x1_ref, pay_ref, info_ref, cnt_ref, hout_ref = (next(it) for _ in range(5))
    run_ref = next(it)

    b = pl.program_id(0)
    t = pl.program_id(1)

    @pl.when((b == 0) & (t == 0))
    def _():
        run_ref[...] = jnp.zeros_like(run_ref)

    xin = x_ref[...]
    xp = xp_ref[...]
    if has_res:
        g2p = g2p_ref[...]
        xin = xin + g2p * y_ref[...]
        xp = xp + g2p * yp_ref[...]
    ada = ada_ref[...]
    sh1, sc1, g1, sh2, sc2 = (ada[i:i + 1] for i in range(5))
    ng = ng_ref[...]
    u = _rms_mod(xin, ng[0:1], sh1, sc1)
    up = _rms_mod(xp, ng[0:1], sh1, sc1)
    up = jnp.where(t == 0, hist_ref[...], up)
    level = jnp.concatenate([up, u], axis=0)
    sums = []
    for g, w in enumerate(POOL_WINDOWS):
        level = level + pltpu.roll(level, w // 2, 0)
        sums.append(level[HIST_ROWS:, :POOL_GROUP_DIM])
        if g + 1 < N_POOL_GROUPS:
            level = level[:, POOL_GROUP_DIM:]

    pos = start_pos + t * tm + lax.broadcasted_iota(jnp.int32, (tm, 1), 0)
    cols = []
    for g, w in enumerate(POOL_WINDOWS):
        sl = slice(g * POOL_GROUP_DIM, (g + 1) * POOL_GROUP_DIM)
        cnt = jnp.minimum(pos + 1, w).astype(F32)
        cols.append(_bdot(sums[g] / cnt - u[:, sl], wp_ref[g]))
    x1 = xin + (g1 * ps_ref[...]) * jnp.concatenate(cols, axis=1)
    x1_ref[...] = x1
    hout_ref[...] = u[tm - POOL_BUF:, :]
    _moe_prep(x1, ng[1:2], sh2, sc2, wrt_ref, br_ref, run_ref, pay_ref, info_ref, cnt_ref)


def _mixer_call(x, res, hist16, ada_l, ng_l, wp_l, ps_l, wrt, br, start_pos):
    bsz, t, _ = x.shape
    tm = _tile_rows(t)
    prev_spec = pl.BlockSpec((None, HIST_ROWS, D_MODEL),
                             lambda b, i: (b, jnp.maximum(i * (tm // HIST_ROWS) - 1, 0), 0))
    ins = [x, x]
    specs = [_tok_spec(tm), prev_spec]
    if res is not None:
        y, g2p = res
        ins += [y, y, g2p]
        specs += [_tok_spec(tm), prev_spec, _per_batch_spec(1)]
    ins += [hist16, ada_l, ng_l, wp_l, ps_l, wrt, br]
    specs += [_per_batch_spec(HIST_ROWS), _per_batch_spec(6), _const_spec((2, D_MODEL)),
              _const_spec((N_POOL_GROUPS, POOL_GROUP_DIM, POOL_GROUP_DIM)), _const_spec((1, D_MODEL)),
              _const_spec((N_EXPERTS, D_MODEL)), _const_spec((N_EXPERTS, 1))]
    p_shapes, p_specs = _prep_out(bsz, t, tm)
    return pl.pallas_call(
        functools.partial(_mixer_body, res is not None, tm, start_pos),
        grid=(bsz, t // tm),
        in_specs=specs,
        out_specs=[_tok_spec(tm)] + p_specs + [_per_batch_spec(POOL_BUF)],
        out_shape=[jax.ShapeDtypeStruct((bsz, t, D_MODEL), F32)] + p_shapes
                  + [jax.ShapeDtypeStruct((bsz, POOL_BUF, D_MODEL), F32)],
        scratch_shapes=[pltpu.VMEM((CLS_ROWS, LANES), F32)],
        compiler_params=_cparams(("arbitrary", "arbitrary")),
        name="pool_mixer",
    )(*ins)


def _proj_body(with_kv, tm, *refs):
    it = iter(refs)
    x1p_ref, y_ref, g2p_ref, ada_ref, ng_ref, wq_ref, qn_ref, s_ref, st_ref = (next(it) for _ in range(9))
    if with_kv:
        kva_ref, kvn_ref, wkv_ref, wf_ref, bf_ref, kn_ref = (next(it) for _ in range(6))
    x_ref, q_ref = next(it), next(it)
    if with_kv:
        k_ref, v_ref, kb_ref, vt_ref, lf_ref, lfw_ref = (next(it) for _ in range(6))

    x = x1p_ref[...] + g2p_ref[...] * y_ref[...]
    x_ref[...] = x
    ada = ada_ref[...]
    ng = ng_ref[...]
    h = _rms_mod(x, ng[0:1], ada[0:1], ada[1:2])
    q = _head_rms(_bdot(h, wq_ref[...]), s_ref, st_ref, qn_ref[...])
    rows = max(tm, LANES)
    for i, blk in enumerate(_query_blocks(q * (ATTN_SCALE * LOG2E))):
        q_ref[i * LANES:(i + 1) * LANES, :] = _pad_rows(blk, rows).T[:, :tm].astype(BF16)
    if with_kv:
        kva = kva_ref[...]
        hk = _rms_mod(x, kvn_ref[...], kva[0:1], kva[1:2])
        proj = _bdot(hk, wkv_ref[...])
        k = _head_rms(proj[:, :KV_WIDTH], s_ref, st_ref, kn_ref[...])
        v = proj[:, KV_WIDTH:]
        k_ref[...] = k
        v_ref[...] = v
        kb_ref[...] = k.astype(BF16)
        vt_ref[...] = _pad_rows(v, max(tm, LANES)).T[:, :tm].astype(BF16)
        z = _dot3(hk, wf_ref[...]) + bf_ref[...]
        lf = jnp.minimum(z, 0.0) - jnp.log(1.0 + jnp.exp(-jnp.abs(z)))
        lf_ref[...] = lf[:, :N_HEADS]
        lane = lax.broadcasted_iota(jnp.int32, (1, LANES), 1)
        lfw_ref[...] = jnp.where(lane < N_HEADS, lf, 0.0)


def _proj_call(x1p, y, g2p, ada_l, ng_l, wq, qn, s_mat, st_mat, kv=None):
    bsz, t, _ = x1p.shape
    tm = _tile_rows(t, 256)
    ins = [x1p, y, g2p, ada_l, ng_l, wq, qn, s_mat, st_mat]
    specs = [_tok_spec(tm), _tok_spec(tm), _per_batch_spec(1), _per_batch_spec(6), _const_spec((2, D_MODEL)),
             _const_spec((D_MODEL, KV_WIDTH)), _const_spec((1, D_MODEL)),
             _const_spec((D_MODEL, LANES)), _const_spec((LANES, D_MODEL))]
    out_shapes = [jax.ShapeDtypeStruct((bsz, t, D_MODEL), F32), jax.ShapeDtypeStruct((bsz, PAD_W, t), BF16)]
    out_specs = [_tok_spec(tm), pl.BlockSpec((None, PAD_W, tm), lambda b, i: (b, 0, i))]
    if kv is not None:
        kva, kvn, wkv, wf, bf, kn = kv
        ins += [kva, kvn, wkv, wf, bf, kn]
        specs += [_per_batch_spec(2), _const_spec((1, D_MODEL)), _const_spec((D_MODEL, 2 * KV_WIDTH)),
                  _const_spec((D_MODEL, LANES)), _const_spec((1, LANES)), _const_spec((1, D_MODEL))]
        out_shapes += [jax.ShapeDtypeStruct((bsz, t, D_MODEL), F32)] * 2
        out_shapes += [jax.ShapeDtypeStruct((bsz, t, D_MODEL), BF16), jax.ShapeDtypeStruct((bsz, D_MODEL, t), BF16)]
        out_shapes += [jax.ShapeDtypeStruct((bsz, t, N_HEADS), F32), jax.ShapeDtypeStruct((bsz, t, LANES), F32)]
        out_specs += [_tok_spec(tm)] * 2
        out_specs += [_tok_spec(tm), pl.BlockSpec((None, D_MODEL, tm), lambda b, i: (b, 0, i))]
        out_specs += [_tok_spec(tm, N_HEADS), _tok_spec(tm, LANES)]
    return pl.pallas_call(
        functools.partial(_proj_body, kv is not None, tm),
        grid=(bsz, t // tm),
        in_specs=specs, out_specs=out_specs, out_shape=out_shapes,
        compiler_params=_cparams(("arbitrary", "arbitrary")),
        name="qkv_proj" if kv is not None else "q_proj",
    )(*ins)


def _decay_body(tc, nb, lf_ref, o_ref, carry_ref):
    @pl.when(pl.program_id(1) == 0)
    def _():
        carry_ref[...] = jnp.zeros_like(carry_ref)

    lf = jnp.concatenate([lf_ref[i] for i in range(nb)], axis=1)
    r = lax.broadcasted_iota(jnp.int32, (tc, tc), 0)
    c = lax.broadcasted_iota(jnp.int32, (tc, tc), 1)
    lower = jnp.where(r >= c, 1.0, 0.0).astype(BF16)
    hi, lo = _split(lf)
    f = (jnp.dot(lower, hi, preferred_element_type=F32) + jnp.dot(lower, lo, preferred_element_type=F32)
         + carry_ref[0:1, :])
    carry_ref[...] = jnp.broadcast_to(f[tc - 1:tc, :], carry_ref.shape)
    bias = f * (-LOG2E)
    p1 = bias.astype(BF16)
    r1 = bias - p1.astype(F32)
    p2 = r1.astype(BF16)
    p3 = (r1 - p2.astype(F32)).astype(BF16)
    hr = lax.broadcasted_iota(jnp.int32, (LANES, LANES), 0)
    lc = lax.broadcasted_iota(jnp.int32, (LANES, LANES), 1)
    places = [jnp.where((lc == _piece_lane(hr, i)) & (hr < N_HEADS), 1.0, 0.0).astype(BF16)
              for i in range(DECAY_PIECES)]
    for b in range(nb):
        out = None
        for p, place in zip((p1, p2, p3), places):
            term = jnp.dot(p[:, b * LANES:(b + 1) * LANES], place, preferred_element_type=F32)
            out = term if out is None else out + term
        o_ref[b] = out.astype(BF16)


def _decay_call(lfw):
    bsz, tk, _ = lfw.shape
    tc = 512
    nb = 8 if bsz % 8 == 0 else 1
    spec = pl.BlockSpec((nb, tc, LANES), lambda b, t: (b, t, 0))
    return pl.pallas_call(
        functools.partial(_decay_body, tc, nb),
        grid=(bsz // nb, tk // tc), in_specs=[spec], out_specs=spec,
        out_shape=jax.ShapeDtypeStruct(lfw.shape, BF16),
        scratch_shapes=[pltpu.VMEM((8, nb * LANES), F32)],
        compiler_params=_cparams(("arbitrary", "arbitrary")),
        name="decay_bias",
    )(lfw)


def _transpose_body(v_ref, o_ref):
    o_ref[...] = v_ref[...].T.astype(BF16)


def _transpose_call(v):
    bsz, p, _ = v.shape
    tp = 512
    return pl.pallas_call(
        _transpose_body, grid=(bsz, p // tp),
        in_specs=[pl.BlockSpec((None, tp, D_MODEL), lambda b, t: (b, t, 0))],
        out_specs=pl.BlockSpec((None, D_MODEL, tp), lambda b, t: (b, 0, t)),
        out_shape=jax.ShapeDtypeStruct((bsz, D_MODEL, p), BF16),
        compiler_params=_cparams(("arbitrary", "arbitrary")),
        name="value_transpose",
    )(v)


def _flash_body(tq, tkp, tks, n_past, n_self, *refs):
    it = iter(refs)
    q_ref = next(it)
    if n_past:
        kp_ref, vtp_ref, fpp_ref = next(it), next(it), next(it)
    ks_ref, vts_ref, fps_ref = next(it), next(it), next(it)
    o_ref, m_ref, l_ref, acc_ref = next(it), next(it), next(it), next(it)
    qi = pl.program_id(1)
    step = pl.program_id(2)

    @pl.when(step == 0)
    def _():
        m_ref[...] = jnp.full_like(m_ref, NEG_INF)
        l_ref[...] = jnp.zeros_like(l_ref)
        acc_ref[...] = jnp.zeros_like(acc_ref)

    def process(k_ref, vt_ref, fp_ref, tk, key_base):
        fp = fp_ref[...]
        lane = lax.broadcasted_iota(jnp.int32, (1, LANES), 1)
        if key_base is not None:
            kpos = key_base + lax.broadcasted_iota(jnp.int32, (tk, 1), 0)
            qpos = qi * tq + lax.broadcasted_iota(jnp.int32, (1, tq), 1)
            visible = kpos <= qpos

        pair_keys = {}

        def logits(h):
            hp = h // 2
            if hp not in pair_keys:
                pair_keys[hp] = k_ref[:, hp * LANES:(hp + 1) * LANES].astype(BF16)
            own = (lane >= HEAD_DIM) if h % 2 else (lane < HEAD_DIM)
            lhs = jnp.where(own, pair_keys[hp], fp)
            rhs = q_ref[h * LANES:(h + 1) * LANES, :]
            if tq < 2 * MXU_COLS and tk % 2 == 0:
                half = tk // 2
                s = jnp.concatenate([jnp.dot(lhs[:half], rhs, preferred_element_type=F32),
                                     jnp.dot(lhs[half:], rhs, preferred_element_type=F32)], axis=0)
            else:
                s = jnp.dot(lhs, rhs, preferred_element_type=F32)
            return s if key_base is None else jnp.where(visible, s, NEG_INF)

        ones_rows = jnp.ones((SUM_ROWS, tk), BF16)
        s_next = logits(0)
        for h in range(N_HEADS):
            s = s_next
            if h + 1 < N_HEADS:
                s_next = logits(h + 1)
            rows = slice(h * HEAD_DIM, (h + 1) * HEAD_DIM)
            m_old = m_ref[h]
            m_new = jnp.maximum(m_old, jnp.max(s, axis=0, keepdims=True))
            alpha = jnp.exp2(m_old - m_new)
            p = jnp.exp2(s - m_new).astype(BF16)
            m_ref[h] = m_new
            pv = jnp.dot(jnp.concatenate([vt_ref[rows, :], ones_rows], axis=0), p,
                         preferred_element_type=F32)
            l_ref[h] = alpha * l_ref[h] + pv[HEAD_DIM:HEAD_DIM + 1, :]
            acc_ref[rows, :] = alpha * acc_ref[rows, :] + pv[:HEAD_DIM, :]

    if n_past:
        @pl.when(step < n_past)
        def _():
            process(kp_ref, vtp_ref, fpp_ref, tkp, None)

    j = step - n_past
    last = (qi * tq + tq - 1) // tks
    has_hidden = (j + 1) * tks - 1 > qi * tq

    @pl.when((j >= 0) & (j <= last) & has_hidden)
    def _():
        process(ks_ref, vts_ref, fps_ref, tks, j * tks)

    @pl.when((j >= 0) & (j <= last) & jnp.logical_not(has_hidden))
    def _():
        process(ks_ref, vts_ref, fps_ref, tks, None)

    @pl.when(step == n_past + n_self - 1)
    def _():
        row = lax.broadcasted_iota(jnp.int32, (LANES, 1), 0)
        if tq % LANES:
            r = lax.broadcasted_iota(jnp.int32, (tq, tq), 0)
            c = lax.broadcasted_iota(jnp.int32, (tq, tq), 1)
            eye = jnp.where(r == c, 1.0, 0.0).astype(BF16)
        for hp in range(N_HEADS // 2):
            denom = jnp.where(row < HEAD_DIM, l_ref[2 * hp], l_ref[2 * hp + 1])
            o_t = acc_ref[hp * LANES:(hp + 1) * LANES, :] / denom
            if tq % LANES:
                o = lax.dot_general(eye, o_t.astype(BF16), _NT, preferred_element_type=F32)
            else:
                o = o_t.T
            o_ref[:, hp * LANES:(hp + 1) * LANES] = o.astype(BF16)


def _flash_call(q, k_self, vt_self, past, fp, n_past_keys):
    bsz, _, t_q = q.shape
    tq = 512 if t_q % 512 == 0 else t_q
    tks = 512 if t_q % 512 == 0 else t_q
    n_self = t_q // tks
    tkp = 512
    n_past = n_past_keys // tkp
    fp_self_base = n_past_keys // tks

    def self_idx(i, s):
        return jnp.clip(s - n_past, 0, (i * tq + tq - 1) // tks)

    ins = [q]
    specs = [pl.BlockSpec((None, PAD_W, tq), lambda b, i, s: (b, 0, i))]
    if n_past:
        past_idx = lambda s: jnp.minimum(s, n_past - 1)
        ins += [past[0], past[1], fp]
        specs += [pl.BlockSpec((None, tkp, D_MODEL), lambda b, i, s: (b, past_idx(s), 0)),
                  pl.BlockSpec((None, D_MODEL, tkp), lambda b, i, s: (b, 0, past_idx(s))),
                  pl.BlockSpec((None, tkp, LANES), lambda b, i, s: (b, past_idx(s), 0))]
    ins += [k_self, vt_self, fp]
    specs += [pl.BlockSpec((None, tks, D_MODEL), lambda b, i, s: (b, self_idx(i, s), 0)),
              pl.BlockSpec((None, D_MODEL, tks), lambda b, i, s: (b, 0, self_idx(i, s))),
              pl.BlockSpec((None, tks, LANES), lambda b, i, s: (b, fp_self_base + self_idx(i, s), 0))]
    return pl.pallas_call(
        functools.partial(_flash_body, tq, tkp, tks, n_past, n_self),
        grid=(bsz, t_q // tq, n_past + n_self),
        in_specs=specs,
        out_specs=pl.BlockSpec((None, tq, D_MODEL), lambda b, i, s: (b, i, 0)),
        out_shape=jax.ShapeDtypeStruct((bsz, t_q, D_MODEL), BF16),
        scratch_shapes=[pltpu.VMEM((N_HEADS, 1, tq), F32), pltpu.VMEM((N_HEADS, 1, tq), F32),
                        pltpu.VMEM((D_MODEL, tq), F32)],
        compiler_params=_cparams(("arbitrary", "arbitrary", "arbitrary")),
        name="fox_attention",
    )(*ins)


def _oproj_body(x_ref, o_ref, wo_ref, ada_ref, ng_ref, wrt_ref, br_ref,
                x1_ref, pay_ref, info_ref, cnt_ref, run_ref):
    @pl.when((pl.program_id(0) == 0) & (pl.program_id(1) == 0))
    def _():
        run_ref[...] = jnp.zeros_like(run_ref)

    ada = ada_ref[...]
    x1 = x_ref[...] + ada[2:3] * jnp.dot(o_ref[...], wo_ref[...], preferred_element_type=F32)
    x1_ref[...] = x1
    _moe_prep(x1, ng_ref[...][1:2], ada[3:4], ada[4:5], wrt_ref, br_ref, run_ref, pay_ref, info_ref, cnt_ref)


def _oproj_call(x, o, wo, ada_l, ng_l, wrt, br):
    bsz, t, _ = x.shape
    tm = _tile_rows(t)
    p_shapes, p_specs = _prep_out(bsz, t, tm)
    return pl.pallas_call(
        _oproj_body,
        grid=(bsz, t // tm),
        in_specs=[_tok_spec(tm), _tok_spec(tm), _const_spec((KV_WIDTH, D_MODEL)), _per_batch_spec(6),
                  _const_spec((2, D_MODEL)), _const_spec((N_EXPERTS, D_MODEL)), _const_spec((N_EXPERTS, 1))],
        out_specs=[_tok_spec(tm)] + p_specs,
        out_shape=[jax.ShapeDtypeStruct((bsz, t, D_MODEL), F32)] + p_shapes,
        scratch_shapes=[pltpu.VMEM((CLS_ROWS, LANES), F32)],
        compiler_params=_cparams(("arbitrary", "arbitrary")),
        name="attn_out_proj",
    )(x, o, wo, ada_l, ng_l, wrt, br)


def _final_body(x_ref, y_ref, g_ref, o_ref):
    o_ref[...] = x_ref[...] + g_ref[...] * y_ref[...]


def _final_call(x1, y, g2):
    bsz, t, _ = x1.shape
    tm = _tile_rows(t, 1024)
    return pl.pallas_call(
        _final_body, grid=(bsz, t // tm),
        in_specs=[_tok_spec(tm), _tok_spec(tm), _per_batch_spec(1)],
        out_specs=_tok_spec(tm),
        out_shape=jax.ShapeDtypeStruct(x1.shape, F32),
        input_output_aliases={0: 0},
        compiler_params=_cparams(("arbitrary", "arbitrary")),
        name="final_residual",
    )(x1, y, g2)


def _sc_worker_loop(n_win, fn):
    wid = lax.axis_index("s") * SC_CORES + lax.axis_index("c")
    n_workers = SC_CORES * SC_SUBCORES

    @pl.loop(0, pl.cdiv(n_win, n_workers))
    def _(j):
        win = j * n_workers + wid

        @pl.when(win < n_win)
        def _():
            fn(pl.multiple_of(win * SC_WINDOW, SC_WINDOW))


def _sc_scatter_rows(rows, idx, n_out):
    n, width = rows.shape
    mesh = plsc.VectorSubcoreMesh(core_axis_name="c", subcore_axis_name="s")

    @functools.partial(
        pl.kernel, mesh=mesh, out_type=jax.ShapeDtypeStruct((n_out, width), rows.dtype),
        scratch_types=[pltpu.VMEM((SC_WINDOW,), jnp.int32), pltpu.VMEM((SC_WINDOW, width), rows.dtype)])
    def k(rows_hbm, idx_hbm, out_hbm, idx_v, rows_v):
        def one(base):
            pltpu.sync_copy(idx_hbm.at[pl.ds(base, SC_WINDOW)], idx_v)
            pltpu.sync_copy(rows_hbm.at[pl.ds(base, SC_WINDOW)], rows_v)
            pltpu.sync_copy(rows_v, out_hbm.at[idx_v])
        _sc_worker_loop(n // SC_WINDOW, one)

    return k(rows, idx)


def _sc_gather_rows(table, idx):
    n = idx.shape[0]
    width = table.shape[1]
    mesh = plsc.VectorSubcoreMesh(core_axis_name="c", subcore_axis_name="s")

    @functools.partial(
        pl.kernel, mesh=mesh, out_type=jax.ShapeDtypeStruct((n, width), table.dtype),
        scratch_types=[pltpu.VMEM((SC_WINDOW,), jnp.int32), pltpu.VMEM((SC_WINDOW, width), table.dtype)])
    def k(table_hbm, idx_hbm, out_hbm, idx_v, rows_v):
        def one(base):
            pltpu.sync_copy(idx_hbm.at[pl.ds(base, SC_WINDOW)], idx_v)
            pltpu.sync_copy(table_hbm.at[idx_v], rows_v)
            pltpu.sync_copy(rows_v, out_hbm.at[pl.ds(base, SC_WINDOW)])
        _sc_worker_loop(n // SC_WINDOW, one)

    return k(table, idx)


def _moe_body(e1_ref, e2_ref, na_ref, hs_ref, g1_ref, g2_ref, u1_ref, u2_ref, d1_ref, d2_ref, o_ref):
    @pl.when(pl.program_id(0) < na_ref[0])
    def _():
        blk = hs_ref[...]
        h = _unpack_bf16_pairs(blk[:, :PACK_W])
        y = None
        for lane, (g_ref, u_ref, d_ref) in enumerate(((g1_ref, u1_ref, d1_ref), (g2_ref, u2_ref, d2_ref))):
            gate = jnp.dot(h, g_ref[...], preferred_element_type=F32)
            up = jnp.dot(h, u_ref[...], preferred_element_type=F32)
            w = blk[:, PACK_W + lane:PACK_W + lane + 1]
            act = (gate * jax.nn.sigmoid(gate) * up * w).astype(BF16)
            term = jnp.dot(act, d_ref[...], preferred_element_type=F32)
            y = term if y is None else y + term
        o_ref[...] = y


def _moe_call(hs, tile_e1, tile_e2, n_active, wg, wu, wd, layer, tm):
    n_s = hs.shape[0]
    n_tiles = n_s // tm
    base = layer * N_EXPERTS

    def row_map(i, e1, e2, na):
        return (jnp.minimum(i, na[0] - 1), 0)

    def w_map(which):
        def m(i, e1, e2, na):
            e = (e1, e2)[which]
            return (base + e[jnp.minimum(i, na[0] - 1)], 0, 0)
        return m

    gu = lambda which: pl.BlockSpec((None, D_MODEL, D_EXPERT), w_map(which))
    dn = lambda which: pl.BlockSpec((None, D_EXPERT, D_MODEL), w_map(which))
    return pl.pallas_call(
        _moe_body,
        grid_spec=pltpu.PrefetchScalarGridSpec(
            num_scalar_prefetch=3, grid=(n_tiles,),
            in_specs=[pl.BlockSpec((tm, PAY_W), row_map), gu(0), gu(1), gu(0), gu(1), dn(0), dn(1)],
            out_specs=pl.BlockSpec((tm, D_MODEL), row_map)),
        out_shape=jax.ShapeDtypeStruct((n_s, D_MODEL), F32),
        compiler_params=_cparams(("arbitrary",)),
        name="grouped_experts",
    )(tile_e1, tile_e2, n_active, hs, wg, wg, wu, wu, wd, wd)


_PAIR_LO = (0, 0, 0, 1, 1, 2)
_PAIR_HI = (1, 2, 3, 2, 3, 3)


def _moe_layer(pay, info, counts, wg, wu, wd, layer):
    bsz, t, _ = pay.shape
    n = bsz * t
    tm = 256
    n_s =((n + N_CLASSES * (tm - 1)) // tm + 1) * tm
    n_tiles = n_s // tm
    cls = info[:, 0, :].reshape(n)
    rank = info[:, 1, :].reshape(n)
    cnt = counts[:N_CLASSES, 0].astype(jnp.int32)
    padded = ((cnt + tm - 1) // tm) * tm
    ends = jnp.cumsum(padded)
    starts = ends - padded
    dest = starts[cls] + rank
    tile_start = jnp.arange(n_tiles, dtype=jnp.int32) * tm
    tile_cls = jnp.minimum(jnp.sum((tile_start[:, None] >= ends[None, :]).astype(jnp.int32), axis=1),
                           N_CLASSES - 1)
    grp = tile_cls // N_PAIRS
    pr = tile_cls % N_PAIRS
    tile_e1 = grp * EXPERTS_PER_GROUP + jnp.asarray(_PAIR_LO, jnp.int32)[pr]
    tile_e2 = grp * EXPERTS_PER_GROUP + jnp.asarray(_PAIR_HI, jnp.int32)[pr]
    n_active = (ends[-1:] // tm).astype(jnp.int32)
    hs = _sc_scatter_rows(pay.reshape(n, PAY_W), dest, n_s)
    yield
    ys = _moe_call(hs, tile_e1, tile_e2, n_active, wg, wu, wd, layer, tm)
    yield
    return _sc_gather_rows(ys, dest).reshape(bsz, t, D_MODEL)


def _alternate(first, second):
    live = {0: first, 1: second}
    results = {}

    def advance(i):
        try:
            next(live[i])
        except StopIteration as done:
            results[i] = done.value
            del live[i]

    advance(0)
    while live:
        for i in (0, 1):
            if i in live:
                advance(i)
    return results[0], results[1]


def _trunk(x, ada, kva, hist, past, prm):
    bsz, t, _ = x.shape
    start_pos = 0 if past is None else past[0].shape[1]
    wrt, br = prm["wrt"], prm["br"]
    res = None
    new_hist = []
    for layer in range(N_A_LAYERS):
        if hist is None:
            h16 = jnp.zeros((bsz, HIST_ROWS, D_MODEL), F32)
        else:
            h16 = jnp.pad(hist[layer], ((0, 0), (1, 0), (0, 0)))
        x, pay, info, counts, hout = _mixer_call(
            x, res, h16, ada[layer], prm["norm_g"][layer], prm["w_pool"][layer], prm["pool_scale"][layer],
            wrt, br, start_pos)
        new_hist.append(hout)
        yield
        y = yield from _moe_layer(pay, info, counts, prm["w_gate"], prm["w_up"], prm["w_down"], layer)
        yield
        res = (y, ada[layer][:, 5:6, :])

    shared = None
    for j in range(N_B_LAYERS):
        layer = N_A_LAYERS + j
        y, g2p = res
        if j == 0:
            kv = (kva, prm["kv_norm"], prm["w_kv"], prm["w_f"], prm["b_f"], prm["k_norm"])
            xr, q, k, v, kb, vt, logf, lfw = _proj_call(
                x, y, g2p, ada[layer], prm["norm_g"][layer], prm["w_q"][j], prm["q_norm"][j],
                prm["s_mat"], prm["st_mat"], kv)
            if past is None:
                past_kv = None
                fp = _decay_call(lfw)
            else:
                ck, cv, clogf = past
                n_past = ck.shape[1]
                pad = -(n_past + t) % 512
                lfw = jnp.concatenate([jnp.pad(clogf, ((0, 0), (0, 0), (0, LANES - N_HEADS))), lfw,
                                       jnp.zeros((bsz, pad, LANES), F32)], axis=1)
                fp = _decay_call(lfw)
                past_kv = (ck.reshape(bsz, n_past, D_MODEL), _transpose_call(cv.reshape(bsz, n_past, D_MODEL)))
            shared = (k, v, logf, kb, vt, past_kv, fp)
        else:
            xr, q = _proj_call(x, y, g2p, ada[layer], prm["norm_g"][layer], prm["w_q"][j], prm["q_norm"][j],
                               prm["s_mat"], prm["st_mat"])
        yield
        o = _flash_call(q, shared[3], shared[4], shared[5], shared[6], start_pos)
        yield
        x, pay, info, counts = _oproj_call(xr, o, prm["w_o"][j], ada[layer], prm["norm_g"][layer], wrt, br)
        yield
        y = yield from _moe_layer(pay, info, counts, prm["w_gate"], prm["w_up"], prm["w_down"], layer)
        yield
        res = (y, ada[layer][:, 5:6, :])

    out = _final_call(x, res[0], res[1])
    k, v, logf = shared[:3]
    return (out, jnp.stack(new_hist), k.reshape(bsz, t, N_HEADS, HEAD_DIM),
            v.reshape(bsz, t, N_HEADS, HEAD_DIM), logf)


def kernel(x_prompt, x_sample, cache_pool, cache_k, cache_v, cache_logf, c_prompt, c_sample, ada_w, ada_b, norm_g, w_pool, pool_scale, kv_ada_w, kv_ada_b, kv_norm, w_kvf, b_f, k_norm, w_q, q_norm, w_o, w_router, b_router, w_gate, w_up, w_down):
    bp = x_prompt.shape[0]
    c_all = jnp.concatenate([c_prompt, c_sample], axis=0)
    bc = c_all.shape[0]
    ada = _ada_call(c_all, ada_w, ada_b).reshape(DEPTH, bc, 6, D_MODEL)
    kva = _ada_call(c_all, kv_ada_w[None], kv_ada_b[None]).reshape(bc, 2, D_MODEL)

    head_of_lane = jnp.arange(D_MODEL, dtype=jnp.int32) // HEAD_DIM
    s_mat = (head_of_lane[:, None] == jnp.arange(LANES, dtype=jnp.int32)[None, :]).astype(BF16)
    prm = {
        "norm_g": norm_g,
        "w_pool": w_pool.astype(BF16),
        "pool_scale": pool_scale.reshape(N_A_LAYERS, 1, D_MODEL),
        "kv_norm": kv_norm.reshape(1, D_MODEL),
        "w_kv": w_kvf[:, :2 * KV_WIDTH].astype(BF16),
        "w_f": jnp.pad(w_kvf[:, 2 * KV_WIDTH:], ((0, 0), (0, LANES - N_HEADS))),
        "b_f": jnp.pad(b_f, (0, LANES - N_HEADS)).reshape(1, LANES),
        "k_norm": jnp.tile(k_norm, N_HEADS).reshape(1, D_MODEL),
        "w_q": w_q.astype(BF16),
        "q_norm": jnp.tile(q_norm, (1, N_HEADS)).reshape(N_B_LAYERS, 1, D_MODEL),
        "w_o": w_o.astype(BF16),
        "wrt": w_router.T,
        "br": b_router.reshape(N_EXPERTS, 1),
        "w_gate": w_gate.astype(BF16).reshape(DEPTH * N_EXPERTS, D_MODEL, D_EXPERT),
        "w_up": w_up.astype(BF16).reshape(DEPTH * N_EXPERTS, D_MODEL, D_EXPERT),
        "w_down": w_down.astype(BF16).reshape(DEPTH * N_EXPERTS, D_EXPERT, D_MODEL),
        "s_mat": s_mat,
        "st_mat": s_mat.T,
    }
    outs_p, outs_s = _alternate(
        _trunk(x_prompt, ada[:, :bp], kva[:bp], None, None, prm),
        _trunk(x_sample, ada[:, bp:], kva[bp:], cache_pool, (cache_k, cache_v, cache_logf), prm))
    return (outs_p[0], outs_s[0]) + outs_p[1:] + outs_s[1:]
```

```python
import functools

import jax
import jax.numpy as jnp
from jax import lax
from jax.experimental import pallas as pl
from jax.experimental.pallas import tpu as pltpu
from jax.experimental.pallas import tpu_sc as plsc

F32 = jnp.float32
BF16 = jnp.bfloat16

D_MODEL = 1024
DEPTH = 4
N_A_LAYERS = DEPTH // 2
N_B_LAYERS = DEPTH - N_A_LAYERS
POOL_WINDOWS = (2, 4, 8, 16)
N_POOL_GROUPS = len(POOL_WINDOWS)
POOL_GROUP_DIM = D_MODEL // N_POOL_GROUPS
POOL_BUF = max(POOL_WINDOWS) - 1
HIST_ROWS = POOL_BUF + 1
N_HEADS = 16
HEAD_DIM = D_MODEL // N_HEADS
KV_WIDTH = N_HEADS * HEAD_DIM
ATTN_SCALE = HEAD_DIM ** -0.5
N_EXPERTS = 16
N_EXPERT_GROUPS = 4
EXPERTS_PER_GROUP = N_EXPERTS // N_EXPERT_GROUPS
N_PAIRS = 6
N_CLASSES = N_EXPERT_GROUPS * N_PAIRS
D_EXPERT = D_MODEL // 2
EPS = 1e-6
NEG_INF = -1e30

LANES = 128
MXU_COLS = 256
SC_CORES = 2
SC_SUBCORES = 16
SC_WINDOW = 64
PACK_W = D_MODEL // 2
PAY_W = PACK_W + LANES
PAD_W = N_HEADS * LANES
LOG2E = 1.4426950408889634
DECAY_PIECES = 3
SUM_ROWS = 16
CLS_ROWS = 32
VMEM_LIMIT = 48 * 1024 * 1024


def _cparams(sem, flags=None):
    return pltpu.CompilerParams(dimension_semantics=sem, vmem_limit_bytes=VMEM_LIMIT, flags=flags)


def _bdot(a, b):
    return jnp.dot(a.astype(BF16), b.astype(BF16), preferred_element_type=F32)


def _split(a):
    hi = a.astype(BF16)
    lo = (a - hi.astype(F32)).astype(BF16)
    return hi, lo


_NN = (((1,), (0,)), ((), ()))
_NT = (((1,), (1,)), ((), ()))


def _dot3(a, b, dims=_NN):
    ah, al = _split(a)
    bh, bl = _split(b)
    d = lambda x, y: lax.dot_general(x, y, dims, preferred_element_type=F32)
    return d(ah, bh) + (d(ah, bl) + d(al, bh))


def _dot2_exact_rhs(a, b_bf16):
    ah, al = _split(a)
    return (jnp.dot(ah, b_bf16, preferred_element_type=F32)
            + jnp.dot(al, b_bf16, preferred_element_type=F32))


def _rms_mod(x, gain, shift, scale):
    ms = jnp.mean(x * x, axis=-1, keepdims=True)
    return (x * lax.rsqrt(ms + EPS)) * (gain * (1.0 + scale)) + shift


def _head_rms(z, s_ref, st_ref, gain):
    ss = jnp.dot((z * z).astype(BF16), s_ref[...], preferred_element_type=F32)
    inv = lax.rsqrt(ss * (1.0 / HEAD_DIM) + EPS)
    invf = _dot2_exact_rhs(inv, st_ref[...])
    return z * invf * gain


def _query_tile(t):
    return 512 if t % 512 == 0 else t


def _pieces_share_key_lanes(t):
    return _query_tile(t) < 2 * MXU_COLS


def _piece_lane(h, i, shared):
    if shared:
        return jnp.where(h % 2 == 0, HEAD_DIM, 0) + DECAY_PIECES * (h // 2) + i
    return DECAY_PIECES * h + i


def _query_blocks(q, shared):
    lane = lax.broadcasted_iota(jnp.int32, (1, LANES), 1)
    blocks = []
    for h in range(N_HEADS):
        pair = q[:, (h // 2) * LANES:(h // 2 + 1) * LANES]
        in_head = (lane >= HEAD_DIM) if h % 2 else (lane < HEAD_DIM)
        rest = 0.0
        if shared:
            first = (0 if h % 2 else HEAD_DIM) + DECAY_PIECES * (h // 2)
            rest = jnp.where((lane >= first) & (lane < first + DECAY_PIECES), 1.0, 0.0)
        blocks.append(jnp.where(in_head, pair, rest))
    return blocks


_HI_MASK = 0xFFFF0000


def _pack_bf16_pairs(x):
    bits = lambda a: lax.bitcast_convert_type(a.astype(BF16).astype(F32), jnp.uint32)
    half = x.shape[1] // 2
    word = (bits(x[:, :half]) >> 16) | (bits(x[:, half:]) & jnp.uint32(_HI_MASK))
    return lax.bitcast_convert_type(word, F32)


def _unpack_bf16_pairs(w):
    word = lax.bitcast_convert_type(w, jnp.uint32)
    lo = lax.bitcast_convert_type(word << 16, F32).astype(BF16)
    hi = lax.bitcast_convert_type(word & jnp.uint32(_HI_MASK), F32).astype(BF16)
    return jnp.concatenate([lo, hi], axis=1)


def _pad_rows(a, rows):
    if a.shape[0] == rows:
        return a
    return jnp.concatenate([a, jnp.zeros((rows - a.shape[0], a.shape[1]), a.dtype)], axis=0)


def _route(lt, br):
    m = jnp.max(lt, axis=0, keepdims=True)
    p = jnp.exp(lt - m)
    scores = p / jnp.sum(p, axis=0, keepdims=True)
    sel = scores + br
    row = lambda a, e: a[e:e + 1, :]
    gs = []
    for g in range(N_EXPERT_GROUPS):
        v = [row(sel, g * EXPERTS_PER_GROUP + j) for j in range(EXPERTS_PER_GROUP)]
        best = None
        for i in range(EXPERTS_PER_GROUP):
            for j in range(i + 1, EXPERTS_PER_GROUP):
                s = v[i] + v[j]
                best = s if best is None else jnp.maximum(best, s)
        gs.append(best)
    bg = jnp.zeros_like(gs[0])
    bv = gs[0]
    for g in range(1, N_EXPERT_GROUPS):
        better = gs[g] > bv
        bg = jnp.where(better, float(g), bg)
        bv = jnp.where(better, gs[g], bv)

    def in_group(a, j):
        out = row(a, j)
        for g in range(1, N_EXPERT_GROUPS):
            out = jnp.where(bg == float(g), row(a, g * EXPERTS_PER_GROUP + j), out)
        return out

    sg = [in_group(sel, j) for j in range(EXPERTS_PER_GROUP)]
    cg = [in_group(scores, j) for j in range(EXPERTS_PER_GROUP)]

    def first_argmax(vals):
        mx = vals[0]
        for v in vals[1:]:
            mx = jnp.maximum(mx, v)
        idx = jnp.full_like(mx, float(len(vals) - 1))
        for j in range(len(vals) - 2, -1, -1):
            idx = jnp.where(vals[j] == mx, float(j), idx)
        return idx

    i1 = first_argmax(sg)
    i2 = first_argmax([jnp.where(i1 == float(j), -jnp.inf, sg[j]) for j in range(EXPERTS_PER_GROUP)])
    lo = jnp.minimum(i1, i2)
    hi = jnp.maximum(i1, i2)

    def pick(vals, idx):
        out = vals[0]
        for j in range(1, len(vals)):
            out = jnp.where(idx == float(j), vals[j], out)
        return out

    c_lo = pick(cg, lo)
    c_hi = pick(cg, hi)
    tot = c_lo + c_hi
    pair = jnp.where(lo == 0.0, hi - 1.0, jnp.where(lo == 1.0, hi + 1.0, 5.0))
    return bg * float(N_PAIRS) + pair, c_lo / tot, c_hi / tot


def _moe_prep(x1, ng2, sh2, sc2, wrt_ref, br_ref, run_ref, pay_ref, info_ref, cnt_ref):
    tm = x1.shape[0]
    tr = max(tm, LANES)
    h2 = _rms_mod(x1, ng2, sh2, sc2)
    lt = _dot3(wrt_ref[...], _pad_rows(h2, tr), _NT)
    cls, w_lo, w_hi = _route(lt, br_ref[...])

    r = lax.broadcasted_iota(jnp.int32, (LANES, tr), 0)
    wrows = jnp.where(r == 0, w_lo, jnp.where(r == 1, w_hi, 0.0))
    pay_ref[:, :PACK_W] = _pack_bf16_pairs(h2)
    pay_ref[:, PACK_W:] = wrows.T[:tm, :]

    crow = lax.broadcasted_iota(jnp.int32, (CLS_ROWS, tr), 0).astype(F32)
    lane = lax.broadcasted_iota(jnp.int32, (CLS_ROWS, tr), 1)
    onehot = jnp.where((crow == cls) & (lane < tm), 1.0, 0.0)
    us = lax.broadcasted_iota(jnp.int32, (tr, tr), 0)
    ut = lax.broadcasted_iota(jnp.int32, (tr, tr), 1)
    upper = jnp.where(us < ut, 1.0, 0.0).astype(BF16)
    before = jnp.dot(onehot.astype(BF16), upper, preferred_element_type=F32) + run_ref[:, 0:1]
    rank = jnp.sum(onehot * before, axis=0, keepdims=True)
    run_new = run_ref[...] + jnp.sum(onehot, axis=1, keepdims=True)
    run_ref[...] = run_new
    cnt_ref[...] = run_new
    ir = lax.broadcasted_iota(jnp.int32, (8, tr), 0)
    info = jnp.where(ir == 0, cls, jnp.where(ir == 1, rank, 0.0)).astype(jnp.int32)
    info_ref[...] = info[:, :tm]


def _ada_body(c_ref, w_ref, b_ref, o_ref):
    c = c_ref[...]
    o_ref[...] = _dot3(c * jax.nn.sigmoid(c), w_ref[...]) + b_ref[...]


def _ada_call(c_all, w, b):
    n_l, _, n_out = w.shape
    bc = c_all.shape[0]
    tn = 1536 if n_out % 1536 == 0 else 1024
    return pl.pallas_call(
        _ada_body,
        grid=(n_l, n_out // tn),
        in_specs=[pl.BlockSpec((bc, D_MODEL), lambda l, j: (0, 0)),
                  pl.BlockSpec((None, D_MODEL, tn), lambda l, j: (l, 0, j)),
                  pl.BlockSpec((None, 1, tn), lambda l, j: (l, 0, j))],
        out_specs=pl.BlockSpec((None, bc, tn), lambda l, j: (l, 0, j)),
        out_shape=jax.ShapeDtypeStruct((n_l, bc, n_out), F32),
        compiler_params=_cparams(("arbitrary", "arbitrary")),
        name="adaln",
    )(c_all, w, b.reshape(n_l, 1, n_out))


def _tile_rows(t, rows=512):
    return rows if t % rows == 0 else t


def _tok_spec(tm, width=D_MODEL):
    return pl.BlockSpec((None, tm, width), lambda b, t: (b, t, 0))


def _per_batch_spec(rows, width=D_MODEL):
    return pl.BlockSpec((None, rows, width), lambda b, t: (b, 0, 0))


def _const_spec(shape):
    nd = len(shape)
    return pl.BlockSpec(shape, lambda b, t: (0,) * nd)


def _prep_out(bsz, t, tm):
    shapes = [jax.ShapeDtypeStruct((bsz, t, PAY_W), F32),
              jax.ShapeDtypeStruct((bsz, 8, t), jnp.int32),
              jax.ShapeDtypeStruct((CLS_ROWS, LANES), F32)]
    specs = [_tok_spec(tm, PAY_W),
             pl.BlockSpec((None, 8, tm), lambda b, t: (b, 0, t)),
             _const_spec((CLS_ROWS, LANES))]
    return shapes, specs


def _mixer_body(has_res, tm, start_pos, *refs):
    it = iter(refs)
    x_ref, xp_ref = next(it), next(it)
    if has_res:
        y_ref, yp_ref, g2p_ref = next(it), next(it), next(it)
    hist_ref, ada_ref, ng_ref, wp_ref, ps_ref, wrt_ref, br_ref = (next(it) for _ in range(7))
    x1_ref, pay_ref, info_ref, cnt_ref, hout_ref = (next(it) for _ in range(5))
    run_ref = next(it)

    b = pl.program_id(0)
    t = pl.program_id(1)

    @pl.when((b == 0) & (t == 0))
    def _():
        run_ref[...] = jnp.zeros_like(run_ref)

    xin = x_ref[...]
    xp = xp_ref[...]
    if has_res:
        g2p = g2p_ref[...]
        xin = xin + g2p * y_ref[...]
        xp = xp + g2p * yp_ref[...]
    ada = ada_ref[...]
    sh1, sc1, g1, sh2, sc2 = (ada[i:i + 1] for i in range(5))
    ng = ng_ref[...]
    u = _rms_mod(xin, ng[0:1], sh1, sc1)
    up = _rms_mod(xp, ng[0:1], sh1, sc1)
    up = jnp.where(t == 0, hist_ref[...], up)
    level = jnp.concatenate([up, u], axis=0)
    sums = []
    for g, w in enumerate(POOL_WINDOWS):
        level = level + pltpu.roll(level, w // 2, 0)
        sums.append(level[HIST_ROWS:, :POOL_GROUP_DIM])
        if g + 1 < N_POOL_GROUPS:
            level = level[:, POOL_GROUP_DIM:]

    pos = start_pos + t * tm + lax.broadcasted_iota(jnp.int32, (tm, 1), 0)
    cols = []
    for g, w in enumerate(POOL_WINDOWS):
        sl = slice(g * POOL_GROUP_DIM, (g + 1) * POOL_GROUP_DIM)
        cnt = jnp.minimum(pos + 1, w).astype(F32)
        cols.append(_bdot(sums[g] / cnt - u[:, sl], wp_ref[g]))
    x1 = xin + (g1 * ps_ref[...]) * jnp.concatenate(cols, axis=1)
    x1_ref[...] = x1
    hout_ref[...] = u[tm - POOL_BUF:, :]
    _moe_prep(x1, ng[1:2], sh2, sc2, wrt_ref, br_ref, run_ref, pay_ref, info_ref, cnt_ref)


def _mixer_call(x, res, hist16, ada_l, ng_l, wp_l, ps_l, wrt, br, start_pos):
    bsz, t, _ = x.shape
    tm = _tile_rows(t)
    prev_spec = pl.BlockSpec((None, HIST_ROWS, D_MODEL),
                             lambda b, i: (b, jnp.maximum(i * (tm // HIST_ROWS) - 1, 0), 0))
    ins = [x, x]
    specs = [_tok_spec(tm), prev_spec]
    if res is not None:
        y, g2p = res
        ins += [y, y, g2p]
        specs += [_tok_spec(tm), prev_spec, _per_batch_spec(1)]
    ins += [hist16, ada_l, ng_l, wp_l, ps_l, wrt, br]
    specs += [_per_batch_spec(HIST_ROWS), _per_batch_spec(6), _const_spec((2, D_MODEL)),
              _const_spec((N_POOL_GROUPS, POOL_GROUP_DIM, POOL_GROUP_DIM)), _const_spec((1, D_MODEL)),
              _const_spec((N_EXPERTS, D_MODEL)), _const_spec((N_EXPERTS, 1))]
    p_shapes, p_specs = _prep_out(bsz, t, tm)
    return pl.pallas_call(
        functools.partial(_mixer_body, res is not None, tm, start_pos),
        grid=(bsz, t // tm),
        in_specs=specs,
        out_specs=[_tok_spec(tm)] + p_specs + [_per_batch_spec(POOL_BUF)],
        out_shape=[jax.ShapeDtypeStruct((bsz, t, D_MODEL), F32)] + p_shapes
                  + [jax.ShapeDtypeStruct((bsz, POOL_BUF, D_MODEL), F32)],
        scratch_shapes=[pltpu.VMEM((CLS_ROWS, LANES), F32)],
        compiler_params=_cparams(("arbitrary", "arbitrary")),
        name="pool_mixer",
    )(*ins)


def _proj_body(with_kv, tm, shared, *refs):
    it = iter(refs)
    x1p_ref, y_ref, g2p_ref, ada_ref, ng_ref, wq_ref, qn_ref, s_ref, st_ref = (next(it) for _ in range(9))
    if with_kv:
        kva_ref, kvn_ref, wkv_ref, wf_ref, bf_ref, kn_ref = (next(it) for _ in range(6))
    x_ref, q_ref = next(it), next(it)
    if with_kv:
        k_ref, v_ref, kb_ref, vt_ref, lf_ref, lfw_ref = (next(it) for _ in range(6))

    x = x1p_ref[...] + g2p_ref[...] * y_ref[...]
    x_ref[...] = x
    ada = ada_ref[...]
    ng = ng_ref[...]
    h = _rms_mod(x, ng[0:1], ada[0:1], ada[1:2])
    q = _head_rms(_bdot(h, wq_ref[...]), s_ref, st_ref, qn_ref[...])
    rows = max(tm, LANES)
    for i, blk in enumerate(_query_blocks(q * (ATTN_SCALE * LOG2E), shared)):
        q_ref[i * LANES:(i + 1) * LANES, :] = _pad_rows(blk, rows).T[:, :tm].astype(BF16)
    if with_kv:
        kva = kva_ref[...]
        hk = _rms_mod(x, kvn_ref[...], kva[0:1], kva[1:2])
        proj = _bdot(hk, wkv_ref[...])
        k = _head_rms(proj[:, :KV_WIDTH], s_ref, st_ref, kn_ref[...])
        v = proj[:, KV_WIDTH:]
        k_ref[...] = k
        v_ref[...] = v
        kb_ref[...] = k.astype(BF16)
        vt_ref[...] = _pad_rows(v, max(tm, LANES)).T[:, :tm].astype(BF16)
        z = _dot3(hk, wf_ref[...]) + bf_ref[...]
        lf = jnp.minimum(z, 0.0) - jnp.log(1.0 + jnp.exp(-jnp.abs(z)))
        lf_ref[...] = lf[:, :N_HEADS]
        lane = lax.broadcasted_iota(jnp.int32, (1, LANES), 1)
        lfw_ref[...] = jnp.where(lane < N_HEADS, lf, 0.0)


def _proj_call(x1p, y, g2p, ada_l, ng_l, wq, qn, s_mat, st_mat, kv=None):
    bsz, t, _ = x1p.shape
    tm = _tile_rows(t, 256)
    ins = [x1p, y, g2p, ada_l, ng_l, wq, qn, s_mat, st_mat]
    specs = [_tok_spec(tm), _tok_spec(tm), _per_batch_spec(1), _per_batch_spec(6), _const_spec((2, D_MODEL)),
             _const_spec((D_MODEL, KV_WIDTH)), _const_spec((1, D_MODEL)),
             _const_spec((D_MODEL, LANES)), _const_spec((LANES, D_MODEL))]
    out_shapes = [jax.ShapeDtypeStruct((bsz, t, D_MODEL), F32), jax.ShapeDtypeStruct((bsz, PAD_W, t), BF16)]
    out_specs = [_tok_spec(tm), pl.BlockSpec((None, PAD_W, tm), lambda b, i: (b, 0, i))]
    if kv is not None:
        kva, kvn, wkv, wf, bf, kn = kv
        ins += [kva, kvn, wkv, wf, bf, kn]
        specs += [_per_batch_spec(2), _const_spec((1, D_MODEL)), _const_spec((D_MODEL, 2 * KV_WIDTH)),
                  _const_spec((D_MODEL, LANES)), _const_spec((1, LANES)), _const_spec((1, D_MODEL))]
        out_shapes += [jax.ShapeDtypeStruct((bsz, t, D_MODEL), F32)] * 2
        out_shapes += [jax.ShapeDtypeStruct((bsz, t, D_MODEL), BF16), jax.ShapeDtypeStruct((bsz, D_MODEL, t), BF16)]
        out_shapes += [jax.ShapeDtypeStruct((bsz, t, N_HEADS), F32), jax.ShapeDtypeStruct((bsz, t, LANES), F32)]
        out_specs += [_tok_spec(tm)] * 2
        out_specs += [_tok_spec(tm), pl.BlockSpec((None, D_MODEL, tm), lambda b, i: (b, 0, i))]
        out_specs += [_tok_spec(tm, N_HEADS), _tok_spec(tm, LANES)]
    return pl.pallas_call(
        functools.partial(_proj_body, kv is not None, tm, _pieces_share_key_lanes(t)),
        grid=(bsz, t // tm),
        in_specs=specs, out_specs=out_specs, out_shape=out_shapes,
        compiler_params=_cparams(("arbitrary", "arbitrary")),
        name="qkv_proj" if kv is not None else "q_proj",
    )(*ins)


def _decay_body(tc, nb, shared, lf_ref, o_ref, carry_ref):
    @pl.when(pl.program_id(1) == 0)
    def _():
        carry_ref[...] = jnp.zeros_like(carry_ref)

    lf = jnp.concatenate([lf_ref[i] for i in range(nb)], axis=1)
    r = lax.broadcasted_iota(jnp.int32, (tc, tc), 0)
    c = lax.broadcasted_iota(jnp.int32, (tc, tc), 1)
    lower = jnp.where(r >= c, 1.0, 0.0).astype(BF16)
    hi, lo = _split(lf)
    f = (jnp.dot(lower, hi, preferred_element_type=F32) + jnp.dot(lower, lo, preferred_element_type=F32)
         + carry_ref[0:1, :])
    carry_ref[...] = jnp.broadcast_to(f[tc - 1:tc, :], carry_ref.shape)
    bias = f * (-LOG2E)
    p1 = bias.astype(BF16)
    r1 = bias - p1.astype(F32)
    p2 = r1.astype(BF16)
    p3 = (r1 - p2.astype(F32)).astype(BF16)
    hr = lax.broadcasted_iota(jnp.int32, (LANES, LANES), 0)
    lc = lax.broadcasted_iota(jnp.int32, (LANES, LANES), 1)
    places = [jnp.where((lc == _piece_lane(hr, i, shared)) & (hr < N_HEADS), 1.0, 0.0).astype(BF16)
              for i in range(DECAY_PIECES)]
    for b in range(nb):
        out = None
        for p, place in zip((p1, p2, p3), places):
            term = jnp.dot(p[:, b * LANES:(b + 1) * LANES], place, preferred_element_type=F32)
            out = term if out is None else out + term
        o_ref[b] = out.astype(BF16)


def _decay_call(lfw, shared):
    bsz, tk, _ = lfw.shape
    tc = 512
    nb = 8 if bsz % 8 == 0 else 1
    spec = pl.BlockSpec((nb, tc, LANES), lambda b, t: (b, t, 0))
    return pl.pallas_call(
        functools.partial(_decay_body, tc, nb, shared),
        grid=(bsz // nb, tk // tc), in_specs=[spec], out_specs=spec,
        out_shape=jax.ShapeDtypeStruct(lfw.shape, BF16),
        scratch_shapes=[pltpu.VMEM((8, nb * LANES), F32)],
        compiler_params=_cparams(("arbitrary", "arbitrary")),
        name="decay_bias",
    )(lfw)


def _transpose_body(v_ref, o_ref):
    o_ref[...] = v_ref[...].T.astype(BF16)


def _transpose_call(v):
    bsz, p, _ = v.shape
    tp = 512
    return pl.pallas_call(
        _transpose_body, grid=(bsz, p // tp),
        in_specs=[pl.BlockSpec((None, tp, D_MODEL), lambda b, t: (b, t, 0))],
        out_specs=pl.BlockSpec((None, D_MODEL, tp), lambda b, t: (b, 0, t)),
        out_shape=jax.ShapeDtypeStruct((bsz, D_MODEL, p), BF16),
        compiler_params=_cparams(("arbitrary", "arbitrary")),
        name="value_transpose",
    )(v)


def _flash_body(tq, tkp, tks, n_past, n_self, *refs):
    it = iter(refs)
    q_ref = next(it)
    if n_past:
        kp_ref, vtp_ref, fpp_ref = next(it), next(it), next(it)
    ks_ref, vts_ref, fps_ref = next(it), next(it), next(it)
    o_ref, m_ref, l_ref, acc_ref = next(it), next(it), next(it), next(it)
    shared = tq < 2 * MXU_COLS
    qi = pl.program_id(1)
    step = pl.program_id(2)

    @pl.when(step == 0)
    def _():
        m_ref[...] = jnp.full_like(m_ref, NEG_INF)
        l_ref[...] = jnp.zeros_like(l_ref)
        acc_ref[...] = jnp.zeros_like(acc_ref)

    def process(k_ref, vt_ref, fp_ref, tk, key_base):
        fp = fp_ref[...]
        lane = lax.broadcasted_iota(jnp.int32, (1, LANES), 1)
        piece_row = lax.broadcasted_iota(jnp.int32, (LANES, 1), 0)
        if key_base is not None:
            kpos = key_base + lax.broadcasted_iota(jnp.int32, (tk, 1), 0)
            qpos = qi * tq + lax.broadcasted_iota(jnp.int32, (1, tq), 1)
            visible = kpos <= qpos

        pair_keys = {}

        def logits(h):
            hp = h // 2
            if hp not in pair_keys:
                pair_keys[hp] = k_ref[:, hp * LANES:(hp + 1) * LANES].astype(BF16)
            if shared:
                own = (lane >= HEAD_DIM) if h % 2 else (lane < HEAD_DIM)
                lhs = jnp.where(own, pair_keys[hp], fp)
                rhs = q_ref[h * LANES:(h + 1) * LANES, :]
            else:
                d = piece_row - DECAY_PIECES * h
                ones = jnp.where((d >= 0) & (d < DECAY_PIECES), 1.0, 0.0).astype(BF16)
                lhs = jnp.concatenate([pair_keys[hp], fp], axis=1)
                rhs = jnp.concatenate([q_ref[h * LANES:(h + 1) * LANES, :],
                                       jnp.broadcast_to(ones, (LANES, tq))], axis=0)
            if tq < 2 * MXU_COLS and tk % 2 == 0:
                half = tk // 2
                s = jnp.concatenate([jnp.dot(lhs[:half], rhs, preferred_element_type=F32),
                                     jnp.dot(lhs[half:], rhs, preferred_element_type=F32)], axis=0)
            else:
                s = jnp.dot(lhs, rhs, preferred_element_type=F32)
            return s if key_base is None else jnp.where(visible, s, NEG_INF)

        ones_rows = jnp.ones((SUM_ROWS, tk), BF16)
        s_next = logits(0)
        for h in range(N_HEADS):
            s = s_next
            if h + 1 < N_HEADS:
                s_next = logits(h + 1)
            rows = slice(h * HEAD_DIM, (h + 1) * HEAD_DIM)
            m_old = m_ref[h]
            m_new = jnp.maximum(m_old, jnp.max(s, axis=0, keepdims=True))
            alpha = jnp.exp2(m_old - m_new)
            p = jnp.exp2(s - m_new).astype(BF16)
            m_ref[h] = m_new
            pv = jnp.dot(jnp.concatenate([vt_ref[rows, :], ones_rows], axis=0), p,
                         preferred_element_type=F32)
            l_ref[h] = alpha * l_ref[h] + pv[HEAD_DIM:HEAD_DIM + 1, :]
            acc_ref[rows, :] = alpha * acc_ref[rows, :] + pv[:HEAD_DIM, :]

    if n_past:
        @pl.when(step < n_past)
        def _():
            process(kp_ref, vtp_ref, fpp_ref, tkp, None)

    j = step - n_past
    last = (qi * tq + tq - 1) // tks
    has_hidden = (j + 1) * tks - 1 > qi * tq

    @pl.when((j >= 0) & (j <= last) & has_hidden)
    def _():
        process(ks_ref, vts_ref, fps_ref, tks, j * tks)

    @pl.when((j >= 0) & (j <= last) & jnp.logical_not(has_hidden))
    def _():
        process(ks_ref, vts_ref, fps_ref, tks, None)

    @pl.when(step == n_past + n_self - 1)
    def _():
        row = lax.broadcasted_iota(jnp.int32, (LANES, 1), 0)
        if tq % LANES:
            r = lax.broadcasted_iota(jnp.int32, (tq, tq), 0)
            c = lax.broadcasted_iota(jnp.int32, (tq, tq), 1)
            eye = jnp.where(r == c, 1.0, 0.0).astype(BF16)
        for hp in range(N_HEADS // 2):
            denom = jnp.where(row < HEAD_DIM, l_ref[2 * hp], l_ref[2 * hp + 1])
            o_t = acc_ref[hp * LANES:(hp + 1) * LANES, :] / denom
            if tq % LANES:
                o = lax.dot_general(eye, o_t.astype(BF16), _NT, preferred_element_type=F32)
            else:
                o = o_t.T
            o_ref[:, hp * LANES:(hp + 1) * LANES] = o.astype(BF16)


def _flash_call(q, k_self, vt_self, past, fp, n_past_keys):
    bsz, _, t_q = q.shape
    tq = _query_tile(t_q)
    tks = tq
    n_self = t_q // tks
    tkp = 512
    n_past = n_past_keys // tkp
    fp_self_base = n_past_keys // tks

    def self_idx(i, s):
        return jnp.clip(s - n_past, 0, (i * tq + tq - 1) // tks)

    ins = [q]
    specs = [pl.BlockSpec((None, PAD_W, tq), lambda b, i, s: (b, 0, i))]
    if n_past:
        past_idx = lambda s: jnp.minimum(s, n_past - 1)
        ins += [past[0], past[1], fp]
        specs += [pl.BlockSpec((None, tkp, D_MODEL), lambda b, i, s: (b, past_idx(s), 0)),
                  pl.BlockSpec((None, D_MODEL, tkp), lambda b, i, s: (b, 0, past_idx(s))),
                  pl.BlockSpec((None, tkp, LANES), lambda b, i, s: (b, past_idx(s), 0))]
    ins += [k_self, vt_self, fp]
    specs += [pl.BlockSpec((None, tks, D_MODEL), lambda b, i, s: (b, self_idx(i, s), 0)),
              pl.BlockSpec((None, D_MODEL, tks), lambda b, i, s: (b, 0, self_idx(i, s))),
              pl.BlockSpec((None, tks, LANES), lambda b, i, s: (b, fp_self_base + self_idx(i, s), 0))]
    return pl.pallas_call(
        functools.partial(_flash_body, tq, tkp, tks, n_past, n_self),
        grid=(bsz, t_q // tq, n_past + n_self),
        in_specs=specs,
        out_specs=pl.BlockSpec((None, tq, D_MODEL), lambda b, i, s: (b, i, 0)),
        out_shape=jax.ShapeDtypeStruct((bsz, t_q, D_MODEL), BF16),
        scratch_shapes=[pltpu.VMEM((N_HEADS, 1, tq), F32), pltpu.VMEM((N_HEADS, 1, tq), F32),
                        pltpu.VMEM((D_MODEL, tq), F32)],
        compiler_params=_cparams(("arbitrary", "arbitrary", "arbitrary")),
        name="fox_attention",
    )(*ins)


def _oproj_body(x_ref, o_ref, wo_ref, ada_ref, ng_ref, wrt_ref, br_ref,
                x1_ref, pay_ref, info_ref, cnt_ref, run_ref):
    @pl.when((pl.program_id(0) == 0) & (pl.program_id(1) == 0))
    def _():
        run_ref[...] = jnp.zeros_like(run_ref)

    ada = ada_ref[...]
    x1 = x_ref[...] + ada[2:3] * jnp.dot(o_ref[...], wo_ref[...], preferred_element_type=F32)
    x1_ref[...] = x1
    _moe_prep(x1, ng_ref[...][1:2], ada[3:4], ada[4:5], wrt_ref, br_ref, run_ref, pay_ref, info_ref, cnt_ref)


def _oproj_call(x, o, wo, ada_l, ng_l, wrt, br):
    bsz, t, _ = x.shape
    tm = _tile_rows(t)
    p_shapes, p_specs = _prep_out(bsz, t, tm)
    return pl.pallas_call(
        _oproj_body,
        grid=(bsz, t // tm),
        in_specs=[_tok_spec(tm), _tok_spec(tm), _const_spec((KV_WIDTH, D_MODEL)), _per_batch_spec(6),
                  _const_spec((2, D_MODEL)), _const_spec((N_EXPERTS, D_MODEL)), _const_spec((N_EXPERTS, 1))],
        out_specs=[_tok_spec(tm)] + p_specs,
        out_shape=[jax.ShapeDtypeStruct((bsz, t, D_MODEL), F32)] + p_shapes,
        scratch_shapes=[pltpu.VMEM((CLS_ROWS, LANES), F32)],
        compiler_params=_cparams(("arbitrary", "arbitrary")),
        name="attn_out_proj",
    )(x, o, wo, ada_l, ng_l, wrt, br)


def _final_body(x_ref, y_ref, g_ref, o_ref):
    o_ref[...] = x_ref[...] + g_ref[...] * y_ref[...]


def _final_call(x1, y, g2):
    bsz, t, _ = x1.shape
    tm = _tile_rows(t, 1024)
    return pl.pallas_call(
        _final_body, grid=(bsz, t // tm),
        in_specs=[_tok_spec(tm), _tok_spec(tm), _per_batch_spec(1)],
        out_specs=_tok_spec(tm),
        out_shape=jax.ShapeDtypeStruct(x1.shape, F32),
        input_output_aliases={0: 0},
        compiler_params=_cparams(("arbitrary", "arbitrary")),
        name="final_residual",
    )(x1, y, g2)


def _sc_worker_loop(n_win, fn):
    wid = lax.axis_index("s") * SC_CORES + lax.axis_index("c")
    n_workers = SC_CORES * SC_SUBCORES

    @pl.loop(0, pl.cdiv(n_win, n_workers))
    def _(j):
        win = j * n_workers + wid

        @pl.when(win < n_win)
        def _():
            fn(pl.multiple_of(win * SC_WINDOW, SC_WINDOW))


def _sc_scatter_rows(rows, idx, n_out):
    n, width = rows.shape
    mesh = plsc.VectorSubcoreMesh(core_axis_name="c", subcore_axis_name="s")

    @functools.partial(
        pl.kernel, mesh=mesh, out_type=jax.ShapeDtypeStruct((n_out, width), rows.dtype),
        scratch_types=[pltpu.VMEM((SC_WINDOW,), jnp.int32), pltpu.VMEM((SC_WINDOW, width), rows.dtype)])
    def k(rows_hbm, idx_hbm, out_hbm, idx_v, rows_v):
        def one(base):
            pltpu.sync_copy(idx_hbm.at[pl.ds(base, SC_WINDOW)], idx_v)
            pltpu.sync_copy(rows_hbm.at[pl.ds(base, SC_WINDOW)], rows_v)
            pltpu.sync_copy(rows_v, out_hbm.at[idx_v])
        _sc_worker_loop(n // SC_WINDOW, one)

    return k(rows, idx)


def _sc_gather_rows(table, idx):
    n = idx.shape[0]
    width = table.shape[1]
    mesh = plsc.VectorSubcoreMesh(core_axis_name="c", subcore_axis_name="s")

    @functools.partial(
        pl.kernel, mesh=mesh, out_type=jax.ShapeDtypeStruct((n, width), table.dtype),
        scratch_types=[pltpu.VMEM((SC_WINDOW,), jnp.int32), pltpu.VMEM((SC_WINDOW, width), table.dtype)])
    def k(table_hbm, idx_hbm, out_hbm, idx_v, rows_v):
        def one(base):
            pltpu.sync_copy(idx_hbm.at[pl.ds(base, SC_WINDOW)], idx_v)
            pltpu.sync_copy(table_hbm.at[idx_v], rows_v)
            pltpu.sync_copy(rows_v, out_hbm.at[pl.ds(base, SC_WINDOW)])
        _sc_worker_loop(n // SC_WINDOW, one)

    return k(table, idx)


def _moe_body(e1_ref, e2_ref, na_ref, hs_ref, g1_ref, g2_ref, u1_ref, u2_ref, d1_ref, d2_ref, o_ref):
    @pl.when(pl.program_id(0) < na_ref[0])
    def _():
        blk = hs_ref[...]
        h = _unpack_bf16_pairs(blk[:, :PACK_W])
        y = None
        for lane, (g_ref, u_ref, d_ref) in enumerate(((g1_ref, u1_ref, d1_ref), (g2_ref, u2_ref, d2_ref))):
            gate = jnp.dot(h, g_ref[...], preferred_element_type=F32)
            up = jnp.dot(h, u_ref[...], preferred_element_type=F32)
            w = blk[:, PACK_W + lane:PACK_W + lane + 1]
            act = (gate * jax.nn.sigmoid(gate) * up * w).astype(BF16)
            term = jnp.dot(act, d_ref[...], preferred_element_type=F32)
            y = term if y is None else y + term
        o_ref[...] = y


def _moe_call(hs, tile_e1, tile_e2, n_active, wg, wu, wd, layer, tm):
    n_s = hs.shape[0]
    n_tiles = n_s // tm
    base = layer * N_EXPERTS

    def row_map(i, e1, e2, na):
        return (jnp.minimum(i, na[0] - 1), 0)

    def w_map(which):
        def m(i, e1, e2, na):
            e = (e1, e2)[which]
            return (base + e[jnp.minimum(i, na[0] - 1)], 0, 0)
        return m

    gu = lambda which: pl.BlockSpec((None, D_MODEL, D_EXPERT), w_map(which))
    dn = lambda which: pl.BlockSpec((None, D_EXPERT, D_MODEL), w_map(which))
    return pl.pallas_call(
        _moe_body,
        grid_spec=pltpu.PrefetchScalarGridSpec(
            num_scalar_prefetch=3, grid=(n_tiles,),
            in_specs=[pl.BlockSpec((tm, PAY_W), row_map), gu(0), gu(1), gu(0), gu(1), dn(0), dn(1)],
            out_specs=pl.BlockSpec((tm, D_MODEL), row_map)),
        out_shape=jax.ShapeDtypeStruct((n_s, D_MODEL), F32),
        compiler_params=_cparams(("arbitrary",)),
        name="grouped_experts",
    )(tile_e1, tile_e2, n_active, hs, wg, wg, wu, wu, wd, wd)


_PAIR_LO = (0, 0, 0, 1, 1, 2)
_PAIR_HI = (1, 2, 3, 2, 3, 3)


def _moe_layer(pay, info, counts, wg, wu, wd, layer):
    bsz, t, _ = pay.shape
    n = bsz * t
    tm = 256
    n_s =((n + N_CLASSES * (tm - 1)) // tm + 1) * tm
    n_tiles = n_s // tm
    cls = info[:, 0, :].reshape(n)
    rank = info[:, 1, :].reshape(n)
    cnt = counts[:N_CLASSES, 0].astype(jnp.int32)
    padded = ((cnt + tm - 1) // tm) * tm
    ends = jnp.cumsum(padded)
    starts = ends - padded
    dest = starts[cls] + rank
    tile_start = jnp.arange(n_tiles, dtype=jnp.int32) * tm
    tile_cls = jnp.minimum(jnp.sum((tile_start[:, None] >= ends[None, :]).astype(jnp.int32), axis=1),
                           N_CLASSES - 1)
    grp = tile_cls // N_PAIRS
    pr = tile_cls % N_PAIRS
    tile_e1 = grp * EXPERTS_PER_GROUP + jnp.asarray(_PAIR_LO, jnp.int32)[pr]
    tile_e2 = grp * EXPERTS_PER_GROUP + jnp.asarray(_PAIR_HI, jnp.int32)[pr]
    n_active = (ends[-1:] // tm).astype(jnp.int32)
    hs = _sc_scatter_rows(pay.reshape(n, PAY_W), dest, n_s)
    yield
    ys = _moe_call(hs, tile_e1, tile_e2, n_active, wg, wu, wd, layer, tm)
    yield
    return _sc_gather_rows(ys, dest).reshape(bsz, t, D_MODEL)


def _alternate(first, second):
    live = {0: first, 1: second}
    results = {}

    def advance(i):
        try:
            next(live[i])
        except StopIteration as done:
            results[i] = done.value
            del live[i]

    advance(0)
    while live:
        for i in (0, 1):
            if i in live:
                advance(i)
    return results[0], results[1]


def _trunk(x, ada, kva, hist, past, prm):
    bsz, t, _ = x.shape
    start_pos = 0 if past is None else past[0].shape[1]
    wrt, br = prm["wrt"], prm["br"]
    res = None
    new_hist = []
    for layer in range(N_A_LAYERS):
        if hist is None:
            h16 = jnp.zeros((bsz, HIST_ROWS, D_MODEL), F32)
        else:
            h16 = jnp.pad(hist[layer], ((0, 0), (1, 0), (0, 0)))
        x, pay, info, counts, hout = _mixer_call(
            x, res, h16, ada[layer], prm["norm_g"][layer], prm["w_pool"][layer], prm["pool_scale"][layer],
            wrt, br, start_pos)
        new_hist.append(hout)
        yield
        y = yield from _moe_layer(pay, info, counts, prm["w_gate"], prm["w_up"], prm["w_down"], layer)
        yield
        res = (y, ada[layer][:, 5:6, :])

    shared = None
    for j in range(N_B_LAYERS):
        layer = N_A_LAYERS + j
        y, g2p = res
        if j == 0:
            kv = (kva, prm["kv_norm"], prm["w_kv"], prm["w_f"], prm["b_f"], prm["k_norm"])
            xr, q, k, v, kb, vt, logf, lfw = _proj_call(
                x, y, g2p, ada[layer], prm["norm_g"][layer], prm["w_q"][j], prm["q_norm"][j],
                prm["s_mat"], prm["st_mat"], kv)
            if past is None:
                past_kv = None
                fp = _decay_call(lfw, _pieces_share_key_lanes(t))
            else:
                ck, cv, clogf = past
                n_past = ck.shape[1]
                pad = -(n_past + t) % 512
                lfw = jnp.concatenate([jnp.pad(clogf, ((0, 0), (0, 0), (0, LANES - N_HEADS))), lfw,
                                       jnp.zeros((bsz, pad, LANES), F32)], axis=1)
                fp = _decay_call(lfw, _pieces_share_key_lanes(t))
                past_kv = (ck.reshape(bsz, n_past, D_MODEL), _transpose_call(cv.reshape(bsz, n_past, D_MODEL)))
            shared = (k, v, logf, kb, vt, past_kv, fp)
        else:
            xr, q = _proj_call(x, y, g2p, ada[layer], prm["norm_g"][layer], prm["w_q"][j], prm["q_norm"][j],
                               prm["s_mat"], prm["st_mat"])
        yield
        o = _flash_call(q, shared[3], shared[4], shared[5], shared[6], start_pos)
        yield
        x, pay, info, counts = _oproj_call(xr, o, prm["w_o"][j], ada[layer], prm["norm_g"][layer], wrt, br)
        yield
        y = yield from _moe_layer(pay, info, counts, prm["w_gate"], prm["w_up"], prm["w_down"], layer)
        yield
        res = (y, ada[layer][:, 5:6, :])

    out = _final_call(x, res[0], res[1])
    k, v, logf = shared[:3]
    return (out, jnp.stack(new_hist), k.reshape(bsz, t, N_HEADS, HEAD_DIM),
            v.reshape(bsz, t, N_HEADS, HEAD_DIM), logf)


def kernel(x_prompt, x_sample, cache_pool, cache_k, cache_v, cache_logf, c_prompt, c_sample, ada_w, ada_b, norm_g, w_pool, pool_scale, kv_ada_w, kv_ada_b, kv_norm, w_kvf, b_f, k_norm, w_q, q_norm, w_o, w_router, b_router, w_gate, w_up, w_down):
    bp = x_prompt.shape[0]
    c_all = jnp.concatenate([c_prompt, c_sample], axis=0)
    bc = c_all.shape[0]
    ada = _ada_call(c_all, ada_w, ada_b).reshape(DEPTH, bc, 6, D_MODEL)
    kva = _ada_call(c_all, kv_ada_w[None], kv_ada_b[None]).reshape(bc, 2, D_MODEL)

    head_of_lane = jnp.arange(D_MODEL, dtype=jnp.int32) // HEAD_DIM
    s_mat = (head_of_lane[:, None] == jnp.arange(LANES, dtype=jnp.int32)[None, :]).astype(BF16)
    prm = {
        "norm_g": norm_g,
        "w_pool": w_pool.astype(BF16),
        "pool_scale": pool_scale.reshape(N_A_LAYERS, 1, D_MODEL),
        "kv_norm": kv_norm.reshape(1, D_MODEL),
        "w_kv": w_kvf[:, :2 * KV_WIDTH].astype(BF16),
        "w_f": jnp.pad(w_kvf[:, 2 * KV_WIDTH:], ((0, 0), (0, LANES - N_HEADS))),
        "b_f": jnp.pad(b_f, (0, LANES - N_HEADS)).reshape(1, LANES),
        "k_norm": jnp.tile(k_norm, N_HEADS).reshape(1, D_MODEL),
        "w_q": w_q.astype(BF16),
        "q_norm": jnp.tile(q_norm, (1, N_HEADS)).reshape(N_B_LAYERS, 1, D_MODEL),
        "w_o": w_o.astype(BF16),
        "wrt": w_router.T,
        "br": b_router.reshape(N_EXPERTS, 1),
        "w_gate": w_gate.astype(BF16).reshape(DEPTH * N_EXPERTS, D_MODEL, D_EXPERT),
        "w_up": w_up.astype(BF16).reshape(DEPTH * N_EXPERTS, D_MODEL, D_EXPERT),
        "w_down": w_down.astype(BF16).reshape(DEPTH * N_EXPERTS, D_EXPERT, D_MODEL),
        "s_mat": s_mat,
        "st_mat": s_mat.T,
    }
    outs_p, outs_s = _alternate(
        _trunk(x_prompt, ada[:, :bp], kva[:bp], None, None, prm),
        _trunk(x_sample, ada[:, bp:], kva[bp:], cache_pool, (cache_k, cache_v, cache_logf), prm))
    return (outs_p[0], outs_s[0]) + outs_p[1:] + outs_s[1:]
```

```python
import functools

import jax
import jax.numpy as jnp
from jax import lax
from jax.experimental import pallas as pl
from jax.experimental.pallas import tpu as pltpu
from jax.experimental.pallas import tpu_sc as plsc

F32 = jnp.float32
BF16 = jnp.bfloat16

D_MODEL = 1024
DEPTH = 4
N_A_LAYERS = DEPTH // 2
N_B_LAYERS = DEPTH - N_A_LAYERS
POOL_WINDOWS = (2, 4, 8, 16)
N_POOL_GROUPS = len(POOL_WINDOWS)
POOL_GROUP_DIM = D_MODEL // N_POOL_GROUPS
POOL_BUF = max(POOL_WINDOWS) - 1
HIST_ROWS = POOL_BUF + 1
N_HEADS = 16
HEAD_DIM = D_MODEL // N_HEADS
KV_WIDTH = N_HEADS * HEAD_DIM
ATTN_SCALE = HEAD_DIM ** -0.5
N_EXPERTS = 16
N_EXPERT_GROUPS = 4
EXPERTS_PER_GROUP = N_EXPERTS // N_EXPERT_GROUPS
N_PAIRS = 6
N_CLASSES = N_EXPERT_GROUPS * N_PAIRS
D_EXPERT = D_MODEL // 2
EPS = 1e-6
NEG_INF = -1e30

LANES = 128
MXU_COLS = 256
SC_CORES = 2
SC_SUBCORES = 16
SC_WINDOW = 64
PACK_W = D_MODEL // 2
PAY_W = PACK_W + LANES
PAD_W = N_HEADS * LANES
LOG2E = 1.4426950408889634
DECAY_PIECES = 3
SUM_ROWS = 16
CLS_ROWS = 32
VMEM_LIMIT = 48 * 1024 * 1024


def _cparams(sem, flags=None):
    return pltpu.CompilerParams(dimension_semantics=sem, vmem_limit_bytes=VMEM_LIMIT, flags=flags)


def _bdot(a, b):
    return jnp.dot(a.astype(BF16), b.astype(BF16), preferred_element_type=F32)


def _split(a):
    hi = a.astype(BF16)
    lo = (a - hi.astype(F32)).astype(BF16)
    return hi, lo


_NN = (((1,), (0,)), ((), ()))
_NT = (((1,), (1,)), ((), ()))


def _dot3(a, b, dims=_NN):
    ah, al = _split(a)
    bh, bl = _split(b)
    d = lambda x, y: lax.dot_general(x, y, dims, preferred_element_type=F32)
    return d(ah, bh) + (d(ah, bl) + d(al, bh))


def _dot2_exact_rhs(a, b_bf16):
    ah, al = _split(a)
    return (jnp.dot(ah, b_bf16, preferred_element_type=F32)
            + jnp.dot(al, b_bf16, preferred_element_type=F32))


def _rms_mod(x, gain, shift, scale):
    ms = jnp.mean(x * x, axis=-1, keepdims=True)
    return (x * lax.rsqrt(ms + EPS)) * (gain * (1.0 + scale)) + shift


def _head_rms(z, s_ref, st_ref, gain):
    ss = jnp.dot((z * z).astype(BF16), s_ref[...], preferred_element_type=F32)
    inv = lax.rsqrt(ss * (1.0 / HEAD_DIM) + EPS)
    invf = _dot2_exact_rhs(inv, st_ref[...])
    return z * invf * gain


def _query_tile(t):
    return 512 if t % 512 == 0 else t


def _pieces_share_key_lanes(t):
    return _query_tile(t) < 2 * MXU_COLS


def _piece_lane(h, i, shared):
    if shared:
        return jnp.where(h % 2 == 0, HEAD_DIM, 0) + DECAY_PIECES * (h // 2) + i
    return DECAY_PIECES * h + i


def _query_blocks(q, shared):
    lane = lax.broadcasted_iota(jnp.int32, (1, LANES), 1)
    blocks = []
    for h in range(N_HEADS):
        pair = q[:, (h // 2) * LANES:(h // 2 + 1) * LANES]
        in_head = (lane >= HEAD_DIM) if h % 2 else (lane < HEAD_DIM)
        rest = 0.0
        if shared:
            first = (0 if h % 2 else HEAD_DIM) + DECAY_PIECES * (h // 2)
            rest = jnp.where((lane >= first) & (lane < first + DECAY_PIECES), 1.0, 0.0)
        blocks.append(jnp.where(in_head, pair, rest))
    return blocks


_HI_MASK = 0xFFFF0000


def _pack_bf16_pairs(x):
    bits = lambda a: lax.bitcast_convert_type(a.astype(BF16).astype(F32), jnp.uint32)
    half = x.shape[1] // 2
    word = (bits(x[:, :half]) >> 16) | (bits(x[:, half:]) & jnp.uint32(_HI_MASK))
    return lax.bitcast_convert_type(word, F32)


def _unpack_bf16_pairs(w):
    word = lax.bitcast_convert_type(w, jnp.uint32)
    lo = lax.bitcast_convert_type(word << 16, F32).astype(BF16)
    hi = lax.bitcast_convert_type(word & jnp.uint32(_HI_MASK), F32).astype(BF16)
    return jnp.concatenate([lo, hi], axis=1)


def _pad_rows(a, rows):
    if a.shape[0] == rows:
        return a
    return jnp.concatenate([a, jnp.zeros((rows - a.shape[0], a.shape[1]), a.dtype)], axis=0)


def _route(lt, br):
    m = jnp.max(lt, axis=0, keepdims=True)
    p = jnp.exp(lt - m)
    scores = p / jnp.sum(p, axis=0, keepdims=True)
    sel = scores + br
    row = lambda a, e: a[e:e + 1, :]
    gs = []
    for g in range(N_EXPERT_GROUPS):
        v = [row(sel, g * EXPERTS_PER_GROUP + j) for j in range(EXPERTS_PER_GROUP)]
        best = None
        for i in range(EXPERTS_PER_GROUP):
            for j in range(i + 1, EXPERTS_PER_GROUP):
                s = v[i] + v[j]
                best = s if best is None else jnp.maximum(best, s)
        gs.append(best)
    bg = jnp.zeros_like(gs[0])
    bv = gs[0]
    for g in range(1, N_EXPERT_GROUPS):
        better = gs[g] > bv
        bg = jnp.where(better, float(g), bg)
        bv = jnp.where(better, gs[g], bv)

    def in_group(a, j):
        out = row(a, j)
        for g in range(1, N_EXPERT_GROUPS):
            out = jnp.where(bg == float(g), row(a, g * EXPERTS_PER_GROUP + j), out)
        return out

    sg = [in_group(sel, j) for j in range(EXPERTS_PER_GROUP)]
    cg = [in_group(scores, j) for j in range(EXPERTS_PER_GROUP)]

    def first_argmax(vals):
        mx = vals[0]
        for v in vals[1:]:
            mx = jnp.maximum(mx, v)
        idx = jnp.full_like(mx, float(len(vals) - 1))
        for j in range(len(vals) - 2, -1, -1):
            idx = jnp.where(vals[j] == mx, float(j), idx)
        return idx

    i1 = first_argmax(sg)
    i2 = first_argmax([jnp.where(i1 == float(j), -jnp.inf, sg[j]) for j in range(EXPERTS_PER_GROUP)])
    lo = jnp.minimum(i1, i2)
    hi = jnp.maximum(i1, i2)

    def pick(vals, idx):
        out = vals[0]
        for j in range(1, len(vals)):
            out = jnp.where(idx == float(j), vals[j], out)
        return out

    c_lo = pick(cg, lo)
    c_hi = pick(cg, hi)
    tot = c_lo + c_hi
    pair = jnp.where(lo == 0.0, hi - 1.0, jnp.where(lo == 1.0, hi + 1.0, 5.0))
    return bg * float(N_PAIRS) + pair, c_lo / tot, c_hi / tot


def _moe_prep(x1, ng2, sh2, sc2, wrt_ref, br_ref, run_ref, pay_ref, info_ref, cnt_ref):
    tm = x1.shape[0]
    tr = max(tm, LANES)
    h2 = _rms_mod(x1, ng2, sh2, sc2)
    lt = _dot3(wrt_ref[...], _pad_rows(h2, tr), _NT)
    cls, w_lo, w_hi = _route(lt, br_ref[...])

    r = lax.broadcasted_iota(jnp.int32, (LANES, tr), 0)
    wrows = jnp.where(r == 0, w_lo, jnp.where(r == 1, w_hi, 0.0))
    pay_ref[:, :PACK_W] = _pack_bf16_pairs(h2)
    pay_ref[:, PACK_W:] = wrows.T[:tm, :]

    crow = lax.broadcasted_iota(jnp.int32, (CLS_ROWS, tr), 0).astype(F32)
    lane = lax.broadcasted_iota(jnp.int32, (CLS_ROWS, tr), 1)
    onehot = jnp.where((crow == cls) & (lane < tm), 1.0, 0.0)
    us = lax.broadcasted_iota(jnp.int32, (tr, tr), 0)
    ut = lax.broadcasted_iota(jnp.int32, (tr, tr), 1)
    upper = jnp.where(us < ut, 1.0, 0.0).astype(BF16)
    before = jnp.dot(onehot.astype(BF16), upper, preferred_element_type=F32) + run_ref[:, 0:1]
    rank = jnp.sum(onehot * before, axis=0, keepdims=True)
    run_new = run_ref[...] + jnp.sum(onehot, axis=1, keepdims=True)
    run_ref[...] = run_new
    cnt_ref[...] = run_new
    ir = lax.broadcasted_iota(jnp.int32, (8, tr), 0)
    info = jnp.where(ir == 0, cls, jnp.where(ir == 1, rank, 0.0)).astype(jnp.int32)
    info_ref[...] = info[:, :tm]


def _ada_body(c_ref, w_ref, b_ref, o_ref):
    c = c_ref[...]
    o_ref[...] = _dot3(c * jax.nn.sigmoid(c), w_ref[...]) + b_ref[...]


def _ada_call(c_all, w, b):
    n_l, _, n_out = w.shape
    bc = c_all.shape[0]
    tn = 1536 if n_out % 1536 == 0 else 1024
    return pl.pallas_call(
        _ada_body,
        grid=(n_l, n_out // tn),
        in_specs=[pl.BlockSpec((bc, D_MODEL), lambda l, j: (0, 0)),
                  pl.BlockSpec((None, D_MODEL, tn), lambda l, j: (l, 0, j)),
                  pl.BlockSpec((None, 1, tn), lambda l, j: (l, 0, j))],
        out_specs=pl.BlockSpec((None, bc, tn), lambda l, j: (l, 0, j)),
        out_shape=jax.ShapeDtypeStruct((n_l, bc, n_out), F32),
        compiler_params=_cparams(("arbitrary", "arbitrary")),
        name="adaln",
    )(c_all, w, b.reshape(n_l, 1, n_out))


def _tile_rows(t, rows=512):
    return rows if t % rows == 0 else t


def _tok_spec(tm, width=D_MODEL):
    return pl.BlockSpec((None, tm, width), lambda b, t: (b, t, 0))


def _per_batch_spec(rows, width=D_MODEL):
    return pl.BlockSpec((None, rows, width), lambda b, t: (b, 0, 0))


def _const_spec(shape):
    nd = len(shape)
    return pl.BlockSpec(shape, lambda b, t: (0,) * nd)


def _prep_out(bsz, t, tm):
    shapes = [jax.ShapeDtypeStruct((bsz, t, PAY_W), F32),
              jax.ShapeDtypeStruct((bsz, 8, t), jnp.int32),
              jax.ShapeDtypeStruct((CLS_ROWS, LANES), F32)]
    specs = [_tok_spec(tm, PAY_W),
             pl.BlockSpec((None, 8, tm), lambda b, t: (b, 0, t)),
             _const_spec((CLS_ROWS, LANES))]
    return shapes, specs


def _mixer_body(has_res, tm, start_pos, *refs):
    it = iter(refs)
    x_ref, xp_ref = next(it), next(it)
    if has_res:
        y_ref, yp_ref, g2p_ref = next(it), next(it), next(it)
    hist_ref, ada_ref, ng_ref, wp_ref, ps_ref, wrt_ref, br_ref = (next(it) for _ in range(7))
    x1_ref, pay_ref, info_ref, cnt_ref, hout_ref = (next(it) for _ in range(5))
    run_ref = next(it)

    b = pl.program_id(0)
    t = pl.program_id(1)

    @pl.when((b == 0) & (t == 0))
    def _():
        run_ref[...] = jnp.zeros_like(run_ref)

    xin = x_ref[...]
    xp = xp_ref[...]
    if has_res:
        g2p = g2p_ref[...]
        xin = xin + g2p * y_ref[...]
        xp = xp + g2p * yp_ref[...]
    ada = ada_ref[...]
    sh1, sc1, g1, sh2, sc2 = (ada[i:i + 1] for i in range(5))
    ng = ng_ref[...]
    u = _rms_mod(xin, ng[0:1], sh1, sc1)
    up = _rms_mod(xp, ng[0:1], sh1, sc1)
    up = jnp.where(t == 0, hist_ref[...], up)
    level = jnp.concatenate([up, u], axis=0)
    sums = []
    for g, w in enumerate(POOL_WINDOWS):
        level = level + pltpu.roll(level, w // 2, 0)
        sums.append(level[HIST_ROWS:, :POOL_GROUP_DIM])
        if g + 1 < N_POOL_GROUPS:
            level = level[:, POOL_GROUP_DIM:]

    pos = start_pos + t * tm + lax.broadcasted_iota(jnp.int32, (tm, 1), 0)
    cols = []
    for g, w in enumerate(POOL_WINDOWS):
        sl = slice(g * POOL_GROUP_DIM, (g + 1) * POOL_GROUP_DIM)
        cnt = jnp.minimum(pos + 1, w).astype(F32)
        cols.append(_bdot(sums[g] / cnt - u[:, sl], wp_ref[g]))
    x1 = xin + (g1 * ps_ref[...]) * jnp.concatenate(cols, axis=1)
    x1_ref[...] = x1
    hout_ref[...] = u[tm - POOL_BUF:, :]
    _moe_prep(x1, ng[1:2], sh2, sc2, wrt_ref, br_ref, run_ref, pay_ref, info_ref, cnt_ref)


def _mixer_call(x, res, hist16, ada_l, ng_l, wp_l, ps_l, wrt, br, start_pos):
    bsz, t, _ = x.shape
    tm = _tile_rows(t)
    prev_spec = pl.BlockSpec((None, HIST_ROWS, D_MODEL),
                             lambda b, i: (b, jnp.maximum(i * (tm // HIST_ROWS) - 1, 0), 0))
    ins = [x, x]
    specs = [_tok_spec(tm), prev_spec]
    if res is not None:
        y, g2p = res
        ins += [y, y, g2p]
        specs += [_tok_spec(tm), prev_spec, _per_batch_spec(1)]
    ins += [hist16, ada_l, ng_l, wp_l, ps_l, wrt, br]
    specs += [_per_batch_spec(HIST_ROWS), _per_batch_spec(6), _const_spec((2, D_MODEL)),
              _const_spec((N_POOL_GROUPS, POOL_GROUP_DIM, POOL_GROUP_DIM)), _const_spec((1, D_MODEL)),
              _const_spec((N_EXPERTS, D_MODEL)), _const_spec((N_EXPERTS, 1))]
    p_shapes, p_specs = _prep_out(bsz, t, tm)
    return pl.pallas_call(
        functools.partial(_mixer_body, res is not None, tm, start_pos),
        grid=(bsz, t // tm),
        in_specs=specs,
        out_specs=[_tok_spec(tm)] + p_specs + [_per_batch_spec(POOL_BUF)],
        out_shape=[jax.ShapeDtypeStruct((bsz, t, D_MODEL), F32)] + p_shapes
                  + [jax.ShapeDtypeStruct((bsz, POOL_BUF, D_MODEL), F32)],
        scratch_shapes=[pltpu.VMEM((CLS_ROWS, LANES), F32)],
        compiler_params=_cparams(("arbitrary", "arbitrary")),
        name="pool_mixer",
    )(*ins)


def _proj_body(with_kv, tm, shared, *refs):
    it = iter(refs)
    x1p_ref, y_ref, g2p_ref, ada_ref, ng_ref, wq_ref, qn_ref, s_ref, st_ref = (next(it) for _ in range(9))
    if with_kv:
        kva_ref, kvn_ref, wkv_ref, wf_ref, bf_ref, kn_ref = (next(it) for _ in range(6))
    x_ref, q_ref = next(it), next(it)
    if with_kv:
        k_ref, v_ref, kb_ref, vt_ref, lf_ref, lfw_ref = (next(it) for _ in range(6))

    x = x1p_ref[...] + g2p_ref[...] * y_ref[...]
    x_ref[...] = x
    ada = ada_ref[...]
    ng = ng_ref[...]
    h = _rms_mod(x, ng[0:1], ada[0:1], ada[1:2])
    q = _head_rms(_bdot(h, wq_ref[...]), s_ref, st_ref, qn_ref[...])
    rows = max(tm, LANES)
    for i, blk in enumerate(_query_blocks(q * (ATTN_SCALE * LOG2E), shared)):
        q_ref[i * LANES:(i + 1) * LANES, :] = _pad_rows(blk, rows).T[:, :tm].astype(BF16)
    if with_kv:
        kva = kva_ref[...]
        hk = _rms_mod(x, kvn_ref[...], kva[0:1], kva[1:2])
        proj = _bdot(hk, wkv_ref[...])
        k = _head_rms(proj[:, :KV_WIDTH], s_ref, st_ref, kn_ref[...])
        v = proj[:, KV_WIDTH:]
        k_ref[...] = k
        v_ref[...] = v
        kb_ref[...] = k.astype(BF16)
        vt_ref[...] = _pad_rows(v, max(tm, LANES)).T[:, :tm].astype(BF16)
        z = _dot3(hk, wf_ref[...]) + bf_ref[...]
        lf = jnp.minimum(z, 0.0) - jnp.log(1.0 + jnp.exp(-jnp.abs(z)))
        lf_ref[...] = lf[:, :N_HEADS]
        lane = lax.broadcasted_iota(jnp.int32, (1, LANES), 1)
        lfw_ref[...] = jnp.where(lane < N_HEADS, lf, 0.0)


def _proj_call(x1p, y, g2p, ada_l, ng_l, wq, qn, s_mat, st_mat, kv=None):
    bsz, t, _ = x1p.shape
    tm = _tile_rows(t, 256)
    ins = [x1p, y, g2p, ada_l, ng_l, wq, qn, s_mat, st_mat]
    specs = [_tok_spec(tm), _tok_spec(tm), _per_batch_spec(1), _per_batch_spec(6), _const_spec((2, D_MODEL)),
             _const_spec((D_MODEL, KV_WIDTH)), _const_spec((1, D_MODEL)),
             _const_spec((D_MODEL, LANES)), _const_spec((LANES, D_MODEL))]
    out_shapes = [jax.ShapeDtypeStruct((bsz, t, D_MODEL), F32), jax.ShapeDtypeStruct((bsz, PAD_W, t), BF16)]
    out_specs = [_tok_spec(tm), pl.BlockSpec((None, PAD_W, tm), lambda b, i: (b, 0, i))]
    if kv is not None:
        kva, kvn, wkv, wf, bf, kn = kv
        ins += [kva, kvn, wkv, wf, bf, kn]
        specs += [_per_batch_spec(2), _const_spec((1, D_MODEL)), _const_spec((D_MODEL, 2 * KV_WIDTH)),
                  _const_spec((D_MODEL, LANES)), _const_spec((1, LANES)), _const_spec((1, D_MODEL))]
        out_shapes += [jax.ShapeDtypeStruct((bsz, t, D_MODEL), F32)] * 2
        out_shapes += [jax.ShapeDtypeStruct((bsz, t, D_MODEL), BF16), jax.ShapeDtypeStruct((bsz, D_MODEL, t), BF16)]
        out_shapes += [jax.ShapeDtypeStruct((bsz, t, N_HEADS), F32), jax.ShapeDtypeStruct((bsz, t, LANES), F32)]
        out_specs += [_tok_spec(tm)] * 2
        out_specs += [_tok_spec(tm), pl.BlockSpec((None, D_MODEL, tm), lambda b, i: (b, 0, i))]
        out_specs += [_tok_spec(tm, N_HEADS), _tok_spec(tm, LANES)]
    return pl.pallas_call(
        functools.partial(_proj_body, kv is not None, tm, _pieces_share_key_lanes(t)),
        grid=(bsz, t // tm),
        in_specs=specs, out_specs=out_specs, out_shape=out_shapes,
        compiler_params=_cparams(("arbitrary", "arbitrary")),
        name="qkv_proj" if kv is not None else "q_proj",
    )(*ins)


def _decay_body(tc, nb, shared, lf_ref, o_ref, carry_ref):
    @pl.when(pl.program_id(1) == 0)
    def _():
        carry_ref[...] = jnp.zeros_like(carry_ref)

    lf = jnp.concatenate([lf_ref[i] for i in range(nb)], axis=1)
    r = lax.broadcasted_iota(jnp.int32, (tc, tc), 0)
    c = lax.broadcasted_iota(jnp.int32, (tc, tc), 1)
    lower = jnp.where(r >= c, 1.0, 0.0).astype(BF16)
    hi, lo = _split(lf)
    f = (jnp.dot(lower, hi, preferred_element_type=F32) + jnp.dot(lower, lo, preferred_element_type=F32)
         + carry_ref[0:1, :])
    carry_ref[...] = jnp.broadcast_to(f[tc - 1:tc, :], carry_ref.shape)
    bias = f * (-LOG2E)
    p1 = bias.astype(BF16)
    r1 = bias - p1.astype(F32)
    p2 = r1.astype(BF16)
    p3 = (r1 - p2.astype(F32)).astype(BF16)
    hr = lax.broadcasted_iota(jnp.int32, (LANES, LANES), 0)
    lc = lax.broadcasted_iota(jnp.int32, (LANES, LANES), 1)
    places = [jnp.where((lc == _piece_lane(hr, i, shared)) & (hr < N_HEADS), 1.0, 0.0).astype(BF16)
              for i in range(DECAY_PIECES)]
    for b in range(nb):
        out = None
        for p, place in zip((p1, p2, p3), places):
            term = jnp.dot(p[:, b * LANES:(b + 1) * LANES], place, preferred_element_type=F32)
            out = term if out is None else out + term
        o_ref[b] = out.astype(BF16)


def _decay_call(lfw, shared):
    bsz, tk, _ = lfw.shape
    tc = 512
    nb = 8 if bsz % 8 == 0 else 1
    spec = pl.BlockSpec((nb, tc, LANES), lambda b, t: (b, t, 0))
    return pl.pallas_call(
        functools.partial(_decay_body, tc, nb, shared),
        grid=(bsz // nb, tk // tc), in_specs=[spec], out_specs=spec,
        out_shape=jax.ShapeDtypeStruct(lfw.shape, BF16),
        scratch_shapes=[pltpu.VMEM((8, nb * LANES), F32)],
        compiler_params=_cparams(("arbitrary", "arbitrary")),
        name="decay_bias",
    )(lfw)


def _transpose_body(v_ref, o_ref):
    o_ref[...] = v_ref[...].T.astype(BF16)


def _transpose_call(v):
    bsz, p, _ = v.shape
    tp = 512
    return pl.pallas_call(
        _transpose_body, grid=(bsz, p // tp),
        in_specs=[pl.BlockSpec((None, tp, D_MODEL), lambda b, t: (b, t, 0))],
        out_specs=pl.BlockSpec((None, D_MODEL, tp), lambda b, t: (b, 0, t)),
        out_shape=jax.ShapeDtypeStruct((bsz, D_MODEL, p), BF16),
        compiler_params=_cparams(("arbitrary", "arbitrary")),
        name="value_transpose",
    )(v)


def _flash_body(tq, tkp, tks, n_past, n_self, qi_tab, step_tab, *refs):
    it = iter(refs)
    q_ref = next(it)
    if n_past:
        kp_ref, vtp_ref, fpp_ref = next(it), next(it), next(it)
    ks_ref, vts_ref, fps_ref = next(it), next(it), next(it)
    o_ref, m_ref, l_ref, acc_ref = next(it), next(it), next(it), next(it)
    shared = tq < 2 * MXU_COLS
    qi = qi_tab[pl.program_id(1)]
    step = step_tab[pl.program_id(1)]

    @pl.when(step == 0)
    def _():
        m_ref[...] = jnp.full_like(m_ref, NEG_INF)
        l_ref[...] = jnp.zeros_like(l_ref)
        acc_ref[...] = jnp.zeros_like(acc_ref)

    def process(k_ref, vt_ref, fp_ref, tk, key_base):
        fp = fp_ref[...]
        lane = lax.broadcasted_iota(jnp.int32, (1, LANES), 1)
        piece_row = lax.broadcasted_iota(jnp.int32, (LANES, 1), 0)
        if key_base is not None:
            kpos = key_base + lax.broadcasted_iota(jnp.int32, (tk, 1), 0)
            qpos = qi * tq + lax.broadcasted_iota(jnp.int32, (1, tq), 1)
            visible = kpos <= qpos

        pair_keys = {}

        def logits(h):
            hp = h // 2
            if hp not in pair_keys:
                pair_keys[hp] = k_ref[:, hp * LANES:(hp + 1) * LANES].astype(BF16)
            if shared:
                own = (lane >= HEAD_DIM) if h % 2 else (lane < HEAD_DIM)
                lhs = jnp.where(own, pair_keys[hp], fp)
                rhs = q_ref[h * LANES:(h + 1) * LANES, :]
            else:
                d = piece_row - DECAY_PIECES * h
                ones = jnp.where((d >= 0) & (d < DECAY_PIECES), 1.0, 0.0).astype(BF16)
                lhs = jnp.concatenate([pair_keys[hp], fp], axis=1)
                rhs = jnp.concatenate([q_ref[h * LANES:(h + 1) * LANES, :],
                                       jnp.broadcast_to(ones, (LANES, tq))], axis=0)
            if tq < 2 * MXU_COLS and tk % 2 == 0:
                half = tk // 2
                s = jnp.concatenate([jnp.dot(lhs[:half], rhs, preferred_element_type=F32),
                                     jnp.dot(lhs[half:], rhs, preferred_element_type=F32)], axis=0)
            else:
                s = jnp.dot(lhs, rhs, preferred_element_type=F32)
            return s if key_base is None else jnp.where(visible, s, NEG_INF)

        ones_rows = jnp.ones((SUM_ROWS, tk), BF16)
        s_next = logits(0)
        for h in range(N_HEADS):
            s = s_next
            if h + 1 < N_HEADS:
                s_next = logits(h + 1)
            rows = slice(h * HEAD_DIM, (h + 1) * HEAD_DIM)
            m_old = m_ref[h]
            m_new = jnp.maximum(m_old, jnp.max(s, axis=0, keepdims=True))
            alpha = jnp.exp2(m_old - m_new)
            p = jnp.exp2(s - m_new).astype(BF16)
            m_ref[h] = m_new
            pv = jnp.dot(jnp.concatenate([vt_ref[rows, :], ones_rows], axis=0), p,
                         preferred_element_type=F32)
            l_ref[h] = alpha * l_ref[h] + pv[HEAD_DIM:HEAD_DIM + 1, :]
            acc_ref[rows, :] = alpha * acc_ref[rows, :] + pv[:HEAD_DIM, :]

    if n_past:
        @pl.when(step < n_past)
        def _():
            process(kp_ref, vtp_ref, fpp_ref, tkp, None)

    j = step - n_past
    last = (qi * tq + tq - 1) // tks
    has_hidden = (j + 1) * tks - 1 > qi * tq

    @pl.when((j >= 0) & (j <= last) & has_hidden)
    def _():
        process(ks_ref, vts_ref, fps_ref, tks, j * tks)

    @pl.when((j >= 0) & (j <= last) & jnp.logical_not(has_hidden))
    def _():
        process(ks_ref, vts_ref, fps_ref, tks, None)

    @pl.when(step == n_past + last)
    def _():
        row = lax.broadcasted_iota(jnp.int32, (LANES, 1), 0)
        if tq % LANES:
            r = lax.broadcasted_iota(jnp.int32, (tq, tq), 0)
            c = lax.broadcasted_iota(jnp.int32, (tq, tq), 1)
            eye = jnp.where(r == c, 1.0, 0.0).astype(BF16)
        for hp in range(N_HEADS // 2):
            denom = jnp.where(row < HEAD_DIM, l_ref[2 * hp], l_ref[2 * hp + 1])
            o_t = acc_ref[hp * LANES:(hp + 1) * LANES, :] / denom
            if tq % LANES:
                o = lax.dot_general(eye, o_t.astype(BF16), _NT, preferred_element_type=F32)
            else:
                o = o_t.T
            o_ref[:, hp * LANES:(hp + 1) * LANES] = o.astype(BF16)


def _flash_call(q, k_self, vt_self, past, fp, n_past_keys):
    bsz, _, t_q = q.shape
    tq = _query_tile(t_q)
    tks = tq
    n_self = t_q // tks
    tkp = 512
    n_past = n_past_keys // tkp
    fp_self_base = n_past_keys // tks

    def self_idx(i, s):
        return jnp.clip(s - n_past, 0, (i * tq + tq - 1) // tks)

    pairs = [(i, s) for i in range(t_q // tq) for s in range(n_past + (i * tq + tq - 1) // tks + 1)]
    qi_tab = jnp.asarray([p[0] for p in pairs], jnp.int32)
    step_tab = jnp.asarray([p[1] for p in pairs], jnp.int32)

    ins = [q]
    specs = [pl.BlockSpec((None, PAD_W, tq), lambda b, p, qt, st: (b, 0, qt[p]))]
    if n_past:
        past_idx = lambda s: jnp.minimum(s, n_past - 1)
        ins += [past[0], past[1], fp]
        specs += [pl.BlockSpec((None, tkp, D_MODEL), lambda b, p, qt, st: (b, past_idx(st[p]), 0)),
                  pl.BlockSpec((None, D_MODEL, tkp), lambda b, p, qt, st: (b, 0, past_idx(st[p]))),
                  pl.BlockSpec((None, tkp, LANES), lambda b, p, qt, st: (b, past_idx(st[p]), 0))]
    ins += [k_self, vt_self, fp]
    specs += [pl.BlockSpec((None, tks, D_MODEL), lambda b, p, qt, st: (b, self_idx(qt[p], st[p]), 0)),
              pl.BlockSpec((None, D_MODEL, tks), lambda b, p, qt, st: (b, 0, self_idx(qt[p], st[p]))),
              pl.BlockSpec((None, tks, LANES),
                           lambda b, p, qt, st: (b, fp_self_base + self_idx(qt[p], st[p]), 0))]
    return pl.pallas_call(
        functools.partial(_flash_body, tq, tkp, tks, n_past, n_self),
        grid_spec=pltpu.PrefetchScalarGridSpec(
            num_scalar_prefetch=2, grid=(bsz, len(pairs)),
            in_specs=specs,
            out_specs=pl.BlockSpec((None, tq, D_MODEL), lambda b, p, qt, st: (b, qt[p], 0)),
            scratch_shapes=[pltpu.VMEM((N_HEADS, 1, tq), F32), pltpu.VMEM((N_HEADS, 1, tq), F32),
                            pltpu.VMEM((D_MODEL, tq), F32)]),
        out_shape=jax.ShapeDtypeStruct((bsz, t_q, D_MODEL), BF16),
        compiler_params=_cparams(("arbitrary", "arbitrary")),
        name="fox_attention",
    )(qi_tab, step_tab, *ins)


def _oproj_body(x_ref, o_ref, wo_ref, ada_ref, ng_ref, wrt_ref, br_ref,
                x1_ref, pay_ref, info_ref, cnt_ref, run_ref):
    @pl.when((pl.program_id(0) == 0) & (pl.program_id(1) == 0))
    def _():
        run_ref[...] = jnp.zeros_like(run_ref)

    ada = ada_ref[...]
    x1 = x_ref[...] + ada[2:3] * jnp.dot(o_ref[...], wo_ref[...], preferred_element_type=F32)
    x1_ref[...] = x1
    _moe_prep(x1, ng_ref[...][1:2], ada[3:4], ada[4:5], wrt_ref, br_ref, run_ref, pay_ref, info_ref, cnt_ref)


def _oproj_call(x, o, wo, ada_l, ng_l, wrt, br):
    bsz, t, _ = x.shape
    tm = _tile_rows(t)
    p_shapes, p_specs = _prep_out(bsz, t, tm)
    return pl.pallas_call(
        _oproj_body,
        grid=(bsz, t // tm),
        in_specs=[_tok_spec(tm), _tok_spec(tm), _const_spec((KV_WIDTH, D_MODEL)), _per_batch_spec(6),
                  _const_spec((2, D_MODEL)), _const_spec((N_EXPERTS, D_MODEL)), _const_spec((N_EXPERTS, 1))],
        out_specs=[_tok_spec(tm)] + p_specs,
        out_shape=[jax.ShapeDtypeStruct((bsz, t, D_MODEL), F32)] + p_shapes,
        scratch_shapes=[pltpu.VMEM((CLS_ROWS, LANES), F32)],
        compiler_params=_cparams(("arbitrary", "arbitrary")),
        name="attn_out_proj",
    )(x, o, wo, ada_l, ng_l, wrt, br)


def _final_body(x_ref, y_ref, g_ref, o_ref):
    o_ref[...] = x_ref[...] + g_ref[...] * y_ref[...]


def _final_call(x1, y, g2):
    bsz, t, _ = x1.shape
    tm = _tile_rows(t, 1024)
    return pl.pallas_call(
        _final_body, grid=(bsz, t // tm),
        in_specs=[_tok_spec(tm), _tok_spec(tm), _per_batch_spec(1)],
        out_specs=_tok_spec(tm),
        out_shape=jax.ShapeDtypeStruct(x1.shape, F32),
        input_output_aliases={0: 0},
        compiler_params=_cparams(("arbitrary", "arbitrary")),
        name="final_residual",
    )(x1, y, g2)


def _sc_worker_loop(n_win, fn):
    wid = lax.axis_index("s") * SC_CORES + lax.axis_index("c")
    n_workers = SC_CORES * SC_SUBCORES

    @pl.loop(0, pl.cdiv(n_win, n_workers))
    def _(j):
        win = j * n_workers + wid

        @pl.when(win < n_win)
        def _():
            fn(pl.multiple_of(win * SC_WINDOW, SC_WINDOW))


def _sc_scatter_rows(rows, idx, n_out):
    n, width = rows.shape
    mesh = plsc.VectorSubcoreMesh(core_axis_name="c", subcore_axis_name="s")

    @functools.partial(
        pl.kernel, mesh=mesh, out_type=jax.ShapeDtypeStruct((n_out, width), rows.dtype),
        scratch_types=[pltpu.VMEM((SC_WINDOW,), jnp.int32), pltpu.VMEM((SC_WINDOW, width), rows.dtype)])
    def k(rows_hbm, idx_hbm, out_hbm, idx_v, rows_v):
        def one(base):
            pltpu.sync_copy(idx_hbm.at[pl.ds(base, SC_WINDOW)], idx_v)
            pltpu.sync_copy(rows_hbm.at[pl.ds(base, SC_WINDOW)], rows_v)
            pltpu.sync_copy(rows_v, out_hbm.at[idx_v])
        _sc_worker_loop(n // SC_WINDOW, one)

    return k(rows, idx)


def _sc_gather_rows(table, idx):
    n = idx.shape[0]
    width = table.shape[1]
    mesh = plsc.VectorSubcoreMesh(core_axis_name="c", subcore_axis_name="s")

    @functools.partial(
        pl.kernel, mesh=mesh, out_type=jax.ShapeDtypeStruct((n, width), table.dtype),
        scratch_types=[pltpu.VMEM((SC_WINDOW,), jnp.int32), pltpu.VMEM((SC_WINDOW, width), table.dtype)])
    def k(table_hbm, idx_hbm, out_hbm, idx_v, rows_v):
        def one(base):
            pltpu.sync_copy(idx_hbm.at[pl.ds(base, SC_WINDOW)], idx_v)
            pltpu.sync_copy(table_hbm.at[idx_v], rows_v)
            pltpu.sync_copy(rows_v, out_hbm.at[pl.ds(base, SC_WINDOW)])
        _sc_worker_loop(n // SC_WINDOW, one)

    return k(table, idx)


def _moe_body(e1_ref, e2_ref, na_ref, hs_ref, g1_ref, g2_ref, u1_ref, u2_ref, d1_ref, d2_ref, o_ref):
    @pl.when(pl.program_id(0) < na_ref[0])
    def _():
        blk = hs_ref[...]
        h = _unpack_bf16_pairs(blk[:, :PACK_W])
        y = None
        for lane, (g_ref, u_ref, d_ref) in enumerate(((g1_ref, u1_ref, d1_ref), (g2_ref, u2_ref, d2_ref))):
            gate = jnp.dot(h, g_ref[...], preferred_element_type=F32)
            up = jnp.dot(h, u_ref[...], preferred_element_type=F32)
            w = blk[:, PACK_W + lane:PACK_W + lane + 1]
            act = (gate * jax.nn.sigmoid(gate) * up * w).astype(BF16)
            term = jnp.dot(act, d_ref[...], preferred_element_type=F32)
            y = term if y is None else y + term
        o_ref[...] = y


def _moe_call(hs, tile_e1, tile_e2, n_active, wg, wu, wd, layer, tm):
    n_s = hs.shape[0]
    n_tiles = n_s // tm
    base = layer * N_EXPERTS

    def row_map(i, e1, e2, na):
        return (jnp.minimum(i, na[0] - 1), 0)

    def w_map(which):
        def m(i, e1, e2, na):
            e = (e1, e2)[which]
            return (base + e[jnp.minimum(i, na[0] - 1)], 0, 0)
        return m

    gu = lambda which: pl.BlockSpec((None, D_MODEL, D_EXPERT), w_map(which))
    dn = lambda which: pl.BlockSpec((None, D_EXPERT, D_MODEL), w_map(which))
    return pl.pallas_call(
        _moe_body,
        grid_spec=pltpu.PrefetchScalarGridSpec(
            num_scalar_prefetch=3, grid=(n_tiles,),
            in_specs=[pl.BlockSpec((tm, PAY_W), row_map), gu(0), gu(1), gu(0), gu(1), dn(0), dn(1)],
            out_specs=pl.BlockSpec((tm, D_MODEL), row_map)),
        out_shape=jax.ShapeDtypeStruct((n_s, D_MODEL), F32),
        compiler_params=_cparams(("arbitrary",)),
        name="grouped_experts",
    )(tile_e1, tile_e2, n_active, hs, wg, wg, wu, wu, wd, wd)


_PAIR_LO = (0, 0, 0, 1, 1, 2)
_PAIR_HI = (1, 2, 3, 2, 3, 3)


def _moe_layer(pay, info, counts, wg, wu, wd, layer):
    bsz, t, _ = pay.shape
    n = bsz * t
    tm = 512 if n >= 64 * 512 else 128
    n_s = ((n + N_CLASSES * (tm - 1)) // tm + 1) * tm
    n_tiles = n_s // tm
    cls = info[:, 0, :].reshape(n)
    rank = info[:, 1, :].reshape(n)
    cnt = counts[:N_CLASSES, 0].astype(jnp.int32)
    padded = ((cnt + tm - 1) // tm) * tm
    ends = jnp.cumsum(padded)
    starts = ends - padded
    dest = starts[cls] + rank
    tile_start = jnp.arange(n_tiles, dtype=jnp.int32) * tm
    tile_cls = jnp.minimum(jnp.sum((tile_start[:, None] >= ends[None, :]).astype(jnp.int32), axis=1),
                           N_CLASSES - 1)
    grp = tile_cls // N_PAIRS
    pr = tile_cls % N_PAIRS
    tile_e1 = grp * EXPERTS_PER_GROUP + jnp.asarray(_PAIR_LO, jnp.int32)[pr]
    tile_e2 = grp * EXPERTS_PER_GROUP + jnp.asarray(_PAIR_HI, jnp.int32)[pr]
    n_active = (ends[-1:] // tm).astype(jnp.int32)
    hs = _sc_scatter_rows(pay.reshape(n, PAY_W), dest, n_s)
    yield
    ys = _moe_call(hs, tile_e1, tile_e2, n_active, wg, wu, wd, layer, tm)
    yield
    return _sc_gather_rows(ys, dest).reshape(bsz, t, D_MODEL)


def _alternate(first, second):
    live = {0: first, 1: second}
    results = {}

    def advance(i):
        try:
            next(live[i])
        except StopIteration as done:
            results[i] = done.value
            del live[i]

    advance(0)
    while live:
        for i in (0, 1):
            if i in live:
                advance(i)
    return results[0], results[1]


def _trunk(x, ada, kva, hist, past, prm):
    bsz, t, _ = x.shape
    start_pos = 0 if past is None else past[0].shape[1]
    wrt, br = prm["wrt"], prm["br"]
    res = None
    new_hist = []
    for layer in range(N_A_LAYERS):
        if hist is None:
            h16 = jnp.zeros((bsz, HIST_ROWS, D_MODEL), F32)
        else:
            h16 = jnp.pad(hist[layer], ((0, 0), (1, 0), (0, 0)))
        x, pay, info, counts, hout = _mixer_call(
            x, res, h16, ada[layer], prm["norm_g"][layer], prm["w_pool"][layer], prm["pool_scale"][layer],
            wrt, br, start_pos)
        new_hist.append(hout)
        yield
        y = yield from _moe_layer(pay, info, counts, prm["w_gate"], prm["w_up"], prm["w_down"], layer)
        yield
        res = (y, ada[layer][:, 5:6, :])

    shared = None
    for j in range(N_B_LAYERS):
        layer = N_A_LAYERS + j
        y, g2p = res
        if j == 0:
            kv = (kva, prm["kv_norm"], prm["w_kv"], prm["w_f"], prm["b_f"], prm["k_norm"])
            xr, q, k, v, kb, vt, logf, lfw = _proj_call(
                x, y, g2p, ada[layer], prm["norm_g"][layer], prm["w_q"][j], prm["q_norm"][j],
                prm["s_mat"], prm["st_mat"], kv)
            if past is None:
                past_kv = None
                fp = _decay_call(lfw, _pieces_share_key_lanes(t))
            else:
                ck, cv, clogf = past
                n_past = ck.shape[1]
                pad = -(n_past + t) % 512
                lfw = jnp.concatenate([jnp.pad(clogf, ((0, 0), (0, 0), (0, LANES - N_HEADS))), lfw,
                                       jnp.zeros((bsz, pad, LANES), F32)], axis=1)
                fp = _decay_call(lfw, _pieces_share_key_lanes(t))
                past_kv = (ck.reshape(bsz, n_past, D_MODEL).astype(BF16),
                           _transpose_call(cv.reshape(bsz, n_past, D_MODEL)))
            shared = (k, v, logf, kb, vt, past_kv, fp)
        else:
            xr, q = _proj_call(x, y, g2p, ada[layer], prm["norm_g"][layer], prm["w_q"][j], prm["q_norm"][j],
                               prm["s_mat"], prm["st_mat"])
        yield
        o = _flash_call(q, shared[3], shared[4], shared[5], shared[6], start_pos)
        yield
        x, pay, info, counts = _oproj_call(xr, o, prm["w_o"][j], ada[layer], prm["norm_g"][layer], wrt, br)
        yield
        y = yield from _moe_layer(pay, info, counts, prm["w_gate"], prm["w_up"], prm["w_down"], layer)
        yield
        res = (y, ada[layer][:, 5:6, :])

    out = _final_call(x, res[0], res[1])
    k, v, logf = shared[:3]
    return (out, jnp.stack(new_hist), k.reshape(bsz, t, N_HEADS, HEAD_DIM),
            v.reshape(bsz, t, N_HEADS, HEAD_DIM), logf)


def kernel(x_prompt, x_sample, cache_pool, cache_k, cache_v, cache_logf, c_prompt, c_sample, ada_w, ada_b, norm_g, w_pool, pool_scale, kv_ada_w, kv_ada_b, kv_norm, w_kvf, b_f, k_norm, w_q, q_norm, w_o, w_router, b_router, w_gate, w_up, w_down):
    bp = x_prompt.shape[0]
    c_all = jnp.concatenate([c_prompt, c_sample], axis=0)
    bc = c_all.shape[0]
    ada = _ada_call(c_all, ada_w, ada_b).reshape(DEPTH, bc, 6, D_MODEL)
    kva = _ada_call(c_all, kv_ada_w[None], kv_ada_b[None]).reshape(bc, 2, D_MODEL)

    head_of_lane = jnp.arange(D_MODEL, dtype=jnp.int32) // HEAD_DIM
    s_mat = (head_of_lane[:, None] == jnp.arange(LANES, dtype=jnp.int32)[None, :]).astype(BF16)
    prm = {
        "norm_g": norm_g,
        "w_pool": w_pool.astype(BF16),
        "pool_scale": pool_scale.reshape(N_A_LAYERS, 1, D_MODEL),
        "kv_norm": kv_norm.reshape(1, D_MODEL),
        "w_kv": w_kvf[:, :2 * KV_WIDTH].astype(BF16),
        "w_f": jnp.pad(w_kvf[:, 2 * KV_WIDTH:], ((0, 0), (0, LANES - N_HEADS))),
        "b_f": jnp.pad(b_f, (0, LANES - N_HEADS)).reshape(1, LANES),
        "k_norm": jnp.tile(k_norm, N_HEADS).reshape(1, D_MODEL),
        "w_q": w_q.astype(BF16),
        "q_norm": jnp.tile(q_norm, (1, N_HEADS)).reshape(N_B_LAYERS, 1, D_MODEL),
        "w_o": w_o.astype(BF16),
        "wrt": w_router.T,
        "br": b_router.reshape(N_EXPERTS, 1),
        "w_gate": w_gate.astype(BF16).reshape(DEPTH * N_EXPERTS, D_MODEL, D_EXPERT),
        "w_up": w_up.astype(BF16).reshape(DEPTH * N_EXPERTS, D_MODEL, D_EXPERT),
        "w_down": w_down.astype(BF16).reshape(DEPTH * N_EXPERTS, D_EXPERT, D_MODEL),
        "s_mat": s_mat,
        "st_mat": s_mat.T,
    }
    outs_p, outs_s = _alternate(
        _trunk(x_prompt, ada[:, :bp], kva[:bp], None, None, prm),
        _trunk(x_sample, ada[:, bp:], kva[bp:], cache_pool, (cache_k, cache_v, cache_logf), prm))
    return (outs_p[0], outs_s[0]) + outs_p[1:] + outs_s[1:]
```
